```python
import math
import jax, jax.numpy as jnp
from jax import lax
import numpy as np

D_MODEL = 1024
BATCH = 4
SEQ = 8192
DEPTH = 1
DEC_BATCH = 8
DEC_SEQ = 32
PAST_LEN = 4096

CHUNK = 64
D_MIX = D_MODEL
D_ATT = D_MIX // 2
D_LRU = D_MIX - D_ATT
N_HEADS_A = 4
D_HEAD_V = D_ATT // N_HEADS_A
D_HEAD_QK = D_HEAD_V // 2
N_BLOCKS_LRU = 8
D_BLOCK_LRU = D_LRU // N_BLOCKS_LRU
CONV_W = 4
LRU_C = 8.0
D_FF = 2816
Q_BLOCK = 128
RMS_EPS = 1e-6
D_IN = 3 * D_ATT + 2 * D_LRU
NEG_INF = -1e30

kernel_name = "hybrid_diffattn_rglru_streaming_step"


def rmsnorm(x, g):
    xf = x.astype(jnp.float32)
    y = xf * lax.rsqrt(jnp.mean(xf * xf, axis=-1, keepdims=True) + RMS_EPS)
    return (y * g.astype(jnp.float32)).astype(x.dtype)


def swiglu(x, w_gate, w_up, w_down):
    return (jax.nn.silu(x @ w_gate) * (x @ w_up)) @ w_down


def alibi_slopes():
    return jnp.asarray(2.0 ** (-8.0 * np.arange(1, N_HEADS_A + 1) / N_HEADS_A), dtype=jnp.float32)


def diff_attn_core(q, k, v, q_pos, k_pos, lam):
    scale = 1.0 / math.sqrt(D_HEAD_QK)
    s = jnp.einsum('bqhcd,bkhcd->bhcqk', q.astype(jnp.float32), k.astype(jnp.float32)) * scale
    dist = jnp.abs(q_pos[:, None] - k_pos[None, :]).astype(jnp.float32)
    bias = -alibi_slopes()[None, :, None, None, None] * dist[None, None, None]
    mask = (q_pos[:, None] // CHUNK) >= (k_pos[None, :] // CHUNK)
    s = jnp.where(mask[None, None, None], s + bias, NEG_INF)
    p = jax.nn.softmax(s, axis=-1)
    p = p[:, :, 0] - lam * p[:, :, 1]
    return jnp.einsum('bhqk,bkhd->bqhd', p, v.astype(jnp.float32))


def diff_attn_prompt(q, k, v, lam):
    B, T = q.shape[0], q.shape[1]
    nb = T // Q_BLOCK
    qb = q.reshape(B, nb, Q_BLOCK, N_HEADS_A, 2, D_HEAD_QK).swapaxes(0, 1)
    pb = jnp.arange(T, dtype=jnp.int32).reshape(nb, Q_BLOCK)
    kpos = jnp.arange(T, dtype=jnp.int32)
    o = lax.map(lambda a: diff_attn_core(a[0], k, v, a[1], kpos, lam), (qb, pb))
    return o.swapaxes(0, 1).reshape(B, T, N_HEADS_A, D_HEAD_V)


def causal_conv(x, buf, w, b):
    T = x.shape[1]
    xp = jnp.concatenate([buf.astype(x.dtype), x], axis=1)
    y = b + sum(xp[:, j:j + T] * w[j] for j in range(CONV_W))
    return y, xp[:, -(CONV_W - 1):]


def block_diag(x, w, b):
    B, T = x.shape[0], x.shape[1]
    xb = x.reshape(B, T, N_BLOCKS_LRU, D_BLOCK_LRU)
    return jnp.einsum('btnc,ncd->btnd', xb, w).reshape(B, T, D_LRU) + b


def rglru(x, h0, w_r, b_r, w_i, b_i, lru_lambda):
    xf = x.astype(jnp.float32)
    r = jax.nn.sigmoid(block_diag(xf, w_r.astype(jnp.float32), b_r.astype(jnp.float32)))
    i = jax.nn.sigmoid(block_diag(xf, w_i.astype(jnp.float32), b_i.astype(jnp.float32)))
    log_a = -LRU_C * r * jax.nn.softplus(-lru_lambda.astype(jnp.float32))
    a = jnp.exp(log_a)
    bx = jnp.sqrt(-jnp.expm1(2.0 * log_a)) * (i * xf)
    bx = bx.at[:, 0].add(a[:, 0] * h0.astype(jnp.float32))

    def combine(c1, c2):
        a1, b1 = c1
        a2, b2 = c2
        return a1 * a2, a2 * b1 + b2

    _, h = lax.associative_scan(combine, (a, bx), axis=1)
    return h.astype(x.dtype), h[:, -1].astype(x.dtype)


def mixer(xn, past_k, past_v, h0, conv_buf, w_in, w_out, lq1, lk1, lq2, lk2, subln_g,
          conv_w, conv_b, w_rg, b_rg, w_ig, b_ig, lru_lambda, lambda_init):
    B, T = xn.shape[0], xn.shape[1]
    proj = xn @ w_in
    q = proj[..., :D_ATT].reshape(B, T, N_HEADS_A, 2, D_HEAD_QK)
    k = proj[..., D_ATT:2 * D_ATT].reshape(B, T, N_HEADS_A, 2, D_HEAD_QK)
    v = proj[..., 2 * D_ATT:3 * D_ATT].reshape(B, T, N_HEADS_A, D_HEAD_V)
    lru_x = proj[..., 3 * D_ATT:3 * D_ATT + D_LRU]
    lru_gate = proj[..., 3 * D_ATT + D_LRU:]
    lam = (jnp.exp(jnp.sum(lq1.astype(jnp.float32) * lk1.astype(jnp.float32)))
           - jnp.exp(jnp.sum(lq2.astype(jnp.float32) * lk2.astype(jnp.float32))) + lambda_init)
    if past_k is None:
        o = diff_attn_prompt(q, k, v, lam)
    else:
        P = past_k.shape[1]
        k_all = jnp.concatenate([past_k.astype(k.dtype), k], axis=1)
        v_all = jnp.concatenate([past_v.astype(v.dtype), v], axis=1)
        q_pos = P + jnp.arange(T, dtype=jnp.int32)
        k_pos = jnp.arange(P + T, dtype=jnp.int32)
        o = diff_attn_core(q, k_all, v_all, q_pos, k_pos, lam)
    o = rmsnorm(o, subln_g) * (1.0 - lambda_init)
    att_out = o.reshape(B, T, D_ATT).astype(xn.dtype)
    xc, new_buf = causal_conv(lru_x, conv_buf, conv_w, conv_b)
    h, h_last = rglru(xc, h0, w_rg, b_rg, w_ig, b_ig, lru_lambda)
    lru_out = h * jax.nn.gelu(lru_gate, approximate=True)
    out = jnp.concatenate([att_out, lru_out], axis=-1) @ w_out
    return out, k, v, h_last, new_buf


def layer(x, past_k, past_v, h0, conv_buf, lp, lambda_init):
    (w_in, w_out, lq1, lk1, lq2, lk2, subln_g, conv_w, conv_b, w_rg, b_rg, w_ig, b_ig, lru_lambda,
     f1g, f1u, f1d, f2g, f2u, f2d, g1a, g1b, gma, gmb, g2a, g2b) = lp
    x = x + 0.5 * rmsnorm(swiglu(rmsnorm(x, g1a), f1g, f1u, f1d), g1b)
    m, k_new, v_new, h_last, new_buf = mixer(rmsnorm(x, gma), past_k, past_v, h0, conv_buf, w_in, w_out,
                                             lq1, lk1, lq2, lk2, subln_g, conv_w, conv_b, w_rg, b_rg,
                                             w_ig, b_ig, lru_lambda, lambda_init)
    x = x + rmsnorm(m, gmb)
    x = x + 0.5 * rmsnorm(swiglu(rmsnorm(x, g2a), f2g, f2u, f2d), g2b)
    return x, k_new, v_new, h_last, new_buf


def setup_inputs(seed: int = 0) -> dict:
    key = jax.random.key(seed)
    ks = iter(jax.random.split(key, 40))
    f32 = jnp.float32

    def nrm(shape, scale):
        return jax.random.normal(next(ks), shape, f32) * scale

    def gain():
        return jnp.ones((DEPTH, D_MODEL), f32) + nrm((DEPTH, D_MODEL), 0.01)

    a_init = jax.random.uniform(next(ks), (DEPTH, D_LRU), f32, 0.9, 0.999)
    s_init = a_init ** (1.0 / LRU_C)
    lru_lambda = jnp.log(s_init) - jnp.log1p(-s_init)
    return {
        "x_prompt": nrm((BATCH, SEQ, D_MODEL), 1.0),
        "x_sample": nrm((DEC_BATCH, DEC_SEQ, D_MODEL), 1.0),
        "cache_k": nrm((DEPTH, DEC_BATCH, PAST_LEN, N_HEADS_A, 2, D_HEAD_QK), 1.0),
        "cache_v": nrm((DEPTH, DEC_BATCH, PAST_LEN, N_HEADS_A, D_HEAD_V), 1.0),
        "state_lru_h": nrm((DEPTH, DEC_BATCH, D_LRU), 0.5),
        "state_conv": nrm((DEPTH, DEC_BATCH, CONV_W - 1, D_LRU), 1.0),
        "w_in": nrm((DEPTH, D_MODEL, D_IN), D_MODEL ** -0.5),
        "w_out": nrm((DEPTH, D_MIX, D_MODEL), D_MIX ** -0.5),
        "lambda_q1": nrm((DEPTH, D_HEAD_QK), 0.1),
        "lambda_k1": nrm((DEPTH, D_HEAD_QK), 0.1),
        "lambda_q2": nrm((DEPTH, D_HEAD_QK), 0.1),
        "lambda_k2": nrm((DEPTH, D_HEAD_QK), 0.1),
        "subln_g": jnp.ones((DEPTH, D_HEAD_V), f32) + nrm((DEPTH, D_HEAD_V), 0.01),
        "conv_w": nrm((DEPTH, CONV_W, D_LRU), CONV_W ** -0.5),
        "conv_b": nrm((DEPTH, D_LRU), 0.01),
        "w_rgate": nrm((DEPTH, N_BLOCKS_LRU, D_BLOCK_LRU, D_BLOCK_LRU), D_BLOCK_LRU ** -0.5),
        "b_rgate": nrm((DEPTH, D_LRU), 0.01),
        "w_igate": nrm((DEPTH, N_BLOCKS_LRU, D_BLOCK_LRU, D_BLOCK_LRU), D_BLOCK_LRU ** -0.5),
        "b_igate": nrm((DEPTH, D_LRU), 0.01),
        "lru_lambda": lru_lambda,
        "ffn1_w_gate": nrm((DEPTH, D_MODEL, D_FF), D_MODEL ** -0.5),
        "ffn1_w_up": nrm((DEPTH, D_MODEL, D_FF), D_MODEL ** -0.5),
        "ffn1_w_down": nrm((DEPTH, D_FF, D_MODEL), D_FF ** -0.5),
        "ffn2_w_gate": nrm((DEPTH, D_MODEL, D_FF), D_MODEL ** -0.5),
        "ffn2_w_up": nrm((DEPTH, D_MODEL, D_FF), D_MODEL ** -0.5),
        "ffn2_w_down": nrm((DEPTH, D_FF, D_MODEL), D_FF ** -0.5),
        "g_ffn1_pre": gain(),
        "g_ffn1_post": gain(),
        "g_mix_pre": gain(),
        "g_mix_post": gain(),
        "g_ffn2_pre": gain(),
        "g_ffn2_post": gain(),
    }


def reference(x_prompt, x_sample, cache_k, cache_v, state_lru_h, state_conv, w_in, w_out,
              lambda_q1, lambda_k1, lambda_q2, lambda_k2, subln_g, conv_w, conv_b,
              w_rgate, b_rgate, w_igate, b_igate, lru_lambda,
              ffn1_w_gate, ffn1_w_up, ffn1_w_down, ffn2_w_gate, ffn2_w_up, ffn2_w_down,
              g_ffn1_pre, g_ffn1_post, g_mix_pre, g_mix_post, g_ffn2_pre, g_ffn2_post):
    xp, xs = x_prompt, x_sample
    kp_l, vp_l, hp_l, cp_l, ks_l, vs_l, hs_l, cs_l = [], [], [], [], [], [], [], []
    for l in range(DEPTH):
        lambda_init = 0.8 - 0.6 * math.exp(-0.3 * l)
        lp = (w_in[l], w_out[l], lambda_q1[l], lambda_k1[l], lambda_q2[l], lambda_k2[l], subln_g[l],
              conv_w[l], conv_b[l], w_rgate[l], b_rgate[l], w_igate[l], b_igate[l], lru_lambda[l],
              ffn1_w_gate[l], ffn1_w_up[l], ffn1_w_down[l], ffn2_w_gate[l], ffn2_w_up[l], ffn2_w_down[l],
              g_ffn1_pre[l], g_ffn1_post[l], g_mix_pre[l], g_mix_post[l], g_ffn2_pre[l], g_ffn2_post[l])
        h0_p = jnp.zeros((xp.shape[0], D_LRU), xp.dtype)
        buf_p = jnp.zeros((xp.shape[0], CONV_W - 1, D_LRU), xp.dtype)
        xp, k_p, v_p, h_p, c_p = layer(xp, None, None, h0_p, buf_p, lp, lambda_init)
        xs, k_s, v_s, h_s, c_s = layer(xs, cache_k[l], cache_v[l], state_lru_h[l], state_conv[l], lp, lambda_init)
        kp_l.append(k_p); vp_l.append(v_p); hp_l.append(h_p); cp_l.append(c_p)
        ks_l.append(k_s); vs_l.append(v_s); hs_l.append(h_s); cs_l.append(c_s)
    k_prompt = jnp.stack(kp_l); v_prompt = jnp.stack(vp_l)
    lru_h_prompt = jnp.stack(hp_l); conv_prompt = jnp.stack(cp_l)
    k_sample = jnp.stack(ks_l); v_sample = jnp.stack(vs_l)
    lru_h_sample = jnp.stack(hs_l); conv_sample = jnp.stack(cs_l)
    return (xp, xs, k_prompt, v_prompt, lru_h_prompt, conv_prompt, k_sample, v_sample, lru_h_sample, conv_sample)
```

```python
import functools
import math

import jax
import jax.numpy as jnp
import numpy as np
from jax import lax
from jax.experimental import pallas as pl
from jax.experimental.pallas import tpu as pltpu

F32 = jnp.float32
BF16 = jnp.bfloat16

D_MODEL = 1024
D_ATT = 512
D_LRU = 512
N_HEADS = 4
D_HEAD_V = 128
D_HEAD_QK = 64
N_BLOCKS_LRU = 8
CONV_W = 4
LRU_C = 8.0
D_FF = 2816
D_IN = 3 * D_ATT + 2 * D_LRU
CHUNK = 64
RMS_EPS = 1e-6
NEG_INF = -1e30
LAMBDA_INIT = 0.8 - 0.6 * math.exp(-0.3 * 0)

LANES = 128
FF_CHUNK = 256
N_FF_CHUNKS = D_FF // FF_CHUNK
ROW_TILE = 256
ATTN_TILE = 256
LRU_TILE = 512
CACHE_TILE = 1024
VMEM_LIMIT = 56 * 1024 * 1024


def _dot(a, b):
    return jnp.dot(a, b, preferred_element_type=F32)


def _dot_nt(a, b):
    return lax.dot_general(a, b, (((1,), (1,)), ((), ())), preferred_element_type=F32)


def _rms(x, g):
    return x * lax.rsqrt(jnp.mean(x * x, axis=-1, keepdims=True) + RMS_EPS) * g


def _swiglu_ffn(x, g_pre, g_post, wg_ref, wu_ref, wd_ref):
    xn = _rms(x, g_pre).astype(BF16)
    acc = jnp.zeros(x.shape, F32)
    for j in range(N_FF_CHUNKS):
        g = _dot(xn, wg_ref[j])
        u = _dot(xn, wu_ref[j])
        h = (g * jax.nn.sigmoid(g) * u).astype(BF16)
        acc = acc + _dot(h, wd_ref[j])
    return x + 0.5 * _rms(acc, g_post)


def _ffn_inproj_kernel(x_ref, g1a_ref, g1b_ref, gma_ref, wg_ref, wu_ref, wd_ref, win_ref,
                       x1_ref, qb_ref, kf_ref, vf_ref, kb_ref, vb_ref, lx_ref, lg_ref):
    x1 = _swiglu_ffn(x_ref[...], g1a_ref[...], g1b_ref[...], wg_ref, wu_ref, wd_ref)
    x1_ref[...] = x1
    xm = _rms(x1, gma_ref[...]).astype(BF16)
    scale = 1.0 / math.sqrt(D_HEAD_QK)
    qb_ref[...] = (_dot(xm, win_ref[:, 0:D_ATT]) * scale).astype(BF16)
    k = _dot(xm, win_ref[:, D_ATT:2 * D_ATT])
    kf_ref[...] = k
    kb_ref[...] = k.astype(BF16)
    v = _dot(xm, win_ref[:, 2 * D_ATT:3 * D_ATT])
    vf_ref[...] = v
    vb_ref[...] = v.astype(BF16)
    lx_ref[...] = _dot(xm, win_ref[:, 3 * D_ATT:3 * D_ATT + D_LRU])
    lg_ref[...] = _dot(xm, win_ref[:, 3 * D_ATT + D_LRU:D_IN])


def _resident(shape):
    nd = len(shape)
    return pl.BlockSpec(shape, lambda i: (0,) * nd, pipeline_mode=pl.Buffered(1))


def _ffn_inproj(x, g1a, g1b, gma, wg, wu, wd, win):
    m = x.shape[0]
    tm = min(ROW_TILE, m)
    rows = lambda d: pl.BlockSpec((tm, d), lambda i: (i, 0))
    out_shape = (
        jax.ShapeDtypeStruct((m, D_MODEL), F32),
        jax.ShapeDtypeStruct((m, D_ATT), BF16),
        jax.ShapeDtypeStruct((m, D_ATT), F32),
        jax.ShapeDtypeStruct((m, D_ATT), F32),
        jax.ShapeDtypeStruct((m, D_ATT), BF16),
        jax.ShapeDtypeStruct((m, D_ATT), BF16),
        jax.ShapeDtypeStruct((m, D_LRU), F32),
        jax.ShapeDtypeStruct((m, D_LRU), F32),
    )
    return pl.pallas_call(
        _ffn_inproj_kernel,
        grid=(m // tm,),
        in_specs=[rows(D_MODEL), _resident(g1a.shape), _resident(g1b.shape), _resident(gma.shape),
                  _resident(wg.shape), _resident(wu.shape), _resident(wd.shape), _resident(win.shape)],
        out_specs=(rows(D_MODEL), rows(D_ATT), rows(D_ATT), rows(D_ATT), rows(D_ATT), rows(D_ATT),
                   rows(D_LRU), rows(D_LRU)),
        out_shape=out_shape,
        compiler_params=pltpu.CompilerParams(dimension_semantics=("arbitrary",), vmem_limit_bytes=VMEM_LIMIT),
        name="ffn_inproj",
    )(x, g1a, g1b, gma, wg, wu, wd, win)


def _outproj_ffn_kernel(att_ref, lru_ref, x1_ref, woa_ref, wob_ref, gmb_ref, g2a_ref, g2b_ref,
                        wg_ref, wu_ref, wd_ref, y_ref):
    mix = _dot(att_ref[...], woa_ref[...]) + _dot(lru_ref[...], wob_ref[...])
    x2 = x1_ref[...] + _rms(mix, gmb_ref[...])
    y_ref[...] = _swiglu_ffn(x2, g2a_ref[...], g2b_ref[...], wg_ref, wu_ref, wd_ref)


def _outproj_ffn(att, lru, x1, woa, wob, gmb, g2a, g2b, wg, wu, wd):
    m = x1.shape[0]
    tm = min(ROW_TILE, m)
    rows = lambda d: pl.BlockSpec((tm, d), lambda i: (i, 0))
    return pl.pallas_call(
        _outproj_ffn_kernel,
        grid=(m // tm,),
        in_specs=[rows(D_ATT), rows(D_LRU), rows(D_MODEL), _resident(woa.shape), _resident(wob.shape),
                  _resident(gmb.shape), _resident(g2a.shape), _resident(g2b.shape),
                  _resident(wg.shape), _resident(wu.shape), _resident(wd.shape)],
        out_specs=rows(D_MODEL),
        out_shape=jax.ShapeDtypeStruct((m, D_MODEL), F32),
        compiler_params=pltpu.CompilerParams(dimension_semantics=("arbitrary",), vmem_limit_bytes=VMEM_LIMIT),
        name="outproj_ffn",
    )(att, lru, x1, woa, wob, gmb, g2a, g2b, wg, wu, wd)


def _split_maps(q):
    lane = lax.broadcasted_iota(jnp.int32, q.shape, 1)
    zero = jnp.zeros_like(q)
    return jnp.where(lane < D_HEAD_QK, q, zero), jnp.where(lane >= D_HEAD_QK, q, zero)


def _softmax_tile_update(m_ref, l_ref, acc_ref, idx, s, v_t):
    m_prev = m_ref[idx]
    m_new = jnp.maximum(m_prev, jnp.max(s, axis=1, keepdims=True))
    alpha = jnp.exp(m_prev - m_new)
    ps = [jnp.exp(s[:, b * LANES:(b + 1) * LANES] - m_new) for b in range(s.shape[1] // LANES)]
    l_ref[idx] = alpha * l_ref[idx] + functools.reduce(lambda a, b: a + b, ps)
    p = jnp.concatenate(ps, axis=1).astype(BF16)
    acc_ref[idx] = alpha * acc_ref[idx] + _dot(p, v_t)
    m_ref[idx] = m_new


def _diff_combine(lam_refs, g, m_ref, l_ref, acc_ref, i0, i1):
    lq1, lk1, lq2, lk2 = lam_refs
    lam = (jnp.exp(jnp.sum(lq1[...] * lk1[...], axis=1, keepdims=True))
           - jnp.exp(jnp.sum(lq2[...] * lk2[...], axis=1, keepdims=True)) + LAMBDA_INIT)
    l0 = jnp.sum(l_ref[i0], axis=1, keepdims=True)
    l1 = jnp.sum(l_ref[i1], axis=1, keepdims=True)
    o = acc_ref[i0] / l0 - lam * (acc_ref[i1] / l1)
    return _rms(o, g) * (1.0 - LAMBDA_INIT)


def _attn_prompt_kernel(slopes_ref, q_ref, k_ref, v_ref, lq1, lk1, lq2, lk2, g_ref, o_ref,
                        m_ref, l_ref, acc_ref, *, tq):
    h = pl.program_id(1)
    qi = pl.program_id(2)
    slope = slopes_ref[h]
    qz = _split_maps(q_ref[0])
    m_ref[...] = jnp.full(m_ref.shape, NEG_INF, F32)
    l_ref[...] = jnp.zeros(l_ref.shape, F32)
    acc_ref[...] = jnp.zeros(acc_ref.shape, F32)

    def before_tile(j, carry):
        k0 = pl.multiple_of(j * tq, tq)
        k_t = k_ref[0, pl.ds(k0, tq), :]
        v_t = v_ref[0, pl.ds(k0, tq), :]
        col = lax.broadcasted_iota(jnp.int32, (1, tq), 1) + (j - qi) * tq
        col_bias = slope * col.astype(F32)
        for c in range(2):
            _softmax_tile_update(m_ref, l_ref, acc_ref, c, _dot_nt(qz[c], k_t) + col_bias, v_t)
        return carry

    lax.fori_loop(0, qi, before_tile, 0)

    q0 = pl.multiple_of(qi * tq, tq)
    k_t = k_ref[0, pl.ds(q0, tq), :]
    v_t = v_ref[0, pl.ds(q0, tq), :]
    row = lax.broadcasted_iota(jnp.int32, (tq, tq), 0)
    col = lax.broadcasted_iota(jnp.int32, (tq, tq), 1)
    bias = slope * (row - jnp.abs(row - col)).astype(F32)
    visible = (row // CHUNK) >= (col // CHUNK)
    for c in range(2):
        s = jnp.where(visible, _dot_nt(qz[c], k_t) + bias, NEG_INF)
        _softmax_tile_update(m_ref, l_ref, acc_ref, c, s, v_t)

    o_ref[0] = _diff_combine((lq1, lk1, lq2, lk2), g_ref[...], m_ref, l_ref, acc_ref, 0, 1).astype(o_ref.dtype)


def _attn_prompt(slopes, q, k, v, lq1, lk1, lq2, lk2, g):
    b, t, _ = q.shape
    tq = min(ATTN_TILE, t)
    small = lambda a: pl.BlockSpec(a.shape, lambda bi, hi, qi: (0, 0))
    return pl.pallas_call(
        functools.partial(_attn_prompt_kernel, tq=tq),
        grid=(b, N_HEADS, t // tq),
        in_specs=[pl.BlockSpec(memory_space=pltpu.SMEM),
                  pl.BlockSpec((1, tq, LANES), lambda bi, hi, qi: (bi, qi, hi)),
                  pl.BlockSpec((1, t, LANES), lambda bi, hi, qi: (bi, 0, hi)),
                  pl.BlockSpec((1, t, LANES), lambda bi, hi, qi: (bi, 0, hi)),
                  small(lq1), small(lk1), small(lq2), small(lk2), small(g)],
        out_specs=pl.BlockSpec((1, tq, LANES), lambda bi, hi, qi: (bi, qi, hi)),
        out_shape=jax.ShapeDtypeStruct((b, t, D_ATT), BF16),
        scratch_shapes=[pltpu.VMEM((2, tq, LANES), F32)] * 3,
        compiler_params=pltpu.CompilerParams(dimension_semantics=("arbitrary",) * 3, vmem_limit_bytes=VMEM_LIMIT),
        name="attn_prompt",
    )(slopes, q, k, v, lq1, lk1, lq2, lk2, g)


def _attn_sample_kernel(slopes_ref, q_ref, ck_ref, cv_ref, kn_ref, vn_ref, lq1, lk1, lq2, lk2, g_ref, o_ref,
                        m_ref, l_ref, acc_ref, *, past, tk):
    j = pl.program_id(1)
    tq = q_ref.shape[1]

    @pl.when(j == 0)
    def _():
        m_ref[...] = jnp.full(m_ref.shape, NEG_INF, F32)
        l_ref[...] = jnp.zeros(l_ref.shape, F32)
        acc_ref[...] = jnp.zeros(acc_ref.shape, F32)

    def attend(k_all, v_all, k_start, n_valid):
        n = k_all.shape[0]
        q_pos = past + lax.broadcasted_iota(jnp.int32, (tq, n), 0)
        k_pos = k_start + lax.broadcasted_iota(jnp.int32, (tq, n), 1)
        dist = jnp.abs(q_pos - k_pos).astype(F32)
        visible = ((q_pos // CHUNK) >= (k_pos // CHUNK)) & (k_pos < k_start + n_valid)
        for h in range(N_HEADS):
            cols = slice(h * LANES, (h + 1) * LANES)
            qz = _split_maps(q_ref[0, :, cols])
            k_h = k_all[:, cols].astype(BF16)
            v_h = v_all[:, cols].astype(BF16)
            for c in range(2):
                s = jnp.where(visible, _dot_nt(qz[c], k_h) - slopes_ref[h] * dist, NEG_INF)
                _softmax_tile_update(m_ref, l_ref, acc_ref, 2 * h + c, s, v_h)

    attend(ck_ref[0], cv_ref[0], j * tk, tk)

    @pl.when(j == pl.num_programs(1) - 1)
    def _():
        attend(kn_ref[0], vn_ref[0], past, tq)
        for h in range(N_HEADS):
            o = _diff_combine((lq1, lk1, lq2, lk2), g_ref[...], m_ref, l_ref, acc_ref, 2 * h, 2 * h + 1)
            o_ref[0, :, h * LANES:(h + 1) * LANES] = o.astype(o_ref.dtype)


def _attn_sample(slopes, q, cache_k, cache_v, k_new, v_new, lq1, lk1, lq2, lk2, g):
    b, t, _ = q.shape
    past = cache_k.shape[1]
    tk = min(CACHE_TILE, past)
    k_new = jnp.pad(k_new, ((0, 0), (0, -t % LANES), (0, 0)))
    v_new = jnp.pad(v_new, ((0, 0), (0, -t % LANES), (0, 0)))
    small = lambda a: pl.BlockSpec(a.shape, lambda bi, j: (0, 0))
    per_stream = lambda a: pl.BlockSpec((1,) + a.shape[1:], lambda bi, j: (bi, 0, 0))
    return pl.pallas_call(
        functools.partial(_attn_sample_kernel, past=past, tk=tk),
        grid=(b, past // tk),
        in_specs=[pl.BlockSpec(memory_space=pltpu.SMEM),
                  per_stream(q),
                  pl.BlockSpec((1, tk, D_ATT), lambda bi, j: (bi, j, 0)),
                  pl.BlockSpec((1, tk, D_ATT), lambda bi, j: (bi, j, 0)),
                  per_stream(k_new), per_stream(v_new),
                  small(lq1), small(lk1), small(lq2), small(lk2), small(g)],
        out_specs=pl.BlockSpec((1, t, D_ATT), lambda bi, j: (bi, 0, 0)),
        out_shape=jax.ShapeDtypeStruct((b, t, D_ATT), BF16),
        scratch_shapes=[pltpu.VMEM((2 * N_HEADS, t, LANES), F32)] * 3,
        compiler_params=pltpu.CompilerParams(dimension_semantics=("arbitrary",) * 2, vmem_limit_bytes=VMEM_LIMIT),
        name="attn_sample",
    )(slopes, q, cache_k, cache_v, k_new, v_new, lq1, lk1, lq2, lk2, g)


def _lru_kernel(x_ref, gate_ref, h0_ref, cbuf_ref, cw_ref, cb_ref, wr_ref, br_ref, wi_ref, bi_ref, lam_ref,
                out_ref, hlast_ref, xbuf_ref, h_ref, *, tt):
    ti = pl.program_id(1)
    pad = xbuf_ref.shape[0] - tt

    @pl.when(ti == 0)
    def _():
        xbuf_ref[0:pad, :] = cbuf_ref[0]
        h_ref[...] = h0_ref[0]

    @pl.when(ti > 0)
    def _():
        xbuf_ref[0:pad, :] = xbuf_ref[tt:tt + pad, :]

    xbuf_ref[pad:pad + tt, :] = x_ref[0]
    xc = cb_ref[...] + xbuf_ref[pad:pad + tt, :] * cw_ref[CONV_W - 1:CONV_W, :]
    for j in range(CONV_W - 1):
        back = CONV_W - 1 - j
        xc = xc + xbuf_ref[pad - back:pad - back + tt, :] * cw_ref[j:j + 1, :]

    xb = xc.astype(BF16)
    r = jax.nn.sigmoid(_dot(xb, wr_ref[...]) + br_ref[...])
    i = jax.nn.sigmoid(_dot(xb, wi_ref[...]) + bi_ref[...])
    neg_lam = -lam_ref[...]
    softplus = jnp.maximum(neg_lam, 0.0) + jnp.log1p(jnp.exp(-jnp.abs(neg_lam)))
    log_a = -LRU_C * r * softplus
    a = jnp.exp(log_a)
    b = jnp.sqrt(-jnp.tanh(log_a) * (1.0 + a * a)) * (i * xc)

    row = lax.broadcasted_iota(jnp.int32, (tt, D_LRU), 0)
    d = 1
    while d < tt:
        has_prev = row >= d
        a_prev = jnp.where(has_prev, pltpu.roll(a, d, 0), 1.0)
        b_prev = jnp.where(has_prev, pltpu.roll(b, d, 0), 0.0)
        b = a * b_prev + b
        a = a * a_prev
        d *= 2
    hs = a * h_ref[...] + b
    h_new = hs[tt - 1:tt, :]
    h_ref[...] = h_new
    hlast_ref[0] = h_new

    gate = gate_ref[0]
    gelu = 0.5 * gate * (1.0 + jnp.tanh(math.sqrt(2.0 / math.pi) * (gate + 0.044715 * (gate * gate * gate))))
    out_ref[0] = (hs * gelu).astype(out_ref.dtype)


def _lru(x, gate, h0, cbuf, cw, cb, wr, br, wi, bi, lam):
    b, t, _ = x.shape
    tt = min(LRU_TILE, t)
    pad = cbuf.shape[1]
    small = lambda a: pl.BlockSpec(a.shape, lambda bi_, ti: (0, 0))
    tile = pl.BlockSpec((1, tt, D_LRU), lambda bi_, ti: (bi_, ti, 0))
    per_stream = lambda a: pl.BlockSpec((1,) + a.shape[1:], lambda bi_, ti: (bi_, 0, 0))
    return pl.pallas_call(
        functools.partial(_lru_kernel, tt=tt),
        grid=(b, t // tt),
        in_specs=[tile, tile, per_stream(h0), per_stream(cbuf), small(cw), small(cb),
                  small(wr), small(br), small(wi), small(bi), small(lam)],
        out_specs=(tile, pl.BlockSpec((1, 1, D_LRU), lambda bi_, ti: (bi_, 0, 0))),
        out_shape=(jax.ShapeDtypeStruct((b, t, D_LRU), BF16), jax.ShapeDtypeStruct((b, 1, D_LRU), F32)),
        scratch_shapes=[pltpu.VMEM((tt + pad, D_LRU), F32), pltpu.VMEM((1, D_LRU), F32)],
        compiler_params=pltpu.CompilerParams(dimension_semantics=("arbitrary",) * 2, vmem_limit_bytes=VMEM_LIMIT),
        name="lru",
    )(x, gate, h0, cbuf, cw, cb, wr, br, wi, bi, lam)


def _block_diag_dense(w):
    n, c, _ = w.shape
    eye = jnp.eye(n, dtype=w.dtype)
    return (eye[:, None, :, None] * w[:, :, None, :]).reshape(n * c, n * c)


def _chunk_cols(w):
    return w.reshape(w.shape[0], N_FF_CHUNKS, FF_CHUNK).transpose(1, 0, 2)


def kernel(x_prompt, x_sample, cache_k, cache_v, state_lru_h, state_conv, w_in, w_out, lambda_q1, lambda_k1,
           lambda_q2, lambda_k2, subln_g, conv_w, conv_b, w_rgate, b_rgate, w_igate, b_igate, lru_lambda,
           ffn1_w_gate, ffn1_w_up, ffn1_w_down, ffn2_w_gate, ffn2_w_up, ffn2_w_down,
           g_ffn1_pre, g_ffn1_post, g_mix_pre, g_mix_post, g_ffn2_pre, g_ffn2_post):
    bp, tp, _ = x_prompt.shape
    bs, ts, _ = x_sample.shape
    past = cache_k.shape[2]

    wg1, wu1 = _chunk_cols(ffn1_w_gate[0].astype(BF16)), _chunk_cols(ffn1_w_up[0].astype(BF16))
    wd1 = ffn1_w_down[0].astype(BF16).reshape(N_FF_CHUNKS, FF_CHUNK, D_MODEL)
    wg2, wu2 = _chunk_cols(ffn2_w_gate[0].astype(BF16)), _chunk_cols(ffn2_w_up[0].astype(BF16))
    wd2 = ffn2_w_down[0].astype(BF16).reshape(N_FF_CHUNKS, FF_CHUNK, D_MODEL)
    win = w_in[0].astype(BF16)
    woa, wob = w_out[0, :D_ATT].astype(BF16), w_out[0, D_ATT:].astype(BF16)
    wr = _block_diag_dense(w_rgate[0]).astype(BF16)
    wi = _block_diag_dense(w_igate[0]).astype(BF16)
    slopes = jnp.asarray(2.0 ** (-8.0 * np.arange(1, N_HEADS + 1) / N_HEADS), dtype=F32)
    lam_vecs = (lambda_q1, lambda_k1, lambda_q2, lambda_k2)
    lru_params = (conv_w[0], conv_b, wr, b_rgate, wi, b_igate, lru_lambda)
    tail_pad = 8 - (CONV_W - 1)

    def stream(x, h0, conv_hist, attend):
        b, t, _ = x.shape
        x1, qb, kf, vf, kb, vb, lx, lg = _ffn_inproj(x.reshape(b * t, D_MODEL), g_ffn1_pre, g_ffn1_post, g_mix_pre,
                                                     wg1, wu1, wd1, win)
        att = attend(qb.reshape(b, t, D_ATT), kb.reshape(b, t, D_ATT), vb.reshape(b, t, D_ATT))
        lx3 = lx.reshape(b, t, D_LRU)
        cbuf = jnp.pad(conv_hist, ((0, 0), (tail_pad, 0), (0, 0)))
        lru_out, h_last = _lru(lx3, lg.reshape(b, t, D_LRU), h0.reshape(b, 1, D_LRU), cbuf, *lru_params)
        y = _outproj_ffn(att.reshape(b * t, D_ATT), lru_out.reshape(b * t, D_LRU), x1, woa, wob,
                         g_mix_post, g_ffn2_pre, g_ffn2_post, wg2, wu2, wd2)
        conv_new = jnp.concatenate([conv_hist, lx3], axis=1)[:, -(CONV_W - 1):]
        return (y.reshape(b, t, D_MODEL), kf.reshape(1, b, t, N_HEADS, 2, D_HEAD_QK),
                vf.reshape(1, b, t, N_HEADS, D_HEAD_V), h_last.reshape(1, b, D_LRU), conv_new[None])

    def attend_prompt(q, k, v):
        return _attn_prompt(slopes, q, k, v, *lam_vecs, subln_g)

    def attend_sample(q, k, v):
        ck = cache_k[0].reshape(bs, past, D_ATT)
        cv = cache_v[0].reshape(bs, past, D_ATT)
        return _attn_sample(slopes, q, ck, cv, k, v, *lam_vecs, subln_g)

    yp, kp, vp, hp, cp = stream(x_prompt, jnp.zeros((bp, D_LRU), F32), jnp.zeros((bp, CONV_W - 1, D_LRU), F32),
                                attend_prompt)
    ys, ks, vs, hs, cs = stream(x_sample, state_lru_h[0], state_conv[0], attend_sample)
    return (yp, ys, kp, vp, hp, cp, ks, vs, hs, cs)
```

```python
import functools
import math

import jax
import jax.numpy as jnp
import numpy as np
from jax import lax
from jax.experimental import pallas as pl
from jax.experimental.pallas import tpu as pltpu

F32 = jnp.float32
BF16 = jnp.bfloat16

D_MODEL = 1024
D_ATT = 512
D_LRU = 512
N_HEADS = 4
D_HEAD_V = 128
D_HEAD_QK = 64
CONV_W = 4
LRU_C = 8.0
D_FF = 2816
CHUNK = 64
RMS_EPS = 1e-6
NEG_INF = -1e30
LAMBDA_INIT = 0.8 - 0.6 * math.exp(-0.3 * 0)
LOG2E = math.log2(math.e)

LANES = 128
FF_CHUNK = 256
N_FF_CHUNKS = D_FF // FF_CHUNK
ROW_TILE = 256
ATTN_TILE = 2 * ROW_TILE
LRU_TILE = 512
CACHE_TILE = 1024
VMEM_LIMIT = 56 * 1024 * 1024


def _dot(a, b):
    return jnp.dot(a, b, preferred_element_type=F32)


def _dot_nt(a, b):
    return lax.dot_general(a, b, (((1,), (1,)), ((), ())), preferred_element_type=F32)


def _rms(x, g):
    return x * lax.rsqrt(jnp.mean(x * x, axis=-1, keepdims=True) + RMS_EPS) * g


def _swiglu_ffn(x, g_pre, g_post, wg_ref, wu_ref, wd_ref):
    xn = _rms(x, g_pre).astype(BF16)
    acc = jnp.zeros(x.shape, F32)
    for j in range(N_FF_CHUNKS):
        g = _dot(xn, wg_ref[j])
        u = _dot(xn, wu_ref[j])
        h = (g * jax.nn.sigmoid(g) * u).astype(BF16)
        acc = acc + _dot(h, wd_ref[j])
    return x + 0.5 * _rms(acc, g_post)


def _resident(shape):
    nd = len(shape)
    return pl.BlockSpec(shape, lambda i: (0,) * nd, pipeline_mode=pl.Buffered(1))


def _ffn_inproj_kernel(x_ref, g1a_ref, g1b_ref, gma_ref, wg_ref, wu_ref, wd_ref, wq_ref, wkt_ref, wv_ref, wl_ref,
                       x1_ref, qb_ref, kf_ref, kb_ref, vf_ref, vb_ref, lx_ref, lg_ref, *, seq_layout):
    x1 = _swiglu_ffn(x_ref[...], g1a_ref[...], g1b_ref[...], wg_ref, wu_ref, wd_ref)
    x1_ref[...] = x1
    xm = _rms(x1, gma_ref[...]).astype(BF16)
    tm = xm.shape[0]
    qb_ref[...] = (_dot(xm, wq_ref[...]) * (LOG2E / math.sqrt(D_HEAD_QK))).astype(BF16)
    v = _dot(xm, wv_ref[...])
    vb_ref[...] = v.astype(BF16)
    if seq_layout:
        kt = _dot_nt(wkt_ref[...], xm)
        kf_ref[0] = kt
        kb_ref[0, 0] = kt.astype(BF16)
        for h in range(N_HEADS):
            vf_ref[pl.ds(h, tm, stride=N_HEADS), :] = v[:, h * D_HEAD_V:(h + 1) * D_HEAD_V]
    else:
        k = _dot_nt(xm, wkt_ref[...])
        kf_ref[...] = k
        kb_ref[...] = k.astype(BF16)
        vf_ref[...] = v
    lru = _dot(xm, wl_ref[...])
    lx_ref[...] = lru[:, :D_LRU]
    lg_ref[...] = lru[:, D_LRU:]


def _ffn_inproj(x, g1a, g1b, gma, wg, wu, wd, wq, wkt, wv, wl, seq_len=None):
    m = x.shape[0]
    tm = min(ROW_TILE, m)
    rows = lambda d: pl.BlockSpec((tm, d), lambda i: (i, 0))
    row_out = lambda d, dt: jax.ShapeDtypeStruct((m, d), dt)
    seq_layout = seq_len is not None
    if seq_layout:
        nt = seq_len // tm
        b = m // seq_len
        k_shapes = (jax.ShapeDtypeStruct((b, D_ATT, seq_len), F32), jax.ShapeDtypeStruct((b, nt, D_ATT, tm), BF16))
        k_specs = (pl.BlockSpec((1, D_ATT, tm), lambda i: (i // nt, 0, i % nt)),
                   pl.BlockSpec((1, 1, D_ATT, tm), lambda i: (i // nt, i % nt, 0, 0)))
        vf_shape = jax.ShapeDtypeStruct((m * N_HEADS, D_HEAD_V), F32)
        vf_spec = pl.BlockSpec((tm * N_HEADS, D_HEAD_V), lambda i: (i, 0))
    else:
        k_shapes = (row_out(D_ATT, F32), row_out(D_ATT, BF16))
        k_specs = (rows(D_ATT), rows(D_ATT))
        vf_shape = row_out(D_ATT, F32)
        vf_spec = rows(D_ATT)
    out_shape = (row_out(D_MODEL, F32), row_out(D_ATT, BF16), *k_shapes, vf_shape, row_out(D_ATT, BF16),
                 row_out(D_LRU, F32), row_out(D_LRU, F32))
    out_specs = (rows(D_MODEL), rows(D_ATT), *k_specs, vf_spec, rows(D_ATT), rows(D_LRU), rows(D_LRU))
    weights = (g1a, g1b, gma, wg, wu, wd, wq, wkt, wv, wl)
    return pl.pallas_call(
        functools.partial(_ffn_inproj_kernel, seq_layout=seq_layout),
        grid=(m // tm,),
        in_specs=[rows(D_MODEL)] + [_resident(w.shape) for w in weights],
        out_specs=out_specs,
        out_shape=out_shape,
        compiler_params=pltpu.CompilerParams(dimension_semantics=("arbitrary",), vmem_limit_bytes=VMEM_LIMIT),
        name="ffn_inproj",
    )(x, *weights)


def _outproj_ffn_kernel(att_ref, lru_ref, x1_ref, woa_ref, wob_ref, gmb_ref, g2a_ref, g2b_ref,
                        wg_ref, wu_ref, wd_ref, y_ref):
    mix = _dot(att_ref[...], woa_ref[...]) + _dot(lru_ref[...], wob_ref[...])
    x2 = x1_ref[...] + _rms(mix, gmb_ref[...])
    y_ref[...] = _swiglu_ffn(x2, g2a_ref[...], g2b_ref[...], wg_ref, wu_ref, wd_ref)


def _outproj_ffn(att, lru, x1, woa, wob, gmb, g2a, g2b, wg, wu, wd):
    m = x1.shape[0]
    tm = min(ROW_TILE, m)
    rows = lambda d: pl.BlockSpec((tm, d), lambda i: (i, 0))
    weights = (woa, wob, gmb, g2a, g2b, wg, wu, wd)
    return pl.pallas_call(
        _outproj_ffn_kernel,
        grid=(m // tm,),
        in_specs=[rows(D_ATT), rows(D_LRU), rows(D_MODEL)] + [_resident(w.shape) for w in weights],
        out_specs=rows(D_MODEL),
        out_shape=jax.ShapeDtypeStruct((m, D_MODEL), F32),
        compiler_params=pltpu.CompilerParams(dimension_semantics=("arbitrary",), vmem_limit_bytes=VMEM_LIMIT),
        name="outproj_ffn",
    )(att, lru, x1, *weights)


def _split_maps(q):
    lane = lax.broadcasted_iota(jnp.int32, q.shape, 1)
    zero = jnp.zeros_like(q)
    return jnp.where(lane < D_HEAD_QK, q, zero), jnp.where(lane >= D_HEAD_QK, q, zero)


def _softmax_tile_update(m_ref, l_ref, acc_ref, idx, s, v_t):
    m_prev = m_ref[idx]
    m_new = jnp.maximum(m_prev, jnp.max(s, axis=1, keepdims=True))
    alpha = jnp.exp2(m_prev - m_new)
    ps = [jnp.exp2(s[:, b * LANES:(b + 1) * LANES] - m_new) for b in range(s.shape[1] // LANES)]
    l_ref[idx] = alpha * l_ref[idx] + functools.reduce(lambda a, b: a + b, ps)
    p = jnp.concatenate(ps, axis=1).astype(BF16)
    acc_ref[idx] = alpha * acc_ref[idx] + _dot(p, v_t)
    m_ref[idx] = m_new


def _diff_combine(lam_refs, g, l_ref, acc_ref, i0, i1):
    lq1, lk1, lq2, lk2 = lam_refs
    lam = (jnp.exp(jnp.sum(lq1[...] * lk1[...], axis=1, keepdims=True))
           - jnp.exp(jnp.sum(lq2[...] * lk2[...], axis=1, keepdims=True)) + LAMBDA_INIT)
    l0 = jnp.sum(l_ref[i0], axis=1, keepdims=True)
    l1 = jnp.sum(l_ref[i1], axis=1, keepdims=True)
    o = acc_ref[i0] / l0 - lam * (acc_ref[i1] / l1)
    return _rms(o, g) * (1.0 - LAMBDA_INIT)


def _init_softmax_state(m_ref, l_ref, acc_ref):
    m_ref[...] = jnp.full(m_ref.shape, NEG_INF, F32)
    l_ref[...] = jnp.zeros(l_ref.shape, F32)
    acc_ref[...] = jnp.zeros(acc_ref.shape, F32)


def _attn_prompt_kernel(slopes_ref, q_ref, kt_ref, v_ref, lq1, lk1, lq2, lk2, g_ref, o_ref,
                        m_ref, l_ref, acc_ref, *, tq, sub):
    h = pl.program_id(1)
    qi = pl.program_id(2)
    slope = slopes_ref[h] * LOG2E
    qz = _split_maps(q_ref[0])
    _init_softmax_state(m_ref, l_ref, acc_ref)

    def logits(c, j):
        return jnp.concatenate([_dot(qz[c], kt_ref[0, j * sub + e]) for e in range(sub)], axis=1)

    def before_tile(j, carry):
        v_t = v_ref[0, pl.ds(pl.multiple_of(j * tq, tq), tq), :]
        col = lax.broadcasted_iota(jnp.int32, (1, tq), 1) + (j - qi) * tq
        col_bias = slope * col.astype(F32)
        for c in range(2):
            _softmax_tile_update(m_ref, l_ref, acc_ref, c, logits(c, j) + col_bias, v_t)
        return carry

    lax.fori_loop(0, qi, before_tile, 0)

    v_t = v_ref[0, pl.ds(pl.multiple_of(qi * tq, tq), tq), :]
    row = lax.broadcasted_iota(jnp.int32, (tq, tq), 0)
    col = lax.broadcasted_iota(jnp.int32, (tq, tq), 1)
    bias = slope * (row - jnp.abs(row - col)).astype(F32)
    visible = (row // CHUNK) >= (col // CHUNK)
    for c in range(2):
        s = jnp.where(visible, logits(c, qi) + bias, NEG_INF)
        _softmax_tile_update(m_ref, l_ref, acc_ref, c, s, v_t)

    o_ref[0] = _diff_combine((lq1, lk1, lq2, lk2), g_ref[...], l_ref, acc_ref, 0, 1).astype(o_ref.dtype)


def _attn_prompt(slopes, q, kt, v, lq1, lk1, lq2, lk2, g):
    b, t, _ = q.shape
    n_kt, tk = kt.shape[1], kt.shape[3]
    tq = min(ATTN_TILE, t)
    small = lambda a: pl.BlockSpec(a.shape, lambda bi, hi, qi: (0, 0))
    return pl.pallas_call(
        functools.partial(_attn_prompt_kernel, tq=tq, sub=tq // tk),
        grid=(b, N_HEADS, t // tq),
        in_specs=[pl.BlockSpec(memory_space=pltpu.SMEM),
                  pl.BlockSpec((1, tq, LANES), lambda bi, hi, qi: (bi, qi, hi)),
                  pl.BlockSpec((1, n_kt, LANES, tk), lambda bi, hi, qi: (bi, 0, hi, 0)),
                  pl.BlockSpec((1, t, LANES), lambda bi, hi, qi: (bi, 0, hi)),
                  small(lq1), small(lk1), small(lq2), small(lk2), small(g)],
        out_specs=pl.BlockSpec((1, tq, LANES), lambda bi, hi, qi: (bi, qi, hi)),
        out_shape=jax.ShapeDtypeStruct((b, t, D_ATT), BF16),
        scratch_shapes=[pltpu.VMEM((2, tq, LANES), F32)] * 3,
        compiler_params=pltpu.CompilerParams(dimension_semantics=("arbitrary",) * 3, vmem_limit_bytes=VMEM_LIMIT),
        name="attn_prompt",
    )(slopes, q, kt, v, lq1, lk1, lq2, lk2, g)


def _attn_sample_kernel(slopes_ref, q_ref, ckt_ref, cv_ref, kn_ref, vn_ref, lq1, lk1, lq2, lk2, g_ref, o_ref,
                        m_ref, l_ref, acc_ref, *, past, tk):
    j = pl.program_id(1)
    tq = q_ref.shape[1]

    @pl.when(j == 0)
    def _():
        _init_softmax_state(m_ref, l_ref, acc_ref)

    def attend(n, k_start, n_valid, scores, values):
        q_pos = past + lax.broadcasted_iota(jnp.int32, (tq, n), 0)
        k_pos = k_start + lax.broadcasted_iota(jnp.int32, (tq, n), 1)
        dist = jnp.abs(q_pos - k_pos).astype(F32)
        visible = ((q_pos // CHUNK) >= (k_pos // CHUNK)) & (k_pos < k_start + n_valid)
        for h in range(N_HEADS):
            cols = slice(h * LANES, (h + 1) * LANES)
            qz = _split_maps(q_ref[0, :, cols])
            v_h = values(h)
            for c in range(2):
                s = jnp.where(visible, scores(qz[c], h) - (slopes_ref[h] * LOG2E) * dist, NEG_INF)
                _softmax_tile_update(m_ref, l_ref, acc_ref, 2 * h + c, s, v_h)

    attend(tk, j * tk, tk,
           lambda qc, h: _dot(qc, ckt_ref[0, h * LANES:(h + 1) * LANES, :].astype(BF16)),
           lambda h: cv_ref[0, pl.ds(h, tk, stride=N_HEADS), :].astype(BF16))

    @pl.when(j == pl.num_programs(1) - 1)
    def _():
        attend(kn_ref.shape[1], past, tq,
               lambda qc, h: _dot_nt(qc, kn_ref[0, :, h * LANES:(h + 1) * LANES]),
               lambda h: vn_ref[0, :, h * LANES:(h + 1) * LANES])
        for h in range(N_HEADS):
            o = _diff_combine((lq1, lk1, lq2, lk2), g_ref[...], l_ref, acc_ref, 2 * h, 2 * h + 1)
            o_ref[0, :, h * LANES:(h + 1) * LANES] = o.astype(o_ref.dtype)


def _attn_sample(slopes, q, cache_kt, cache_v_rows, k_new, v_new, lq1, lk1, lq2, lk2, g):
    b, t, _ = q.shape
    past = cache_kt.shape[2]
    tk = min(CACHE_TILE, past)
    k_new = jnp.pad(k_new, ((0, 0), (0, -t % LANES), (0, 0)))
    v_new = jnp.pad(v_new, ((0, 0), (0, -t % LANES), (0, 0)))
    small = lambda a: pl.BlockSpec(a.shape, lambda bi, j: (0, 0))
    per_stream = lambda a: pl.BlockSpec((1,) + a.shape[1:], lambda bi, j: (bi, 0, 0))
    return pl.pallas_call(
        functools.partial(_attn_sample_kernel, past=past, tk=tk),
        grid=(b, past // tk),
        in_specs=[pl.BlockSpec(memory_space=pltpu.SMEM),
                  per_stream(q),
                  pl.BlockSpec((1, D_ATT, tk), lambda bi, j: (bi, 0, j)),
                  pl.BlockSpec((1, tk * N_HEADS, D_HEAD_V), lambda bi, j: (bi, j, 0)),
                  per_stream(k_new), per_stream(v_new),
                  small(lq1), small(lk1), small(lq2), small(lk2), small(g)],
        out_specs=pl.BlockSpec((1, t, D_ATT), lambda bi, j: (bi, 0, 0)),
        out_shape=jax.ShapeDtypeStruct((b, t, D_ATT), BF16),
        scratch_shapes=[pltpu.VMEM((2 * N_HEADS, t, LANES), F32)] * 3,
        compiler_params=pltpu.CompilerParams(dimension_semantics=("arbitrary",) * 2, vmem_limit_bytes=VMEM_LIMIT),
        name="attn_sample",
    )(slopes, q, cache_kt, cache_v_rows, k_new, v_new, lq1, lk1, lq2, lk2, g)


def _lru_kernel(x_ref, gate_ref, h0_ref, cbuf_ref, cw_ref, cb_ref, wr_ref, br_ref, wi_ref, bi_ref, lam_ref,
                out_ref, hlast_ref, xbuf_ref, h_ref, *, tt):
    ti = pl.program_id(1)
    pad = xbuf_ref.shape[0] - tt

    @pl.when(ti == 0)
    def _():
        xbuf_ref[0:pad, :] = cbuf_ref[0]
        h_ref[...] = h0_ref[0]

    @pl.when(ti > 0)
    def _():
        xbuf_ref[0:pad, :] = xbuf_ref[tt:tt + pad, :]

    xbuf_ref[pad:pad + tt, :] = x_ref[0]
    xc = cb_ref[...] + xbuf_ref[pad:pad + tt, :] * cw_ref[CONV_W - 1:CONV_W, :]
    for j in range(CONV_W - 1):
        back = CONV_W - 1 - j
        xc = xc + xbuf_ref[pad - back:pad - back + tt, :] * cw_ref[j:j + 1, :]

    xb = xc.astype(BF16)
    r = jax.nn.sigmoid(_dot(xb, wr_ref[...]) + br_ref[...])
    i = jax.nn.sigmoid(_dot(xb, wi_ref[...]) + bi_ref[...])
    neg_lam = -lam_ref[...]
    softplus = jnp.maximum(neg_lam, 0.0) + jnp.log1p(jnp.exp(-jnp.abs(neg_lam)))
    log_a = -LRU_C * r * softplus
    a = jnp.exp(log_a)
    b = jnp.sqrt(-jnp.tanh(log_a) * (1.0 + a * a)) * (i * xc)

    row = lax.broadcasted_iota(jnp.int32, (tt, D_LRU), 0)
    d = 1
    while d < tt:
        has_prev = row >= d
        a_prev = jnp.where(has_prev, pltpu.roll(a, d, 0), 1.0)
        b_prev = jnp.where(has_prev, pltpu.roll(b, d, 0), 0.0)
        b = a * b_prev + b
        a = a * a_prev
        d *= 2
    hs = a * h_ref[...] + b
    h_new = hs[tt - 1:tt, :]
    h_ref[...] = h_new
    hlast_ref[0] = h_new

    gate = gate_ref[0]
    gelu = 0.5 * gate * (1.0 + jnp.tanh(math.sqrt(2.0 / math.pi) * (gate + 0.044715 * (gate * gate * gate))))
    out_ref[0] = (hs * gelu).astype(out_ref.dtype)


def _lru(x, gate, h0, cbuf, cw, cb, wr, br, wi, bi, lam):
    b, t, _ = x.shape
    tt = min(LRU_TILE, t)
    pad = cbuf.shape[1]
    small = lambda a: pl.BlockSpec(a.shape, lambda bi_, ti: (0, 0))
    tile = pl.BlockSpec((1, tt, D_LRU), lambda bi_, ti: (bi_, ti, 0))
    per_stream = lambda a: pl.BlockSpec((1,) + a.shape[1:], lambda bi_, ti: (bi_, 0, 0))
    return pl.pallas_call(
        functools.partial(_lru_kernel, tt=tt),
        grid=(b, t // tt),
        in_specs=[tile, tile, per_stream(h0), per_stream(cbuf), small(cw), small(cb),
                  small(wr), small(br), small(wi), small(bi), small(lam)],
        out_specs=(tile, pl.BlockSpec((1, 1, D_LRU), lambda bi_, ti: (bi_, 0, 0))),
        out_shape=(jax.ShapeDtypeStruct((b, t, D_LRU), BF16), jax.ShapeDtypeStruct((b, 1, D_LRU), F32)),
        scratch_shapes=[pltpu.VMEM((tt + pad, D_LRU), F32), pltpu.VMEM((1, D_LRU), F32)],
        compiler_params=pltpu.CompilerParams(dimension_semantics=("arbitrary",) * 2, vmem_limit_bytes=VMEM_LIMIT),
        name="lru",
    )(x, gate, h0, cbuf, cw, cb, wr, br, wi, bi, lam)


def _block_diag_dense(w):
    n, c, _ = w.shape
    eye = jnp.eye(n, dtype=w.dtype)
    return (eye[:, None, :, None] * w[:, :, None, :]).reshape(n * c, n * c)


def _chunk_cols(w):
    return w.reshape(w.shape[0], N_FF_CHUNKS, FF_CHUNK).transpose(1, 0, 2)


def kernel(x_prompt, x_sample, cache_k, cache_v, state_lru_h, state_conv, w_in, w_out, lambda_q1, lambda_k1,
           lambda_q2, lambda_k2, subln_g, conv_w, conv_b, w_rgate, b_rgate, w_igate, b_igate, lru_lambda,
           ffn1_w_gate, ffn1_w_up, ffn1_w_down, ffn2_w_gate, ffn2_w_up, ffn2_w_down,
           g_ffn1_pre, g_ffn1_post, g_mix_pre, g_mix_post, g_ffn2_pre, g_ffn2_post):
    bp, tp, _ = x_prompt.shape
    bs, ts, _ = x_sample.shape
    past = cache_k.shape[2]

    wg1, wu1 = _chunk_cols(ffn1_w_gate[0].astype(BF16)), _chunk_cols(ffn1_w_up[0].astype(BF16))
    wd1 = ffn1_w_down[0].astype(BF16).reshape(N_FF_CHUNKS, FF_CHUNK, D_MODEL)
    wg2, wu2 = _chunk_cols(ffn2_w_gate[0].astype(BF16)), _chunk_cols(ffn2_w_up[0].astype(BF16))
    wd2 = ffn2_w_down[0].astype(BF16).reshape(N_FF_CHUNKS, FF_CHUNK, D_MODEL)
    win = w_in[0].astype(BF16)
    wq, wkt, wv, wl = win[:, :D_ATT], win[:, D_ATT:2 * D_ATT].T, win[:, 2 * D_ATT:3 * D_ATT], win[:, 3 * D_ATT:]
    woa, wob = w_out[0, :D_ATT].astype(BF16), w_out[0, D_ATT:].astype(BF16)
    wr = _block_diag_dense(w_rgate[0]).astype(BF16)
    wi = _block_diag_dense(w_igate[0]).astype(BF16)
    slopes = jnp.asarray(2.0 ** (-8.0 * np.arange(1, N_HEADS + 1) / N_HEADS), dtype=F32)
    lam_vecs = (lambda_q1, lambda_k1, lambda_q2, lambda_k2)
    ffn1 = (g_ffn1_pre, g_ffn1_post, g_mix_pre, wg1, wu1, wd1, wq, wkt, wv, wl)
    lru_params = (conv_w[0], conv_b, wr, b_rgate, wi, b_igate, lru_lambda)
    tail_pad = 8 - (CONV_W - 1)

    def mix_and_ffn2(b, t, x1, att, lx, lg, h0, conv_hist):
        lx3 = lx.reshape(b, t, D_LRU)
        cbuf = jnp.pad(conv_hist, ((0, 0), (tail_pad, 0), (0, 0)))
        lru_out, h_last = _lru(lx3, lg.reshape(b, t, D_LRU), h0.reshape(b, 1, D_LRU), cbuf, *lru_params)
        y = _outproj_ffn(att.reshape(b * t, D_ATT), lru_out.reshape(b * t, D_LRU), x1, woa, wob,
                         g_mix_post, g_ffn2_pre, g_ffn2_post, wg2, wu2, wd2)
        conv_new = jnp.concatenate([conv_hist, lx3], axis=1)[:, -(CONV_W - 1):]
        return y.reshape(b, t, D_MODEL), h_last.reshape(1, b, D_LRU), conv_new[None]

    x1, qb, kt, ktb, v_rows, vb, lx, lg = _ffn_inproj(x_prompt.reshape(bp * tp, D_MODEL), *ffn1, seq_len=tp)
    att = _attn_prompt(slopes, qb.reshape(bp, tp, D_ATT), ktb, vb.reshape(bp, tp, D_ATT), *lam_vecs, subln_g)
    yp, hp, cp = mix_and_ffn2(bp, tp, x1, att, lx, lg, jnp.zeros((bp, D_LRU), F32),
                              jnp.zeros((bp, CONV_W - 1, D_LRU), F32))
    kp = kt.reshape(bp, N_HEADS, 2, D_HEAD_QK, tp).transpose(0, 4, 1, 2, 3)[None]
    vp = v_rows.reshape(1, bp, tp, N_HEADS, D_HEAD_V)

    x1, qb, kf, kb, vf, vb, lx, lg = _ffn_inproj(x_sample.reshape(bs * ts, D_MODEL), *ffn1)
    cache_kt = cache_k[0].transpose(0, 2, 3, 4, 1).reshape(bs, D_ATT, past)
    cache_v_rows = cache_v[0].reshape(bs, past * N_HEADS, D_HEAD_V)
    att = _attn_sample(slopes, qb.reshape(bs, ts, D_ATT), cache_kt, cache_v_rows, kb.reshape(bs, ts, D_ATT),
                       vb.reshape(bs, ts, D_ATT), *lam_vecs, subln_g)
    ys, hs, cs = mix_and_ffn2(bs, ts, x1, att, lx, lg, state_lru_h[0], state_conv[0])
    ks = kf.reshape(1, bs, ts, N_HEADS, 2, D_HEAD_QK)
    vs = vf.reshape(1, bs, ts, N_HEADS, D_HEAD_V)
    return (yp, ys, kp, vp, hp, cp, ks, vs, hs, cs)
```

```python
import functools
import math

import jax
import jax.numpy as jnp
import numpy as np
from jax import lax
from jax.experimental import pallas as pl
from jax.experimental.pallas import tpu as pltpu

F32 = jnp.float32
BF16 = jnp.bfloat16

D_MODEL = 1024
D_ATT = 512
D_LRU = 512
N_HEADS = 4
D_HEAD_V = 128
D_HEAD_QK = 64
CONV_W = 4
LRU_C = 8.0
D_FF = 2816
CHUNK = 64
RMS_EPS = 1e-6
NEG_INF = -1e30
LAMBDA_INIT = 0.8 - 0.6 * math.exp(-0.3 * 0)
LOG2E = math.log2(math.e)

LANES = 128
FF_CHUNK = 256
N_FF_CHUNKS = D_FF // FF_CHUNK
ROW_TILE = 512
LRU_TILE = 512
CACHE_TILE = 1024
VMEM_LIMIT = 56 * 1024 * 1024


def _dot(a, b):
    return jnp.dot(a, b, preferred_element_type=F32)


def _dot_nt(a, b):
    return lax.dot_general(a, b, (((1,), (1,)), ((), ())), preferred_element_type=F32)


def _rms(x, g):
    return x * lax.rsqrt(jnp.mean(x * x, axis=-1, keepdims=True) + RMS_EPS) * g


def _swiglu_ffn(x, g_pre, g_post, wg_ref, wu_ref, wd_ref):
    xn = _rms(x, g_pre).astype(BF16)
    acc = jnp.zeros(x.shape, F32)
    for j in range(N_FF_CHUNKS):
        g = _dot(xn, wg_ref[j])
        u = _dot(xn, wu_ref[j])
        h = (g * jax.nn.sigmoid(g) * u).astype(BF16)
        acc = acc + _dot(h, wd_ref[j])
    return x + 0.5 * _rms(acc, g_post)


def _resident(shape):
    nd = len(shape)
    return pl.BlockSpec(shape, lambda i: (0,) * nd, pipeline_mode=pl.Buffered(1))


def _ffn_inproj_kernel(x_ref, g1a_ref, g1b_ref, gma_ref, wg_ref, wu_ref, wd_ref, wq_ref, wkt_ref, wv_ref, wl_ref,
                       x1_ref, qb_ref, kf_ref, kb_ref, vf_ref, vb_ref, lx_ref, lg_ref, *, seq_layout):
    x1 = _swiglu_ffn(x_ref[...], g1a_ref[...], g1b_ref[...], wg_ref, wu_ref, wd_ref)
    x1_ref[...] = x1
    xm = _rms(x1, gma_ref[...]).astype(BF16)
    tm = xm.shape[0]
    qb_ref[...] = (_dot(xm, wq_ref[...]) * (LOG2E / math.sqrt(D_HEAD_QK))).astype(BF16)
    v = _dot(xm, wv_ref[...])
    vb_ref[...] = v.astype(BF16)
    if seq_layout:
        kt = _dot_nt(wkt_ref[...], xm)
        kf_ref[0] = kt
        kb_ref[0, 0] = kt.astype(BF16)
        for h in range(N_HEADS):
            vf_ref[pl.ds(h, tm, stride=N_HEADS), :] = v[:, h * D_HEAD_V:(h + 1) * D_HEAD_V]
    else:
        k = _dot_nt(xm, wkt_ref[...])
        kf_ref[...] = k
        kb_ref[...] = k.astype(BF16)
        vf_ref[...] = v
    lru = _dot(xm, wl_ref[...])
    lx_ref[...] = lru[:, :D_LRU]
    lg_ref[...] = lru[:, D_LRU:]


def _ffn_inproj(x, g1a, g1b, gma, wg, wu, wd, wq, wkt, wv, wl, seq_len=None):
    m = x.shape[0]
    tm = min(ROW_TILE, m)
    rows = lambda d: pl.BlockSpec((tm, d), lambda i: (i, 0))
    row_out = lambda d, dt: jax.ShapeDtypeStruct((m, d), dt)
    seq_layout = seq_len is not None
    if seq_layout:
        nt = seq_len // tm
        b = m // seq_len
        k_shapes = (jax.ShapeDtypeStruct((b, D_ATT, seq_len), F32), jax.ShapeDtypeStruct((b, nt, D_ATT, tm), BF16))
        k_specs = (pl.BlockSpec((1, D_ATT, tm), lambda i: (i // nt, 0, i % nt)),
                   pl.BlockSpec((1, 1, D_ATT, tm), lambda i: (i // nt, i % nt, 0, 0)))
        vf_shape = jax.ShapeDtypeStruct((m * N_HEADS, D_HEAD_V), F32)
        vf_spec = pl.BlockSpec((tm * N_HEADS, D_HEAD_V), lambda i: (i, 0))
    else:
        k_shapes = (row_out(D_ATT, F32), row_out(D_ATT, BF16))
        k_specs = (rows(D_ATT), rows(D_ATT))
        vf_shape = row_out(D_ATT, F32)
        vf_spec = rows(D_ATT)
    out_shape = (row_out(D_MODEL, F32), row_out(D_ATT, BF16), *k_shapes, vf_shape, row_out(D_ATT, BF16),
                 row_out(D_LRU, F32), row_out(D_LRU, F32))
    out_specs = (rows(D_MODEL), rows(D_ATT), *k_specs, vf_spec, rows(D_ATT), rows(D_LRU), rows(D_LRU))
    weights = (g1a, g1b, gma, wg, wu, wd, wq, wkt, wv, wl)
    return pl.pallas_call(
        functools.partial(_ffn_inproj_kernel, seq_layout=seq_layout),
        grid=(m // tm,),
        in_specs=[rows(D_MODEL)] + [_resident(w.shape) for w in weights],
        out_specs=out_specs,
        out_shape=out_shape,
        compiler_params=pltpu.CompilerParams(dimension_semantics=("arbitrary",), vmem_limit_bytes=VMEM_LIMIT),
        name="ffn_inproj",
    )(x, *weights)


def _outproj_ffn_kernel(att_ref, lru_ref, x1_ref, woa_ref, wob_ref, gmb_ref, g2a_ref, g2b_ref,
                        wg_ref, wu_ref, wd_ref, y_ref):
    mix = _dot(att_ref[...], woa_ref[...]) + _dot(lru_ref[...], wob_ref[...])
    x2 = x1_ref[...] + _rms(mix, gmb_ref[...])
    y_ref[...] = _swiglu_ffn(x2, g2a_ref[...], g2b_ref[...], wg_ref, wu_ref, wd_ref)


def _outproj_ffn(att, lru, x1, woa, wob, gmb, g2a, g2b, wg, wu, wd):
    m = x1.shape[0]
    tm = min(ROW_TILE, m)
    rows = lambda d: pl.BlockSpec((tm, d), lambda i: (i, 0))
    weights = (woa, wob, gmb, g2a, g2b, wg, wu, wd)
    return pl.pallas_call(
        _outproj_ffn_kernel,
        grid=(m // tm,),
        in_specs=[rows(D_ATT), rows(D_LRU), rows(D_MODEL)] + [_resident(w.shape) for w in weights],
        out_specs=rows(D_MODEL),
        out_shape=jax.ShapeDtypeStruct((m, D_MODEL), F32),
        compiler_params=pltpu.CompilerParams(dimension_semantics=("arbitrary",), vmem_limit_bytes=VMEM_LIMIT),
        name="outproj_ffn",
    )(att, lru, x1, *weights)


def _split_maps(q):
    lane = lax.broadcasted_iota(jnp.int32, q.shape, 1)
    zero = jnp.zeros_like(q)
    return jnp.where(lane < D_HEAD_QK, q, zero), jnp.where(lane >= D_HEAD_QK, q, zero)


def _softmax_tile_update(m_ref, l_ref, acc_ref, idx, s, v_t):
    m_prev = m_ref[idx]
    m_new = jnp.maximum(m_prev, jnp.max(s, axis=1, keepdims=True))
    alpha = jnp.exp2(m_prev - m_new)
    ps = [jnp.exp2(s[:, b * LANES:(b + 1) * LANES] - m_new) for b in range(s.shape[1] // LANES)]
    l_ref[idx] = alpha * l_ref[idx] + functools.reduce(lambda a, b: a + b, ps)
    p = jnp.concatenate(ps, axis=1).astype(BF16)
    acc_ref[idx] = alpha * acc_ref[idx] + _dot(p, v_t)
    m_ref[idx] = m_new


def _diff_combine(lam_refs, g, l_ref, acc_ref, i0, i1):
    lq1, lk1, lq2, lk2 = lam_refs
    lam = (jnp.exp(jnp.sum(lq1[...] * lk1[...], axis=1, keepdims=True))
           - jnp.exp(jnp.sum(lq2[...] * lk2[...], axis=1, keepdims=True)) + LAMBDA_INIT)
    l0 = jnp.sum(l_ref[i0], axis=1, keepdims=True)
    l1 = jnp.sum(l_ref[i1], axis=1, keepdims=True)
    o = acc_ref[i0] / l0 - lam * (acc_ref[i1] / l1)
    return _rms(o, g) * (1.0 - LAMBDA_INIT)


def _init_softmax_state(m_ref, l_ref, acc_ref):
    m_ref[...] = jnp.full(m_ref.shape, NEG_INF, F32)
    l_ref[...] = jnp.zeros(l_ref.shape, F32)
    acc_ref[...] = jnp.zeros(acc_ref.shape, F32)


def _bf16_split3(x):
    hi = x.astype(BF16)
    r1 = x - hi.astype(F32)
    lo = r1.astype(BF16)
    lo2 = (r1 - lo.astype(F32)).astype(BF16)
    return hi, lo, lo2


def _attn_prompt_kernel(slopes_ref, q_ref, kt_ref, v_ref, lq1, lk1, lq2, lk2, g_ref, o_ref,
                        kta_ref, va_ref, m_ref, acc_ref, *, tq):
    h = pl.program_id(1)
    qi = pl.program_id(2)
    n_kt = kt_ref.shape[1]
    slope = slopes_ref[h] * LOG2E
    half = D_HEAD_QK

    @pl.when(qi == 0)
    def _():
        row = lax.broadcasted_iota(jnp.int32, (LANES, tq), 0)
        col = lax.broadcasted_iota(jnp.int32, (1, tq), 1)
        terms = _bf16_split3(slope * col.astype(F32))
        zero = jnp.zeros((LANES, tq), F32)

        def bias_rows(first):
            blk = zero
            for i, t in enumerate(terms):
                blk = jnp.where(row == first + i, jnp.broadcast_to(t.astype(F32), (LANES, tq)), blk)
            return blk

        bias0, bias1 = bias_rows(half), bias_rows(0)

        def fill(n, carry):
            kt = kt_ref[0, n].astype(F32)
            kta_ref[0, n] = jnp.where(row < half, kt, bias0).astype(BF16)
            kta_ref[1, n] = jnp.where(row >= half, kt, bias1).astype(BF16)
            r0 = pl.multiple_of(n * tq, tq)
            va_ref[pl.ds(r0, tq), 0:D_HEAD_V] = v_ref[0, pl.ds(r0, tq), :]
            va_ref[pl.ds(r0, tq), D_HEAD_V:2 * D_HEAD_V] = jnp.ones((tq, D_HEAD_V), BF16)
            return carry

        lax.fori_loop(0, n_kt, fill, 0)

    q = q_ref[0]
    lane = lax.broadcasted_iota(jnp.int32, q.shape, 1)
    ones0 = ((lane >= half) & (lane < half + 3)).astype(F32).astype(BF16)
    ones1 = (lane < 3).astype(F32).astype(BF16)
    qa = (jnp.where(lane < half, q, ones0), jnp.where(lane >= half, q, ones1))
    m_ref[...] = jnp.full(m_ref.shape, NEG_INF, F32)
    acc_ref[...] = jnp.zeros(acc_ref.shape, F32)

    def update(logits, tile_bias, v_t):
        m_prev = [m_ref[c] for c in range(2)]
        acc_prev = [acc_ref[c] for c in range(2)]
        m_next, acc_next = [], []
        for c in range(2):
            s = logits[c]
            m_new = jnp.maximum(m_prev[c], jnp.max(s, axis=1, keepdims=True) + tile_bias)
            alpha = jnp.exp2(m_prev[c] - m_new)
            shift = m_new - tile_bias
            p = jnp.concatenate([jnp.exp2((s[:, b * LANES:(b + 1) * LANES] - shift).astype(BF16))
                                 for b in range(s.shape[1] // LANES)], axis=1)
            acc_next.append(jnp.concatenate([alpha, alpha], axis=1) * acc_prev[c] + _dot(p, v_t))
            m_next.append(m_new)
        for c in range(2):
            acc_ref[c] = acc_next[c]
            m_ref[c] = m_next[c]

    def before_tile(j, carry):
        v_t = va_ref[pl.ds(pl.multiple_of(j * tq, tq), tq), :]
        tile_bias = slope * ((j - qi) * tq).astype(F32)
        update([_dot(qa[c], kta_ref[c, j]) for c in range(2)], tile_bias, v_t)
        return carry

    lax.fori_loop(0, qi, before_tile, 0)

    v_t = va_ref[pl.ds(pl.multiple_of(qi * tq, tq), tq), :]
    row = lax.broadcasted_iota(jnp.int32, (tq, tq), 0)
    col = lax.broadcasted_iota(jnp.int32, (tq, tq), 1)
    after_fix = slope * jnp.minimum(2 * (row - col), 0).astype(F32)
    visible = (row // CHUNK) >= (col // CHUNK)
    update([jnp.where(visible, _dot(qa[c], kta_ref[c, qi]) + after_fix, NEG_INF) for c in range(2)], 0.0, v_t)

    lam = (jnp.exp(jnp.sum(lq1[...] * lk1[...], axis=1, keepdims=True))
           - jnp.exp(jnp.sum(lq2[...] * lk2[...], axis=1, keepdims=True)) + LAMBDA_INIT)
    a0, a1 = acc_ref[0], acc_ref[1]
    o = a0[:, :D_HEAD_V] / a0[:, D_HEAD_V:] - lam * (a1[:, :D_HEAD_V] / a1[:, D_HEAD_V:])
    o_ref[0] = (_rms(o, g_ref[...]) * (1.0 - LAMBDA_INIT)).astype(o_ref.dtype)


def _attn_prompt(slopes, q, kt, v, lq1, lk1, lq2, lk2, g):
    b, t, _ = q.shape
    n_kt, tq = kt.shape[1], kt.shape[3]
    small = lambda a: pl.BlockSpec(a.shape, lambda bi, hi, qi: (0, 0))
    return pl.pallas_call(
        functools.partial(_attn_prompt_kernel, tq=tq),
        grid=(b, N_HEADS, t // tq),
        in_specs=[pl.BlockSpec(memory_space=pltpu.SMEM),
                  pl.BlockSpec((1, tq, LANES), lambda bi, hi, qi: (bi, qi, hi)),
                  pl.BlockSpec((1, n_kt, LANES, tq), lambda bi, hi, qi: (bi, 0, hi, 0)),
                  pl.BlockSpec((1, t, LANES), lambda bi, hi, qi: (bi, 0, hi)),
                  small(lq1), small(lk1), small(lq2), small(lk2), small(g)],
        out_specs=pl.BlockSpec((1, tq, LANES), lambda bi, hi, qi: (bi, qi, hi)),
        out_shape=jax.ShapeDtypeStruct((b, t, D_ATT), BF16),
        scratch_shapes=[pltpu.VMEM((2, n_kt, LANES, tq), BF16), pltpu.VMEM((t, 2 * D_HEAD_V), BF16),
                        pltpu.VMEM((2, tq, LANES), F32), pltpu.VMEM((2, tq, 2 * D_HEAD_V), F32)],
        compiler_params=pltpu.CompilerParams(dimension_semantics=("arbitrary",) * 3, vmem_limit_bytes=VMEM_LIMIT),
        name="attn_prompt",
    )(slopes, q, kt, v, lq1, lk1, lq2, lk2, g)


def _attn_sample_kernel(slopes_ref, q_ref, ckt_ref, cv_ref, kn_ref, vn_ref, lq1, lk1, lq2, lk2, g_ref, o_ref,
                        m_ref, l_ref, acc_ref, *, past, tk):
    j = pl.program_id(1)
    tq = q_ref.shape[1]

    @pl.when(j == 0)
    def _():
        _init_softmax_state(m_ref, l_ref, acc_ref)

    def attend(n, k_start, n_valid, scores, values):
        q_pos = past + lax.broadcasted_iota(jnp.int32, (tq, n), 0)
        k_pos = k_start + lax.broadcasted_iota(jnp.int32, (tq, n), 1)
        dist = jnp.abs(q_pos - k_pos).astype(F32)
        visible = ((q_pos // CHUNK) >= (k_pos // CHUNK)) & (k_pos < k_start + n_valid)
        for h in range(N_HEADS):
            cols = slice(h * LANES, (h + 1) * LANES)
            qz = _split_maps(q_ref[0, :, cols])
            v_h = values(h)
            for c in range(2):
                s = jnp.where(visible, scores(qz[c], h) - (slopes_ref[h] * LOG2E) * dist, NEG_INF)
                _softmax_tile_update(m_ref, l_ref, acc_ref, 2 * h + c, s, v_h)

    attend(tk, j * tk, tk,
           lambda qc, h: _dot(qc, ckt_ref[0, h * LANES:(h + 1) * LANES, :].astype(BF16)),
           lambda h: cv_ref[0, pl.ds(h, tk, stride=N_HEADS), :].astype(BF16))

    @pl.when(j == pl.num_programs(1) - 1)
    def _():
        attend(kn_ref.shape[1], past, tq,
               lambda qc, h: _dot_nt(qc, kn_ref[0, :, h * LANES:(h + 1) * LANES]),
               lambda h: vn_ref[0, :, h * LANES:(h + 1) * LANES])
        for h in range(N_HEADS):
            o = _diff_combine((lq1, lk1, lq2, lk2), g_ref[...], l_ref, acc_ref, 2 * h, 2 * h + 1)
            o_ref[0, :, h * LANES:(h + 1) * LANES] = o.astype(o_ref.dtype)


def _attn_sample(slopes, q, cache_kt, cache_v_rows, k_new, v_new, lq1, lk1, lq2, lk2, g):
    b, t, _ = q.shape
    past = cache_kt.shape[2]
    tk = min(CACHE_TILE, past)
    k_new = jnp.pad(k_new, ((0, 0), (0, -t % LANES), (0, 0)))
    v_new = jnp.pad(v_new, ((0, 0), (0, -t % LANES), (0, 0)))
    small = lambda a: pl.BlockSpec(a.shape, lambda bi, j: (0, 0))
    per_stream = lambda a: pl.BlockSpec((1,) + a.shape[1:], lambda bi, j: (bi, 0, 0))
    return pl.pallas_call(
        functools.partial(_attn_sample_kernel, past=past, tk=tk),
        grid=(b, past // tk),
        in_specs=[pl.BlockSpec(memory_space=pltpu.SMEM),
                  per_stream(q),
                  pl.BlockSpec((1, D_ATT, tk), lambda bi, j: (bi, 0, j)),
                  pl.BlockSpec((1, tk * N_HEADS, D_HEAD_V), lambda bi, j: (bi, j, 0)),
                  per_stream(k_new), per_stream(v_new),
                  small(lq1), small(lk1), small(lq2), small(lk2), small(g)],
        out_specs=pl.BlockSpec((1, t, D_ATT), lambda bi, j: (bi, 0, 0)),
        out_shape=jax.ShapeDtypeStruct((b, t, D_ATT), BF16),
        scratch_shapes=[pltpu.VMEM((2 * N_HEADS, t, LANES), F32)] * 3,
        compiler_params=pltpu.CompilerParams(dimension_semantics=("arbitrary",) * 2, vmem_limit_bytes=VMEM_LIMIT),
        name="attn_sample",
    )(slopes, q, cache_kt, cache_v_rows, k_new, v_new, lq1, lk1, lq2, lk2, g)


def _lru_kernel(x_ref, gate_ref, h0_ref, cbuf_ref, cw_ref, cb_ref, wr_ref, br_ref, wi_ref, bi_ref, lam_ref,
                out_ref, hlast_ref, xbuf_ref, h_ref, *, tt):
    ti = pl.program_id(1)
    pad = xbuf_ref.shape[0] - tt

    @pl.when(ti == 0)
    def _():
        xbuf_ref[0:pad, :] = cbuf_ref[0]
        h_ref[...] = h0_ref[0]

    @pl.when(ti > 0)
    def _():
        xbuf_ref[0:pad, :] = xbuf_ref[tt:tt + pad, :]

    xbuf_ref[pad:pad + tt, :] = x_ref[0]
    xc = cb_ref[...] + xbuf_ref[pad:pad + tt, :] * cw_ref[CONV_W - 1:CONV_W, :]
    for j in range(CONV_W - 1):
        back = CONV_W - 1 - j
        xc = xc + xbuf_ref[pad - back:pad - back + tt, :] * cw_ref[j:j + 1, :]

    xb = xc.astype(BF16)
    r = jax.nn.sigmoid(_dot(xb, wr_ref[...]) + br_ref[...])
    i = jax.nn.sigmoid(_dot(xb, wi_ref[...]) + bi_ref[...])
    neg_lam = -lam_ref[...]
    softplus = jnp.maximum(neg_lam, 0.0) + jnp.log1p(jnp.exp(-jnp.abs(neg_lam)))
    log_a = -LRU_C * r * softplus
    a = jnp.exp(log_a)
    b = jnp.sqrt(-jnp.tanh(log_a) * (1.0 + a * a)) * (i * xc)

    row = lax.broadcasted_iota(jnp.int32, (tt, D_LRU), 0)
    d = 1
    while d < tt:
        has_prev = row >= d
        a_prev = jnp.where(has_prev, pltpu.roll(a, d, 0), 1.0)
        b_prev = jnp.where(has_prev, pltpu.roll(b, d, 0), 0.0)
        b = a * b_prev + b
        a = a * a_prev
        d *= 2
    hs = a * h_ref[...] + b
    h_new = hs[tt - 1:tt, :]
    h_ref[...] = h_new
    hlast_ref[0] = h_new

    gate = gate_ref[0]
    gelu = 0.5 * gate * (1.0 + jnp.tanh(math.sqrt(2.0 / math.pi) * (gate + 0.044715 * (gate * gate * gate))))
    out_ref[0] = (hs * gelu).astype(out_ref.dtype)


def _lru(x, gate, h0, cbuf, cw, cb, wr, br, wi, bi, lam):
    b, t, _ = x.shape
    tt = min(LRU_TILE, t)
    pad = cbuf.shape[1]
    small = lambda a: pl.BlockSpec(a.shape, lambda bi_, ti: (0, 0))
    tile = pl.BlockSpec((1, tt, D_LRU), lambda bi_, ti: (bi_, ti, 0))
    per_stream = lambda a: pl.BlockSpec((1,) + a.shape[1:], lambda bi_, ti: (bi_, 0, 0))
    return pl.pallas_call(
        functools.partial(_lru_kernel, tt=tt),
        grid=(b, t // tt),
        in_specs=[tile, tile, per_stream(h0), per_stream(cbuf), small(cw), small(cb),
                  small(wr), small(br), small(wi), small(bi), small(lam)],
        out_specs=(tile, pl.BlockSpec((1, 1, D_LRU), lambda bi_, ti: (bi_, 0, 0))),
        out_shape=(jax.ShapeDtypeStruct((b, t, D_LRU), BF16), jax.ShapeDtypeStruct((b, 1, D_LRU), F32)),
        scratch_shapes=[pltpu.VMEM((tt + pad, D_LRU), F32), pltpu.VMEM((1, D_LRU), F32)],
        compiler_params=pltpu.CompilerParams(dimension_semantics=("arbitrary",) * 2, vmem_limit_bytes=VMEM_LIMIT),
        name="lru",
    )(x, gate, h0, cbuf, cw, cb, wr, br, wi, bi, lam)


def _block_diag_dense(w):
    n, c, _ = w.shape
    eye = jnp.eye(n, dtype=w.dtype)
    return (eye[:, None, :, None] * w[:, :, None, :]).reshape(n * c, n * c)


def _chunk_cols(w):
    return w.reshape(w.shape[0], N_FF_CHUNKS, FF_CHUNK).transpose(1, 0, 2)


def kernel(x_prompt, x_sample, cache_k, cache_v, state_lru_h, state_conv, w_in, w_out, lambda_q1, lambda_k1,
           lambda_q2, lambda_k2, subln_g, conv_w, conv_b, w_rgate, b_rgate, w_igate, b_igate, lru_lambda,
           ffn1_w_gate, ffn1_w_up, ffn1_w_down, ffn2_w_gate, ffn2_w_up, ffn2_w_down,
           g_ffn1_pre, g_ffn1_post, g_mix_pre, g_mix_post, g_ffn2_pre, g_ffn2_post):
    bp, tp, _ = x_prompt.shape
    bs, ts, _ = x_sample.shape
    past = cache_k.shape[2]

    wg1, wu1 = _chunk_cols(ffn1_w_gate[0].astype(BF16)), _chunk_cols(ffn1_w_up[0].astype(BF16))
    wd1 = ffn1_w_down[0].astype(BF16).reshape(N_FF_CHUNKS, FF_CHUNK, D_MODEL)
    wg2, wu2 = _chunk_cols(ffn2_w_gate[0].astype(BF16)), _chunk_cols(ffn2_w_up[0].astype(BF16))
    wd2 = ffn2_w_down[0].astype(BF16).reshape(N_FF_CHUNKS, FF_CHUNK, D_MODEL)
    win = w_in[0].astype(BF16)
    wq, wkt, wv, wl = win[:, :D_ATT], win[:, D_ATT:2 * D_ATT].T, win[:, 2 * D_ATT:3 * D_ATT], win[:, 3 * D_ATT:]
    woa, wob = w_out[0, :D_ATT].astype(BF16), w_out[0, D_ATT:].astype(BF16)
    wr = _block_diag_dense(w_rgate[0]).astype(BF16)
    wi = _block_diag_dense(w_igate[0]).astype(BF16)
    slopes = jnp.asarray(2.0 ** (-8.0 * np.arange(1, N_HEADS + 1) / N_HEADS), dtype=F32)
    lam_vecs = (lambda_q1, lambda_k1, lambda_q2, lambda_k2)
    ffn1 = (g_ffn1_pre, g_ffn1_post, g_mix_pre, wg1, wu1, wd1, wq, wkt, wv, wl)
    lru_params = (conv_w[0], conv_b, wr, b_rgate, wi, b_igate, lru_lambda)
    tail_pad = 8 - (CONV_W - 1)

    def mix_and_ffn2(b, t, x1, att, lx, lg, h0, conv_hist):
        lx3 = lx.reshape(b, t, D_LRU)
        cbuf = jnp.pad(conv_hist, ((0, 0), (tail_pad, 0), (0, 0)))
        lru_out, h_last = _lru(lx3, lg.reshape(b, t, D_LRU), h0.reshape(b, 1, D_LRU), cbuf, *lru_params)
        y = _outproj_ffn(att.reshape(b * t, D_ATT), lru_out.reshape(b * t, D_LRU), x1, woa, wob,
                         g_mix_post, g_ffn2_pre, g_ffn2_post, wg2, wu2, wd2)
        conv_new = jnp.concatenate([conv_hist, lx3], axis=1)[:, -(CONV_W - 1):]
        return y.reshape(b, t, D_MODEL), h_last.reshape(1, b, D_LRU), conv_new[None]

    x1, qb, kt, ktb, v_rows, vb, lx, lg = _ffn_inproj(x_prompt.reshape(bp * tp, D_MODEL), *ffn1, seq_len=tp)
    att = _attn_prompt(slopes, qb.reshape(bp, tp, D_ATT), ktb, vb.reshape(bp, tp, D_ATT), *lam_vecs, subln_g)
    yp, hp, cp = mix_and_ffn2(bp, tp, x1, att, lx, lg, jnp.zeros((bp, D_LRU), F32),
                              jnp.zeros((bp, CONV_W - 1, D_LRU), F32))
    kp = kt.reshape(bp, N_HEADS, 2, D_HEAD_QK, tp).transpose(0, 4, 1, 2, 3)[None]
    vp = v_rows.reshape(1, bp, tp, N_HEADS, D_HEAD_V)

    x1, qb, kf, kb, vf, vb, lx, lg = _ffn_inproj(x_sample.reshape(bs * ts, D_MODEL), *ffn1)
    cache_kt = cache_k[0].transpose(0, 2, 3, 4, 1).reshape(bs, D_ATT, past)
    cache_v_rows = cache_v[0].reshape(bs, past * N_HEADS, D_HEAD_V)
    att = _attn_sample(slopes, qb.reshape(bs, ts, D_ATT), cache_kt, cache_v_rows, kb.reshape(bs, ts, D_ATT),
                       vb.reshape(bs, ts, D_ATT), *lam_vecs, subln_g)
    ys, hs, cs = mix_and_ffn2(bs, ts, x1, att, lx, lg, state_lru_h[0], state_conv[0])
    ks = kf.reshape(1, bs, ts, N_HEADS, 2, D_HEAD_QK)
    vs = vf.reshape(1, bs, ts, N_HEADS, D_HEAD_V)
    return (yp, ys, kp, vp, hp, cp, ks, vs, hs, cs)
```

```python
import functools
import math

import jax
import jax.numpy as jnp
import numpy as np
from jax import lax
from jax.experimental import pallas as pl
from jax.experimental.pallas import tpu as pltpu

F32 = jnp.float32
BF16 = jnp.bfloat16

D_MODEL = 1024
D_ATT = 512
D_LRU = 512
N_HEADS = 4
D_HEAD_V = 128
D_HEAD_QK = 64
CONV_W = 4
LRU_C = 8.0
D_FF = 2816
CHUNK = 64
RMS_EPS = 1e-6
NEG_INF = -1e30
LAMBDA_INIT = 0.8 - 0.6 * math.exp(-0.3 * 0)
LOG2E = math.log2(math.e)

LANES = 128
SUBLANES = 8
FF_CHUNK = 256
N_FF_CHUNKS = D_FF // FF_CHUNK
ROW_TILE = 512
LRU_TILE = 512
CACHE_TILE = 1024
VMEM_LIMIT = 56 * 1024 * 1024


def _dot(a, b):
    return jnp.dot(a, b, preferred_element_type=F32)


def _dot_nt(a, b):
    return lax.dot_general(a, b, (((1,), (1,)), ((), ())), preferred_element_type=F32)


def _rms(x, g):
    return x * lax.rsqrt(jnp.mean(x * x, axis=-1, keepdims=True) + RMS_EPS) * g


def _swiglu_ffn(x, g_pre, g_post, wg_ref, wu_ref, wd_ref):
    xn = _rms(x, g_pre).astype(BF16)
    acc = jnp.zeros(x.shape, F32)
    for j in range(N_FF_CHUNKS):
        g = _dot(xn, wg_ref[j])
        u = _dot(xn, wu_ref[j])
        h = (g * jax.nn.sigmoid(g) * u).astype(BF16)
        acc = acc + _dot(h, wd_ref[j])
    return x + 0.5 * _rms(acc, g_post)


def _resident(shape):
    nd = len(shape)
    return pl.BlockSpec(shape, lambda i: (0,) * nd, pipeline_mode=pl.Buffered(1))


def _ffn_inproj_kernel(x_ref, g1a_ref, g1b_ref, gma_ref, wg_ref, wu_ref, wd_ref, wq_ref, wkt_ref, wv_ref, wl_ref,
                       x1_ref, qb_ref, kf_ref, kb_ref, vf_ref, vb_ref, lx_ref, lg_ref, *, seq_layout):
    x1 = _swiglu_ffn(x_ref[...], g1a_ref[...], g1b_ref[...], wg_ref, wu_ref, wd_ref)
    x1_ref[...] = x1
    xm = _rms(x1, gma_ref[...]).astype(BF16)
    tm = xm.shape[0]
    qb_ref[...] = (_dot(xm, wq_ref[...]) * (LOG2E / math.sqrt(D_HEAD_QK))).astype(BF16)
    v = _dot(xm, wv_ref[...])
    vb_ref[...] = v.astype(BF16)
    if seq_layout:
        kt = _dot_nt(wkt_ref[...], xm)
        kf_ref[0] = kt
        kb_ref[0, 0] = kt.astype(BF16)
        for h in range(N_HEADS):
            vf_ref[pl.ds(h, tm, stride=N_HEADS), :] = v[:, h * D_HEAD_V:(h + 1) * D_HEAD_V]
    else:
        k = _dot_nt(xm, wkt_ref[...])
        kf_ref[...] = k
        kb_ref[...] = k.astype(BF16)
        vf_ref[...] = v
    lru = _dot(xm, wl_ref[...])
    lx_ref[...] = lru[:, :D_LRU]
    lg_ref[...] = lru[:, D_LRU:]


def _ffn_inproj(x, g1a, g1b, gma, wg, wu, wd, wq, wkt, wv, wl, seq_len=None):
    m = x.shape[0]
    tm = min(ROW_TILE, m)
    rows = lambda d: pl.BlockSpec((tm, d), lambda i: (i, 0))
    row_out = lambda d, dt: jax.ShapeDtypeStruct((m, d), dt)
    seq_layout = seq_len is not None
    if seq_layout:
        nt = seq_len // tm
        b = m // seq_len
        k_shapes = (jax.ShapeDtypeStruct((b, D_ATT, seq_len), F32), jax.ShapeDtypeStruct((b, nt, D_ATT, tm), BF16))
        k_specs = (pl.BlockSpec((1, D_ATT, tm), lambda i: (i // nt, 0, i % nt)),
                   pl.BlockSpec((1, 1, D_ATT, tm), lambda i: (i // nt, i % nt, 0, 0)))
        vf_shape = jax.ShapeDtypeStruct((m * N_HEADS, D_HEAD_V), F32)
        vf_spec = pl.BlockSpec((tm * N_HEADS, D_HEAD_V), lambda i: (i, 0))
    else:
        k_shapes = (row_out(D_ATT, F32), row_out(D_ATT, BF16))
        k_specs = (rows(D_ATT), rows(D_ATT))
        vf_shape = row_out(D_ATT, F32)
        vf_spec = rows(D_ATT)
    out_shape = (row_out(D_MODEL, F32), row_out(D_ATT, BF16), *k_shapes, vf_shape, row_out(D_ATT, BF16),
                 row_out(D_LRU, F32), row_out(D_LRU, F32))
    out_specs = (rows(D_MODEL), rows(D_ATT), *k_specs, vf_spec, rows(D_ATT), rows(D_LRU), rows(D_LRU))
    weights = (g1a, g1b, gma, wg, wu, wd, wq, wkt, wv, wl)
    return pl.pallas_call(
        functools.partial(_ffn_inproj_kernel, seq_layout=seq_layout),
        grid=(m // tm,),
        in_specs=[rows(D_MODEL)] + [_resident(w.shape) for w in weights],
        out_specs=out_specs,
        out_shape=out_shape,
        compiler_params=pltpu.CompilerParams(dimension_semantics=("arbitrary",), vmem_limit_bytes=VMEM_LIMIT),
        name="ffn_inproj",
    )(x, *weights)


def _outproj_ffn_kernel(att_ref, lru_ref, x1_ref, woa_ref, wob_ref, gmb_ref, g2a_ref, g2b_ref,
                        wg_ref, wu_ref, wd_ref, y_ref):
    mix = _dot(att_ref[...], woa_ref[...]) + _dot(lru_ref[...], wob_ref[...])
    x2 = x1_ref[...] + _rms(mix, gmb_ref[...])
    y_ref[...] = _swiglu_ffn(x2, g2a_ref[...], g2b_ref[...], wg_ref, wu_ref, wd_ref)


def _outproj_ffn(att, lru, x1, woa, wob, gmb, g2a, g2b, wg, wu, wd):
    m = x1.shape[0]
    tm = min(ROW_TILE, m)
    rows = lambda d: pl.BlockSpec((tm, d), lambda i: (i, 0))
    weights = (woa, wob, gmb, g2a, g2b, wg, wu, wd)
    return pl.pallas_call(
        _outproj_ffn_kernel,
        grid=(m // tm,),
        in_specs=[rows(D_ATT), rows(D_LRU), rows(D_MODEL)] + [_resident(w.shape) for w in weights],
        out_specs=rows(D_MODEL),
        out_shape=jax.ShapeDtypeStruct((m, D_MODEL), F32),
        compiler_params=pltpu.CompilerParams(dimension_semantics=("arbitrary",), vmem_limit_bytes=VMEM_LIMIT),
        name="outproj_ffn",
    )(att, lru, x1, *weights)


def _split_maps(q):
    lane = lax.broadcasted_iota(jnp.int32, q.shape, 1)
    zero = jnp.zeros_like(q)
    return jnp.where(lane < D_HEAD_QK, q, zero), jnp.where(lane >= D_HEAD_QK, q, zero)


def _init_softmax_state(m_ref, l_ref, acc_ref):
    m_ref[...] = jnp.full(m_ref.shape, NEG_INF, F32)
    l_ref[...] = jnp.zeros(l_ref.shape, F32)
    acc_ref[...] = jnp.zeros(acc_ref.shape, F32)


def _bf16_split3(x):
    hi = x.astype(BF16)
    r1 = x - hi.astype(F32)
    lo = r1.astype(BF16)
    lo2 = (r1 - lo.astype(F32)).astype(BF16)
    return hi, lo, lo2


def _attn_prompt_kernel(slopes_ref, q_ref, kt_ref, v_ref, lq1, lk1, lq2, lk2, g_ref, o_ref,
                        kta_ref, va_ref, mask_ref, m_ref, acc_ref, pa_ref, pb_ref, *, tq):
    h = pl.program_id(1)
    qi = pl.program_id(2)
    n_kt = kt_ref.shape[1]
    slope = slopes_ref[h] * LOG2E
    half = D_HEAD_QK

    @pl.when(qi == 0)
    def _():
        row = lax.broadcasted_iota(jnp.int32, (LANES, tq), 0)
        col = lax.broadcasted_iota(jnp.int32, (1, tq), 1)
        terms = _bf16_split3(slope * col.astype(F32))
        zero = jnp.zeros((LANES, tq), F32)

        def bias_rows(first):
            blk = zero
            for i, t in enumerate(terms):
                blk = jnp.where(row == first + i, jnp.broadcast_to(t.astype(F32), (LANES, tq)), blk)
            return blk

        bias0, bias1 = bias_rows(half), bias_rows(0)

        def fill(n, carry):
            kt = kt_ref[0, n].astype(F32)
            kta_ref[0, n] = jnp.where(row < half, kt, bias0).astype(BF16)
            kta_ref[1, n] = jnp.where(row >= half, kt, bias1).astype(BF16)
            r0 = pl.multiple_of(n * tq, tq)
            va_ref[pl.ds(r0, tq), 0:D_HEAD_V] = v_ref[0, pl.ds(r0, tq), :]
            va_ref[pl.ds(r0, tq), D_HEAD_V:2 * D_HEAD_V] = jnp.ones((tq, D_HEAD_V), BF16)
            return carry

        lax.fori_loop(0, n_kt, fill, 0)

        qrow = lax.broadcasted_iota(jnp.int32, (tq, tq), 0)
        kcol = lax.broadcasted_iota(jnp.int32, (tq, tq), 1)
        mask_ref[...] = jnp.where((qrow // CHUNK) >= (kcol // CHUNK),
                                  slope * jnp.minimum(2 * (qrow - kcol), 0).astype(F32), NEG_INF)
        acc_ref[...] = jnp.zeros(acc_ref.shape, F32)
        pb_ref[...] = jnp.zeros(pb_ref.shape, BF16)

    q = q_ref[0]
    lane = lax.broadcasted_iota(jnp.int32, q.shape, 1)
    ones0 = ((lane >= half) & (lane < half + 3)).astype(F32).astype(BF16)
    ones1 = (lane < 3).astype(F32).astype(BF16)
    qa = (jnp.where(lane < half, q, ones0), jnp.where(lane >= half, q, ones1))
    m_ref[...] = jnp.full(m_ref.shape, NEG_INF, F32)

    def values(j):
        return va_ref[pl.ds(pl.multiple_of(j * tq, tq), tq), :]

    def softmax_tile(s, m_prev, tile_bias):
        m_new = jnp.maximum(m_prev, jnp.max(s, axis=1, keepdims=True) + tile_bias)
        shift = m_new - tile_bias
        p = jnp.concatenate([jnp.exp2((s[:, b * LANES:(b + 1) * LANES] - shift).astype(BF16))
                             for b in range(s.shape[1] // LANES)], axis=1)
        return m_new, p

    def update(j, p_in_ref, p_out_ref, diagonal=False):
        logits = [_dot(qa[c], kta_ref[c, j]) for c in range(2)]
        if diagonal:
            mask = mask_ref[...]
            logits = [s + mask for s in logits]
        tile_bias = slope * ((j - qi) * tq).astype(F32)
        v_prev = values(jnp.maximum(j - 1, 0))
        pv = [_dot(p_in_ref[c], v_prev) for c in range(2)]
        m_prev = [m_ref[c] for c in range(2)]
        acc_prev = [acc_ref[c] for c in range(2)]
        m_next, acc_next, p_next = [], [], []
        for c in range(2):
            m_new, p = softmax_tile(logits[c], m_prev[c], tile_bias)
            alpha = jnp.exp2(m_prev[c] - m_new)
            acc = jnp.concatenate([alpha, alpha], axis=1) * (acc_prev[c] + pv[c])
            if diagonal:
                acc = acc + _dot(p, values(j))
            acc_next.append(acc)
            m_next.append(m_new)
            p_next.append(p)
        for c in range(2):
            acc_ref[c] = acc_next[c]
            m_ref[c] = m_next[c]
            if not diagonal:
                p_out_ref[c] = p_next[c]

    odd = qi % 2

    @pl.when(odd == 1)
    def _():
        tile_bias = slope * (-qi * tq).astype(F32)
        for c in range(2):
            m_new, p = softmax_tile(_dot(qa[c], kta_ref[c, 0]), m_ref[c], tile_bias)
            m_ref[c] = m_new
            pb_ref[c] = p
        acc_ref[...] = jnp.zeros(acc_ref.shape, F32)

    def pair(i, carry):
        j = odd + 2 * i
        update(j, pb_ref, pa_ref)

        @pl.when(j + 1 < qi)
        def _():
            update(j + 1, pa_ref, pb_ref)

        return carry

    lax.fori_loop(0, qi // 2, pair, 0)
    update(qi, pb_ref, None, diagonal=True)

    lam = (jnp.exp(jnp.sum(lq1[...] * lk1[...], axis=1, keepdims=True))
           - jnp.exp(jnp.sum(lq2[...] * lk2[...], axis=1, keepdims=True)) + LAMBDA_INIT)
    a0, a1 = acc_ref[0], acc_ref[1]
    o = a0[:, :D_HEAD_V] / a0[:, D_HEAD_V:] - lam * (a1[:, :D_HEAD_V] / a1[:, D_HEAD_V:])
    o_ref[0] = (_rms(o, g_ref[...]) * (1.0 - LAMBDA_INIT)).astype(o_ref.dtype)


def _attn_prompt(slopes, q, kt, v, lq1, lk1, lq2, lk2, g):
    b, t, _ = q.shape
    n_kt, tq = kt.shape[1], kt.shape[3]
    small = lambda a: pl.BlockSpec(a.shape, lambda bi, hi, qi: (0, 0))
    return pl.pallas_call(
        functools.partial(_attn_prompt_kernel, tq=tq),
        grid=(b, N_HEADS, t // tq),
        in_specs=[pl.BlockSpec(memory_space=pltpu.SMEM),
                  pl.BlockSpec((1, tq, LANES), lambda bi, hi, qi: (bi, qi, hi)),
                  pl.BlockSpec((1, n_kt, LANES, tq), lambda bi, hi, qi: (bi, 0, hi, 0)),
                  pl.BlockSpec((1, t, LANES), lambda bi, hi, qi: (bi, 0, hi)),
                  small(lq1), small(lk1), small(lq2), small(lk2), small(g)],
        out_specs=pl.BlockSpec((1, tq, LANES), lambda bi, hi, qi: (bi, qi, hi)),
        out_shape=jax.ShapeDtypeStruct((b, t, D_ATT), BF16),
        scratch_shapes=[pltpu.VMEM((2, n_kt, LANES, tq), BF16), pltpu.VMEM((t, 2 * D_HEAD_V), BF16),
                        pltpu.VMEM((tq, tq), F32), pltpu.VMEM((2, tq, LANES), F32),
                        pltpu.VMEM((2, tq, 2 * D_HEAD_V), F32)] + [pltpu.VMEM((2, tq, tq), BF16)] * 2,
        compiler_params=pltpu.CompilerParams(dimension_semantics=("arbitrary",) * 3, vmem_limit_bytes=VMEM_LIMIT),
        name="attn_prompt",
    )(slopes, q, kt, v, lq1, lk1, lq2, lk2, g)


def _attn_sample_kernel(slopes_ref, q_ref, ckt_ref, cv_ref, kn_ref, vn_ref, lq1, lk1, lq2, lk2, g_ref, o_ref,
                        m_ref, l_ref, acc_ref, *, past, tk):
    j = pl.program_id(1)
    tq = q_ref.shape[1]

    @pl.when(j == 0)
    def _():
        _init_softmax_state(m_ref, l_ref, acc_ref)

    def attend(n, k_start, n_valid, scores, values):
        q_pos = past + lax.broadcasted_iota(jnp.int32, (tq, n), 0)
        k_pos = k_start + lax.broadcasted_iota(jnp.int32, (tq, n), 1)
        dist = jnp.abs(q_pos - k_pos).astype(F32)
        dist = jnp.concatenate([dist, dist], axis=0)
        visible = ((q_pos // CHUNK) >= (k_pos // CHUNK)) & (k_pos < k_start + n_valid)
        visible = jnp.concatenate([visible, visible], axis=0)
        state = [(m_ref[h], l_ref[h], acc_ref[h]) for h in range(N_HEADS)]
        for h in range(N_HEADS):
            m_prev, l_prev, acc_prev = state[h]
            q2 = jnp.concatenate(_split_maps(q_ref[0, :, h * LANES:(h + 1) * LANES]), axis=0)
            s = jnp.where(visible, scores(q2, h) - (slopes_ref[h] * LOG2E) * dist, NEG_INF)
            m_new = jnp.maximum(m_prev, jnp.max(s, axis=1, keepdims=True))
            alpha = jnp.exp2(m_prev - m_new)
            ps = [jnp.exp2(s[:, b * LANES:(b + 1) * LANES] - m_new) for b in range(n // LANES)]
            p = jnp.concatenate(ps, axis=1).astype(BF16)
            state[h] = (m_new, alpha * l_prev + functools.reduce(lambda x, y: x + y, ps),
                        alpha * acc_prev + _dot(p, values(h)))
        for h in range(N_HEADS):
            m_ref[h], l_ref[h], acc_ref[h] = state[h]

    attend(tk, j * tk, tk,
           lambda q2, h: _dot(q2, ckt_ref[0, h * LANES:(h + 1) * LANES, :].astype(BF16)),
           lambda h: cv_ref[0, pl.ds(h, tk, stride=N_HEADS), :].astype(BF16))

    @pl.when(j == pl.num_programs(1) - 1)
    def _():
        attend(kn_ref.shape[1], past, tq,
               lambda q2, h: _dot_nt(q2, kn_ref[0, :, h * LANES:(h + 1) * LANES]),
               lambda h: vn_ref[0, :, h * LANES:(h + 1) * LANES])
        lam = (jnp.exp(jnp.sum(lq1[...] * lk1[...], axis=1, keepdims=True))
               - jnp.exp(jnp.sum(lq2[...] * lk2[...], axis=1, keepdims=True)) + LAMBDA_INIT)
        for h in range(N_HEADS):
            sm = acc_ref[h] / jnp.sum(l_ref[h], axis=1, keepdims=True)
            o = _rms(sm[:tq] - lam * sm[tq:], g_ref[...]) * (1.0 - LAMBDA_INIT)
            o_ref[0, :, h * LANES:(h + 1) * LANES] = o.astype(o_ref.dtype)


def _attn_sample(slopes, q, cache_kt, cache_v_rows, k_new, v_new, lq1, lk1, lq2, lk2, g):
    b, t, _ = q.shape
    past = cache_kt.shape[2]
    tk = min(CACHE_TILE, past)
    k_new = jnp.pad(k_new, ((0, 0), (0, -t % LANES), (0, 0)))
    v_new = jnp.pad(v_new, ((0, 0), (0, -t % LANES), (0, 0)))
    small = lambda a: pl.BlockSpec(a.shape, lambda bi, j: (0, 0))
    per_stream = lambda a: pl.BlockSpec((1,) + a.shape[1:], lambda bi, j: (bi, 0, 0))
    return pl.pallas_call(
        functools.partial(_attn_sample_kernel, past=past, tk=tk),
        grid=(b, past // tk),
        in_specs=[pl.BlockSpec(memory_space=pltpu.SMEM),
                  per_stream(q),
                  pl.BlockSpec((1, D_ATT, tk), lambda bi, j: (bi, 0, j)),
                  pl.BlockSpec((1, tk * N_HEADS, D_HEAD_V), lambda bi, j: (bi, j, 0)),
                  per_stream(k_new), per_stream(v_new),
                  small(lq1), small(lk1), small(lq2), small(lk2), small(g)],
        out_specs=pl.BlockSpec((1, t, D_ATT), lambda bi, j: (bi, 0, 0)),
        out_shape=jax.ShapeDtypeStruct((b, t, D_ATT), BF16),
        scratch_shapes=[pltpu.VMEM((N_HEADS, 2 * t, LANES), F32)] * 3,
        compiler_params=pltpu.CompilerParams(dimension_semantics=("arbitrary",) * 2, vmem_limit_bytes=VMEM_LIMIT),
        name="attn_sample",
    )(slopes, q, cache_kt, cache_v_rows, k_new, v_new, lq1, lk1, lq2, lk2, g)


def _lru_kernel(x_ref, gate_ref, h0_ref, cbuf_ref, cw_ref, cb_ref, wr_ref, br_ref, wi_ref, bi_ref, lam_ref,
                out_ref, hlast_ref, xbuf_ref, h_ref, a_ref, b_ref, p_ref, hs_ref, *, tt, n_seg):
    ti = pl.program_id(1)
    pad = xbuf_ref.shape[0] - tt
    seg = tt // n_seg

    @pl.when(ti == 0)
    def _():
        xbuf_ref[0:pad, :] = cbuf_ref[0]
        h_ref[...] = h0_ref[0]

    @pl.when(ti > 0)
    def _():
        xbuf_ref[0:pad, :] = xbuf_ref[tt:tt + pad, :]

    xbuf_ref[pad:pad + tt, :] = x_ref[0]
    xc = cb_ref[...] + xbuf_ref[pad:pad + tt, :] * cw_ref[CONV_W - 1:CONV_W, :]
    for j in range(CONV_W - 1):
        back = CONV_W - 1 - j
        xc = xc + xbuf_ref[pad - back:pad - back + tt, :] * cw_ref[j:j + 1, :]

    xb = xc.astype(BF16)
    r = jax.nn.sigmoid(_dot(xb, wr_ref[...]) + br_ref[...])
    i = jax.nn.sigmoid(_dot(xb, wi_ref[...]) + bi_ref[...])
    neg_lam = -lam_ref[...]
    softplus = jnp.maximum(neg_lam, 0.0) + jnp.log1p(jnp.exp(-jnp.abs(neg_lam)))
    log_a = -LRU_C * r * softplus
    a = jnp.exp(log_a)
    b = jnp.sqrt(-jnp.tanh(log_a) * (1.0 + a * a)) * (i * xc)
    n_blk = D_LRU // LANES
    pitch = a_ref.shape[1] // n_seg
    for s in range(n_seg):
        for l in range(n_blk):
            a_ref[l, s * pitch:s * pitch + seg, :] = a[s * seg:(s + 1) * seg, l * LANES:(l + 1) * LANES]
            b_ref[l, s * pitch:s * pitch + seg, :] = b[s * seg:(s + 1) * seg, l * LANES:(l + 1) * LANES]

    h = [jnp.zeros((n_seg, LANES), F32)] * n_blk
    p = [jnp.ones((n_seg, LANES), F32)] * n_blk
    for j in range(seg):
        rows = pl.ds(j, n_seg, stride=pitch)
        for l in range(n_blk):
            a_j = a_ref[l, rows, :]
            h[l] = a_j * h[l] + b_ref[l, rows, :]
            p[l] = a_j * p[l]
            hs_ref[l, rows, :] = h[l]
            p_ref[l, rows, :] = p[l]

    h_end = jnp.concatenate(h, axis=1)
    p_end = jnp.concatenate(p, axis=1)
    carry = h_ref[...]
    for s in range(n_seg):
        rows = slice(s * pitch, s * pitch + seg)
        p_rows = jnp.concatenate([p_ref[l, rows, :] for l in range(n_blk)], axis=1)
        hs_rows = jnp.concatenate([hs_ref[l, rows, :] for l in range(n_blk)], axis=1)
        hs = p_rows * carry + hs_rows
        gate = gate_ref[0, s * seg:(s + 1) * seg, :]
        gelu = 0.5 * gate * (1.0 + jnp.tanh(math.sqrt(2.0 / math.pi) * (gate + 0.044715 * (gate * gate * gate))))
        out_ref[0, s * seg:(s + 1) * seg, :] = (hs * gelu).astype(out_ref.dtype)
        carry = p_end[s:s + 1, :] * carry + h_end[s:s + 1, :]
    h_ref[...] = carry
    hlast_ref[0] = carry


def _lru(x, gate, h0, cbuf, cw, cb, wr, br, wi, bi, lam):
    b, t, _ = x.shape
    tt = min(LRU_TILE, t)
    pad = cbuf.shape[1]
    small = lambda a: pl.BlockSpec(a.shape, lambda bi_, ti: (0, 0))
    tile = pl.BlockSpec((1, tt, D_LRU), lambda bi_, ti: (bi_, ti, 0))
    per_stream = lambda a: pl.BlockSpec((1,) + a.shape[1:], lambda bi_, ti: (bi_, 0, 0))
    return pl.pallas_call(
        functools.partial(_lru_kernel, tt=tt, n_seg=min(SUBLANES, tt // SUBLANES)),
        grid=(b, t // tt),
        in_specs=[tile, tile, per_stream(h0), per_stream(cbuf), small(cw), small(cb),
                  small(wr), small(br), small(wi), small(bi), small(lam)],
        out_specs=(tile, pl.BlockSpec((1, 1, D_LRU), lambda bi_, ti: (bi_, 0, 0))),
        out_shape=(jax.ShapeDtypeStruct((b, t, D_LRU), BF16), jax.ShapeDtypeStruct((b, 1, D_LRU), F32)),
        scratch_shapes=[pltpu.VMEM((tt + pad, D_LRU), F32), pltpu.VMEM((1, D_LRU), F32)]
        + [pltpu.VMEM((D_LRU // LANES, tt + 4 * min(SUBLANES, tt // SUBLANES), LANES), F32)] * 4,
        compiler_params=pltpu.CompilerParams(dimension_semantics=("arbitrary",) * 2, vmem_limit_bytes=VMEM_LIMIT),
        name="lru",
    )(x, gate, h0, cbuf, cw, cb, wr, br, wi, bi, lam)


def _block_diag_dense(w):
    n, c, _ = w.shape
    eye = jnp.eye(n, dtype=w.dtype)
    return (eye[:, None, :, None] * w[:, :, None, :]).reshape(n * c, n * c)


def _chunk_cols(w):
    return w.reshape(w.shape[0], N_FF_CHUNKS, FF_CHUNK).transpose(1, 0, 2)


def kernel(x_prompt, x_sample, cache_k, cache_v, state_lru_h, state_conv, w_in, w_out, lambda_q1, lambda_k1,
           lambda_q2, lambda_k2, subln_g, conv_w, conv_b, w_rgate, b_rgate, w_igate, b_igate, lru_lambda,
           ffn1_w_gate, ffn1_w_up, ffn1_w_down, ffn2_w_gate, ffn2_w_up, ffn2_w_down,
           g_ffn1_pre, g_ffn1_post, g_mix_pre, g_mix_post, g_ffn2_pre, g_ffn2_post):
    bp, tp, _ = x_prompt.shape
    bs, ts, _ = x_sample.shape
    past = cache_k.shape[2]

    wg1, wu1 = _chunk_cols(ffn1_w_gate[0].astype(BF16)), _chunk_cols(ffn1_w_up[0].astype(BF16))
    wd1 = ffn1_w_down[0].astype(BF16).reshape(N_FF_CHUNKS, FF_CHUNK, D_MODEL)
    wg2, wu2 = _chunk_cols(ffn2_w_gate[0].astype(BF16)), _chunk_cols(ffn2_w_up[0].astype(BF16))
    wd2 = ffn2_w_down[0].astype(BF16).reshape(N_FF_CHUNKS, FF_CHUNK, D_MODEL)
    win = w_in[0].astype(BF16)
    wq, wkt, wv, wl = win[:, :D_ATT], win[:, D_ATT:2 * D_ATT].T, win[:, 2 * D_ATT:3 * D_ATT], win[:, 3 * D_ATT:]
    woa, wob = w_out[0, :D_ATT].astype(BF16), w_out[0, D_ATT:].astype(BF16)
    wr = _block_diag_dense(w_rgate[0]).astype(BF16)
    wi = _block_diag_dense(w_igate[0]).astype(BF16)
    slopes = jnp.asarray(2.0 ** (-8.0 * np.arange(1, N_HEADS + 1) / N_HEADS), dtype=F32)
    lam_vecs = (lambda_q1, lambda_k1, lambda_q2, lambda_k2)
    ffn1 = (g_ffn1_pre, g_ffn1_post, g_mix_pre, wg1, wu1, wd1, wq, wkt, wv, wl)
    lru_params = (conv_w[0], conv_b, wr, b_rgate, wi, b_igate, lru_lambda)
    tail_pad = 8 - (CONV_W - 1)

    def mix_and_ffn2(b, t, x1, att, lx, lg, h0, conv_hist):
        lx3 = lx.reshape(b, t, D_LRU)
        cbuf = jnp.pad(conv_hist, ((0, 0), (tail_pad, 0), (0, 0)))
        lru_out, h_last = _lru(lx3, lg.reshape(b, t, D_LRU), h0.reshape(b, 1, D_LRU), cbuf, *lru_params)
        y = _outproj_ffn(att.reshape(b * t, D_ATT), lru_out.reshape(b * t, D_LRU), x1, woa, wob,
                         g_mix_post, g_ffn2_pre, g_ffn2_post, wg2, wu2, wd2)
        conv_new = jnp.concatenate([conv_hist, lx3], axis=1)[:, -(CONV_W - 1):]
        return y.reshape(b, t, D_MODEL), h_last.reshape(1, b, D_LRU), conv_new[None]

    x1, qb, kt, ktb, v_rows, vb, lx, lg = _ffn_inproj(x_prompt.reshape(bp * tp, D_MODEL), *ffn1, seq_len=tp)
    att = _attn_prompt(slopes, qb.reshape(bp, tp, D_ATT), ktb, vb.reshape(bp, tp, D_ATT), *lam_vecs, subln_g)
    yp, hp, cp = mix_and_ffn2(bp, tp, x1, att, lx, lg, jnp.zeros((bp, D_LRU), F32),
                              jnp.zeros((bp, CONV_W - 1, D_LRU), F32))
    kp = kt.reshape(bp, N_HEADS, 2, D_HEAD_QK, tp).transpose(0, 4, 1, 2, 3)[None]
    vp = v_rows.reshape(1, bp, tp, N_HEADS, D_HEAD_V)

    x1, qb, kf, kb, vf, vb, lx, lg = _ffn_inproj(x_sample.reshape(bs * ts, D_MODEL), *ffn1)
    cache_kt = cache_k[0].transpose(0, 2, 3, 4, 1).reshape(bs, D_ATT, past)
    cache_v_rows = cache_v[0].reshape(bs, past * N_HEADS, D_HEAD_V)
    att = _attn_sample(slopes, qb.reshape(bs, ts, D_ATT), cache_kt, cache_v_rows, kb.reshape(bs, ts, D_ATT),
                       vb.reshape(bs, ts, D_ATT), *lam_vecs, subln_g)
    ys, hs, cs = mix_and_ffn2(bs, ts, x1, att, lx, lg, state_lru_h[0], state_conv[0])
    ks = kf.reshape(1, bs, ts, N_HEADS, 2, D_HEAD_QK)
    vs = vf.reshape(1, bs, ts, N_HEADS, D_HEAD_V)
    return (yp, ys, kp, vp, hp, cp, ks, vs, hs, cs)
```

```python
import functools
import math

import jax
import jax.numpy as jnp
import numpy as np
from jax import lax
from jax.experimental import pallas as pl
from jax.experimental.pallas import tpu as pltpu

F32 = jnp.float32
BF16 = jnp.bfloat16

D_MODEL = 1024
D_ATT = 512
D_LRU = 512
N_HEADS = 4
D_HEAD_V = 128
D_HEAD_QK = 64
CONV_W = 4
LRU_C = 8.0
D_FF = 2816
CHUNK = 64
RMS_EPS = 1e-6
NEG_INF = -1e30
LAMBDA_INIT = 0.8 - 0.6 * math.exp(-0.3 * 0)
LOG2E = math.log2(math.e)

LANES = 128
SUBLANES = 8
FF_CHUNK = 256
N_FF_CHUNKS = D_FF // FF_CHUNK
ROW_TILE = 512
ATTN_HEADS_PER_STEP = 2
LRU_TILE = 512
CACHE_TILE = 4096
VMEM_LIMIT = 56 * 1024 * 1024


def _dot(a, b):
    return jnp.dot(a, b, preferred_element_type=F32)


def _dot_nt(a, b):
    return lax.dot_general(a, b, (((1,), (1,)), ((), ())), preferred_element_type=F32)


def _rms(x, g):
    return x * lax.rsqrt(jnp.mean(x * x, axis=-1, keepdims=True) + RMS_EPS) * g


def _swiglu_ffn(x, g_pre, g_post, wg_ref, wu_ref, wd_ref):
    xn = _rms(x, g_pre).astype(BF16)
    acc = jnp.zeros(x.shape, F32)
    for j in range(N_FF_CHUNKS):
        cols = slice(j * FF_CHUNK, (j + 1) * FF_CHUNK)
        g = _dot(xn, wg_ref[:, cols])
        u = _dot(xn, wu_ref[:, cols])
        h = (g * jax.nn.sigmoid(g) * u).astype(BF16)
        acc = acc + _dot(h, wd_ref[cols, :])
    return x + 0.5 * _rms(acc, g_post)


def _resident(shape):
    nd = len(shape)
    return pl.BlockSpec(shape, lambda i: (0,) * nd, pipeline_mode=pl.Buffered(1))


def _ffn_inproj_kernel(x_ref, g1a_ref, g1b_ref, gma_ref, wg_ref, wu_ref, wd_ref, wq_ref, wkt_ref, wv_ref, wl_ref,
                       x1_ref, qb_ref, kf_ref, kb_ref, vf_ref, vb_ref, lx_ref, lg_ref, *, seq_layout):
    x1 = _swiglu_ffn(x_ref[...], g1a_ref[...], g1b_ref[...], wg_ref, wu_ref, wd_ref)
    x1_ref[...] = x1
    xm = _rms(x1, gma_ref[...]).astype(BF16)
    tm = xm.shape[0]
    qb_ref[...] = (_dot(xm, wq_ref[...]) * (LOG2E / math.sqrt(D_HEAD_QK))).astype(BF16)
    v = _dot(xm, wv_ref[...])
    vb_ref[...] = v.astype(BF16)
    if seq_layout:
        kt = _dot_nt(wkt_ref[...], xm)
        kf_ref[0] = kt
        kb_ref[0, 0] = kt.astype(BF16)
        for h in range(N_HEADS):
            vf_ref[pl.ds(h, tm, stride=N_HEADS), :] = v[:, h * D_HEAD_V:(h + 1) * D_HEAD_V]
    else:
        k = _dot_nt(xm, wkt_ref[...])
        kf_ref[...] = k
        kb_ref[...] = k.astype(BF16)
        vf_ref[...] = v
    lru = _dot(xm, wl_ref[...])
    lx_ref[...] = lru[:, :D_LRU]
    lg_ref[...] = lru[:, D_LRU:]


def _ffn_inproj(x, g1a, g1b, gma, wg, wu, wd, wq, wkt, wv, wl, seq_len=None):
    m = x.shape[0]
    tm = min(ROW_TILE, m)
    rows = lambda d: pl.BlockSpec((tm, d), lambda i: (i, 0))
    row_out = lambda d, dt: jax.ShapeDtypeStruct((m, d), dt)
    seq_layout = seq_len is not None
    if seq_layout:
        nt = seq_len // tm
        b = m // seq_len
        k_shapes = (jax.ShapeDtypeStruct((b, D_ATT, seq_len), F32), jax.ShapeDtypeStruct((b, nt, D_ATT, tm), BF16))
        k_specs = (pl.BlockSpec((1, D_ATT, tm), lambda i: (i // nt, 0, i % nt)),
                   pl.BlockSpec((1, 1, D_ATT, tm), lambda i: (i // nt, i % nt, 0, 0)))
        vf_shape = jax.ShapeDtypeStruct((m * N_HEADS, D_HEAD_V), F32)
        vf_spec = pl.BlockSpec((tm * N_HEADS, D_HEAD_V), lambda i: (i, 0))
    else:
        k_shapes = (row_out(D_ATT, F32), row_out(D_ATT, BF16))
        k_specs = (rows(D_ATT), rows(D_ATT))
        vf_shape = row_out(D_ATT, F32)
        vf_spec = rows(D_ATT)
    out_shape = (row_out(D_MODEL, F32), row_out(D_ATT, BF16), *k_shapes, vf_shape, row_out(D_ATT, BF16),
                 row_out(D_LRU, F32), row_out(D_LRU, F32))
    out_specs = (rows(D_MODEL), rows(D_ATT), *k_specs, vf_spec, rows(D_ATT), rows(D_LRU), rows(D_LRU))
    weights = (g1a, g1b, gma, wg, wu, wd, wq, wkt, wv, wl)
    return pl.pallas_call(
        functools.partial(_ffn_inproj_kernel, seq_layout=seq_layout),
        grid=(m // tm,),
        in_specs=[rows(D_MODEL)] + [_resident(w.shape) for w in weights],
        out_specs=out_specs,
        out_shape=out_shape,
        compiler_params=pltpu.CompilerParams(dimension_semantics=("arbitrary",), vmem_limit_bytes=VMEM_LIMIT),
        name="ffn_inproj",
    )(x, *weights)


def _outproj_ffn_kernel(att_ref, lru_ref, x1_ref, woa_ref, wob_ref, gmb_ref, g2a_ref, g2b_ref,
                        wg_ref, wu_ref, wd_ref, y_ref):
    mix = _dot(att_ref[...], woa_ref[...]) + _dot(lru_ref[...], wob_ref[...])
    x2 = x1_ref[...] + _rms(mix, gmb_ref[...])
    y_ref[...] = _swiglu_ffn(x2, g2a_ref[...], g2b_ref[...], wg_ref, wu_ref, wd_ref)


def _outproj_ffn(att, lru, x1, woa, wob, gmb, g2a, g2b, wg, wu, wd):
    m = x1.shape[0]
    tm = min(ROW_TILE, m)
    rows = lambda d: pl.BlockSpec((tm, d), lambda i: (i, 0))
    weights = (woa, wob, gmb, g2a, g2b, wg, wu, wd)
    return pl.pallas_call(
        _outproj_ffn_kernel,
        grid=(m // tm,),
        in_specs=[rows(D_ATT), rows(D_LRU), rows(D_MODEL)] + [_resident(w.shape) for w in weights],
        out_specs=rows(D_MODEL),
        out_shape=jax.ShapeDtypeStruct((m, D_MODEL), F32),
        compiler_params=pltpu.CompilerParams(dimension_semantics=("arbitrary",), vmem_limit_bytes=VMEM_LIMIT),
        name="outproj_ffn",
    )(att, lru, x1, *weights)


def _split_maps(q):
    lane = lax.broadcasted_iota(jnp.int32, q.shape, 1)
    zero = jnp.zeros_like(q)
    return jnp.where(lane < D_HEAD_QK, q, zero), jnp.where(lane >= D_HEAD_QK, q, zero)


def _init_softmax_state(m_ref, l_ref, acc_ref):
    m_ref[...] = jnp.full(m_ref.shape, NEG_INF, F32)
    l_ref[...] = jnp.zeros(l_ref.shape, F32)
    acc_ref[...] = jnp.zeros(acc_ref.shape, F32)


def _bf16_split3(x):
    hi = x.astype(BF16)
    r1 = x - hi.astype(F32)
    lo = r1.astype(BF16)
    lo2 = (r1 - lo.astype(F32)).astype(BF16)
    return hi, lo, lo2


def _attn_prompt_kernel(slopes_ref, q_ref, kt_ref, v_ref, lq1, lk1, lq2, lk2, g_ref, o_ref,
                        kta_ref, va_ref, mask_ref, m_ref, acc_ref, pa_ref, pb_ref, *, tq, nh):
    hg = pl.program_id(1)
    qi = pl.program_id(2)
    n_kt = kt_ref.shape[1]
    half = D_HEAD_QK
    slopes = [slopes_ref[hg * nh + g] * LOG2E for g in range(nh)]
    n_slots = 2 * nh

    @pl.when(qi == 0)
    def _():
        row = lax.broadcasted_iota(jnp.int32, (LANES, tq), 0)
        col = lax.broadcasted_iota(jnp.int32, (1, tq), 1)
        qrow = lax.broadcasted_iota(jnp.int32, (tq, tq), 0)
        kcol = lax.broadcasted_iota(jnp.int32, (tq, tq), 1)
        zero = jnp.zeros((LANES, tq), F32)
        for g in range(nh):
            terms = _bf16_split3(slopes[g] * col.astype(F32))

            def bias_rows(first):
                blk = zero
                for i, t in enumerate(terms):
                    blk = jnp.where(row == first + i, jnp.broadcast_to(t.astype(F32), (LANES, tq)), blk)
                return blk

            bias0, bias1 = bias_rows(half), bias_rows(0)

            def fill(n, carry):
                kt = kt_ref[0, n, g * LANES:(g + 1) * LANES, :].astype(F32)
                kta_ref[2 * g, n] = jnp.where(row < half, kt, bias0).astype(BF16)
                kta_ref[2 * g + 1, n] = jnp.where(row >= half, kt, bias1).astype(BF16)
                r0 = pl.multiple_of(n * tq, tq)
                va_ref[g, pl.ds(r0, tq), 0:D_HEAD_V] = v_ref[0, pl.ds(r0, tq), g * LANES:(g + 1) * LANES]
                va_ref[g, pl.ds(r0, tq), D_HEAD_V:2 * D_HEAD_V] = jnp.ones((tq, D_HEAD_V), BF16)
                return carry

            lax.fori_loop(0, n_kt, fill, 0)
            mask_ref[g] = jnp.where((qrow // CHUNK) >= (kcol // CHUNK),
                                    slopes[g] * jnp.minimum(2 * (qrow - kcol), 0).astype(F32), NEG_INF)
        acc_ref[...] = jnp.zeros(acc_ref.shape, F32)
        pb_ref[...] = jnp.zeros(pb_ref.shape, BF16)

    lane = lax.broadcasted_iota(jnp.int32, (tq, LANES), 1)
    ones0 = ((lane >= half) & (lane < half + 3)).astype(F32).astype(BF16)
    ones1 = (lane < 3).astype(F32).astype(BF16)
    qa = []
    for g in range(nh):
        q = q_ref[0, :, g * LANES:(g + 1) * LANES]
        qa += [jnp.where(lane < half, q, ones0), jnp.where(lane >= half, q, ones1)]
    m_ref[...] = jnp.full(m_ref.shape, NEG_INF, F32)

    def values(g, j):
        return va_ref[g, pl.ds(pl.multiple_of(j * tq, tq), tq), :]

    def softmax_tile(s, m_prev, tile_bias):
        m_new = jnp.maximum(m_prev, jnp.max(s, axis=1, keepdims=True) + tile_bias)
        shift = m_new - tile_bias
        p = jnp.concatenate([jnp.exp2((s[:, b * LANES:(b + 1) * LANES] - shift).astype(BF16))
                             for b in range(s.shape[1] // LANES)], axis=1)
        return m_new, p

    def update(j, p_in_ref, p_out_ref, diagonal=False):
        logits = [_dot(qa[k], kta_ref[k, j]) for k in range(n_slots)]
        if diagonal:
            logits = [logits[k] + mask_ref[k // 2] for k in range(n_slots)]
        j_prev = jnp.maximum(j - 1, 0)
        pv = [_dot(p_in_ref[k], values(k // 2, j_prev)) for k in range(n_slots)]
        m_prev = [m_ref[k] for k in range(n_slots)]
        acc_prev = [acc_ref[k] for k in range(n_slots)]
        m_next, acc_next, p_next = [], [], []
        for k in range(n_slots):
            tile_bias = slopes[k // 2] * ((j - qi) * tq).astype(F32)
            m_new, p = softmax_tile(logits[k], m_prev[k], tile_bias)
            alpha = jnp.exp2(m_prev[k] - m_new)
            acc = jnp.concatenate([alpha, alpha], axis=1) * (acc_prev[k] + pv[k])
            if diagonal:
                acc = acc + _dot(p, values(k // 2, j))
            acc_next.append(acc)
            m_next.append(m_new)
            p_next.append(p)
        for k in range(n_slots):
            acc_ref[k] = acc_next[k]
            m_ref[k] = m_next[k]
            if not diagonal:
                p_out_ref[k] = p_next[k]

    odd = qi % 2

    @pl.when(odd == 1)
    def _():
        for k in range(n_slots):
            tile_bias = slopes[k // 2] * (-qi * tq).astype(F32)
            m_new, p = softmax_tile(_dot(qa[k], kta_ref[k, 0]), m_ref[k], tile_bias)
            m_ref[k] = m_new
            pb_ref[k] = p
        acc_ref[...] = jnp.zeros(acc_ref.shape, F32)

    def pair(i, carry):
        j = odd + 2 * i
        update(j, pb_ref, pa_ref)

        @pl.when(j + 1 < qi)
        def _():
            update(j + 1, pa_ref, pb_ref)

        return carry

    lax.fori_loop(0, qi // 2, pair, 0)
    update(qi, pb_ref, None, diagonal=True)

    lam = (jnp.exp(jnp.sum(lq1[...] * lk1[...], axis=1, keepdims=True))
           - jnp.exp(jnp.sum(lq2[...] * lk2[...], axis=1, keepdims=True)) + LAMBDA_INIT)
    for g in range(nh):
        a0, a1 = acc_ref[2 * g], acc_ref[2 * g + 1]
        o = a0[:, :D_HEAD_V] / a0[:, D_HEAD_V:] - lam * (a1[:, :D_HEAD_V] / a1[:, D_HEAD_V:])
        o_ref[0, :, g * LANES:(g + 1) * LANES] = (_rms(o, g_ref[...]) * (1.0 - LAMBDA_INIT)).astype(o_ref.dtype)


def _attn_prompt(slopes, q, kt, v, lq1, lk1, lq2, lk2, g):
    b, t, _ = q.shape
    n_kt, tq = kt.shape[1], kt.shape[3]
    nh = ATTN_HEADS_PER_STEP
    w = nh * LANES
    small = lambda a: pl.BlockSpec(a.shape, lambda bi, hi, qi: (0, 0))
    return pl.pallas_call(
        functools.partial(_attn_prompt_kernel, tq=tq, nh=nh),
        grid=(b, N_HEADS // nh, t // tq),
        in_specs=[pl.BlockSpec(memory_space=pltpu.SMEM),
                  pl.BlockSpec((1, tq, w), lambda bi, hi, qi: (bi, qi, hi)),
                  pl.BlockSpec((1, n_kt, w, tq), lambda bi, hi, qi: (bi, 0, hi, 0), pipeline_mode=pl.Buffered(1)),
                  pl.BlockSpec((1, t, w), lambda bi, hi, qi: (bi, 0, hi), pipeline_mode=pl.Buffered(1)),
                  small(lq1), small(lk1), small(lq2), small(lk2), small(g)],
        out_specs=pl.BlockSpec((1, tq, w), lambda bi, hi, qi: (bi, qi, hi)),
        out_shape=jax.ShapeDtypeStruct((b, t, D_ATT), BF16),
        scratch_shapes=[pltpu.VMEM((2 * nh, n_kt, LANES, tq), BF16), pltpu.VMEM((nh, t, 2 * D_HEAD_V), BF16),
                        pltpu.VMEM((nh, tq, tq), F32), pltpu.VMEM((2 * nh, tq, LANES), F32),
                        pltpu.VMEM((2 * nh, tq, 2 * D_HEAD_V), F32)] + [pltpu.VMEM((2 * nh, tq, tq), BF16)] * 2,
        compiler_params=pltpu.CompilerParams(dimension_semantics=("arbitrary",) * 3, vmem_limit_bytes=VMEM_LIMIT),
        name="attn_prompt",
    )(slopes, q, kt, v, lq1, lk1, lq2, lk2, g)


def _attn_sample_kernel(slopes_ref, q_ref, ckt_ref, cv_ref, kn_ref, vn_ref, lq1, lk1, lq2, lk2, g_ref, o_ref,
                        m_ref, l_ref, acc_ref, *, past, tk):
    j = pl.program_id(1)
    tq = q_ref.shape[1]

    @pl.when(j == 0)
    def _():
        _init_softmax_state(m_ref, l_ref, acc_ref)

    def attend(n, k_start, n_valid, scores, values):
        q_pos = past + lax.broadcasted_iota(jnp.int32, (tq, n), 0)
        k_pos = k_start + lax.broadcasted_iota(jnp.int32, (tq, n), 1)
        dist = jnp.abs(q_pos - k_pos).astype(F32)
        dist = jnp.concatenate([dist, dist], axis=0)
        visible = ((q_pos // CHUNK) >= (k_pos // CHUNK)) & (k_pos < k_start + n_valid)
        visible = jnp.concatenate([visible, visible], axis=0)
        state = [(m_ref[h], l_ref[h], acc_ref[h]) for h in range(N_HEADS)]
        for h in range(N_HEADS):
            m_prev, l_prev, acc_prev = state[h]
            q2 = jnp.concatenate(_split_maps(q_ref[0, :, h * LANES:(h + 1) * LANES]), axis=0)
            s = jnp.where(visible, scores(q2, h) - (slopes_ref[h] * LOG2E) * dist, NEG_INF)
            m_new = jnp.maximum(m_prev, jnp.max(s, axis=1, keepdims=True))
            alpha = jnp.exp2(m_prev - m_new)
            ps = [jnp.exp2(s[:, b * LANES:(b + 1) * LANES] - m_new) for b in range(n // LANES)]
            p = jnp.concatenate(ps, axis=1).astype(BF16)
            state[h] = (m_new, alpha * l_prev + functools.reduce(lambda x, y: x + y, ps),
                        alpha * acc_prev + _dot(p, values(h)))
        for h in range(N_HEADS):
            m_ref[h], l_ref[h], acc_ref[h] = state[h]

    attend(tk, j * tk, tk,
           lambda q2, h: _dot(q2, ckt_ref[0, h * LANES:(h + 1) * LANES, :].astype(BF16)),
           lambda h: cv_ref[0, pl.ds(h, tk, stride=N_HEADS), :].astype(BF16))

    @pl.when(j == pl.num_programs(1) - 1)
    def _():
        attend(kn_ref.shape[1], past, tq,
               lambda q2, h: _dot_nt(q2, kn_ref[0, :, h * LANES:(h + 1) * LANES]),
               lambda h: vn_ref[0, :, h * LANES:(h + 1) * LANES])
        lam = (jnp.exp(jnp.sum(lq1[...] * lk1[...], axis=1, keepdims=True))
               - jnp.exp(jnp.sum(lq2[...] * lk2[...], axis=1, keepdims=True)) + LAMBDA_INIT)
        for h in range(N_HEADS):
            sm = acc_ref[h] / jnp.sum(l_ref[h], axis=1, keepdims=True)
            o = _rms(sm[:tq] - lam * sm[tq:], g_ref[...]) * (1.0 - LAMBDA_INIT)
            o_ref[0, :, h * LANES:(h + 1) * LANES] = o.astype(o_ref.dtype)


def _attn_sample(slopes, q, cache_kt, cache_v_rows, k_new, v_new, lq1, lk1, lq2, lk2, g):
    b, t, _ = q.shape
    past = cache_kt.shape[2]
    tk = min(CACHE_TILE, past)
    k_new = jnp.pad(k_new, ((0, 0), (0, -t % LANES), (0, 0)))
    v_new = jnp.pad(v_new, ((0, 0), (0, -t % LANES), (0, 0)))
    small = lambda a: pl.BlockSpec(a.shape, lambda bi, j: (0, 0))
    per_stream = lambda a: pl.BlockSpec((1,) + a.shape[1:], lambda bi, j: (bi, 0, 0))
    return pl.pallas_call(
        functools.partial(_attn_sample_kernel, past=past, tk=tk),
        grid=(b, past // tk),
        in_specs=[pl.BlockSpec(memory_space=pltpu.SMEM),
                  per_stream(q),
                  pl.BlockSpec((1, D_ATT, tk), lambda bi, j: (bi, 0, j)),
                  pl.BlockSpec((1, tk * N_HEADS, D_HEAD_V), lambda bi, j: (bi, j, 0)),
                  per_stream(k_new), per_stream(v_new),
                  small(lq1), small(lk1), small(lq2), small(lk2), small(g)],
        out_specs=pl.BlockSpec((1, t, D_ATT), lambda bi, j: (bi, 0, 0)),
        out_shape=jax.ShapeDtypeStruct((b, t, D_ATT), BF16),
        scratch_shapes=[pltpu.VMEM((N_HEADS, 2 * t, LANES), F32)] * 3,
        compiler_params=pltpu.CompilerParams(dimension_semantics=("arbitrary",) * 2, vmem_limit_bytes=VMEM_LIMIT),
        name="attn_sample",
    )(slopes, q, cache_kt, cache_v_rows, k_new, v_new, lq1, lk1, lq2, lk2, g)


def _lru_kernel(x_ref, gate_ref, h0_ref, cbuf_ref, cw_ref, cb_ref, wr_ref, br_ref, wi_ref, bi_ref, lam_ref,
                out_ref, hlast_ref, xbuf_ref, h_ref, a_ref, b_ref, p_ref, hs_ref, *, tt, n_seg):
    ti = pl.program_id(1)
    pad = xbuf_ref.shape[0] - tt
    seg = tt // n_seg

    @pl.when(ti == 0)
    def _():
        xbuf_ref[0:pad, :] = cbuf_ref[0]
        h_ref[...] = h0_ref[0]

    @pl.when(ti > 0)
    def _():
        xbuf_ref[0:pad, :] = xbuf_ref[tt:tt + pad, :]

    xbuf_ref[pad:pad + tt, :] = x_ref[0]
    xc = cb_ref[...] + xbuf_ref[pad:pad + tt, :] * cw_ref[CONV_W - 1:CONV_W, :]
    for j in range(CONV_W - 1):
        back = CONV_W - 1 - j
        xc = xc + xbuf_ref[pad - back:pad - back + tt, :] * cw_ref[j:j + 1, :]

    xb = xc.astype(BF16)
    r = jax.nn.sigmoid(_dot(xb, wr_ref[...]) + br_ref[...])
    i = jax.nn.sigmoid(_dot(xb, wi_ref[...]) + bi_ref[...])
    neg_lam = -lam_ref[...]
    softplus = jnp.maximum(neg_lam, 0.0) + jnp.log1p(jnp.exp(-jnp.abs(neg_lam)))
    log_a = -LRU_C * r * softplus
    a = jnp.exp(log_a)
    b = jnp.sqrt(-jnp.tanh(log_a) * (1.0 + a * a)) * (i * xc)
    n_blk = D_LRU // LANES
    pitch = a_ref.shape[1] // n_seg
    for s in range(n_seg):
        for l in range(n_blk):
            a_ref[l, s * pitch:s * pitch + seg, :] = a[s * seg:(s + 1) * seg, l * LANES:(l + 1) * LANES]
            b_ref[l, s * pitch:s * pitch + seg, :] = b[s * seg:(s + 1) * seg, l * LANES:(l + 1) * LANES]

    h = [jnp.zeros((n_seg, LANES), F32)] * n_blk
    p = [jnp.ones((n_seg, LANES), F32)] * n_blk
    for j in range(seg):
        rows = pl.ds(j, n_seg, stride=pitch)
        for l in range(n_blk):
            a_j = a_ref[l, rows, :]
            h[l] = a_j * h[l] + b_ref[l, rows, :]
            p[l] = a_j * p[l]
            hs_ref[l, rows, :] = h[l]
            p_ref[l, rows, :] = p[l]

    h_end = jnp.concatenate(h, axis=1)
    p_end = jnp.concatenate(p, axis=1)
    carry = h_ref[...]
    for s in range(n_seg):
        rows = slice(s * pitch, s * pitch + seg)
        p_rows = jnp.concatenate([p_ref[l, rows, :] for l in range(n_blk)], axis=1)
        hs_rows = jnp.concatenate([hs_ref[l, rows, :] for l in range(n_blk)], axis=1)
        hs = p_rows * carry + hs_rows
        gate = gate_ref[0, s * seg:(s + 1) * seg, :]
        gelu = 0.5 * gate * (1.0 + jnp.tanh(math.sqrt(2.0 / math.pi) * (gate + 0.044715 * (gate * gate * gate))))
        out_ref[0, s * seg:(s + 1) * seg, :] = (hs * gelu).astype(out_ref.dtype)
        carry = p_end[s:s + 1, :] * carry + h_end[s:s + 1, :]
    h_ref[...] = carry
    hlast_ref[0] = carry


def _lru(x, gate, h0, cbuf, cw, cb, wr, br, wi, bi, lam):
    b, t, _ = x.shape
    tt = min(LRU_TILE, t)
    pad = cbuf.shape[1]
    small = lambda a: pl.BlockSpec(a.shape, lambda bi_, ti: (0, 0))
    tile = pl.BlockSpec((1, tt, D_LRU), lambda bi_, ti: (bi_, ti, 0))
    per_stream = lambda a: pl.BlockSpec((1,) + a.shape[1:], lambda bi_, ti: (bi_, 0, 0))
    return pl.pallas_call(
        functools.partial(_lru_kernel, tt=tt, n_seg=min(SUBLANES, tt // SUBLANES)),
        grid=(b, t // tt),
        in_specs=[tile, tile, per_stream(h0), per_stream(cbuf), small(cw), small(cb),
                  small(wr), small(br), small(wi), small(bi), small(lam)],
        out_specs=(tile, pl.BlockSpec((1, 1, D_LRU), lambda bi_, ti: (bi_, 0, 0))),
        out_shape=(jax.ShapeDtypeStruct((b, t, D_LRU), BF16), jax.ShapeDtypeStruct((b, 1, D_LRU), F32)),
        scratch_shapes=[pltpu.VMEM((tt + pad, D_LRU), F32), pltpu.VMEM((1, D_LRU), F32)]
        + [pltpu.VMEM((D_LRU // LANES, tt + 4 * min(SUBLANES, tt // SUBLANES), LANES), F32)] * 4,
        compiler_params=pltpu.CompilerParams(dimension_semantics=("arbitrary",) * 2, vmem_limit_bytes=VMEM_LIMIT),
        name="lru",
    )(x, gate, h0, cbuf, cw, cb, wr, br, wi, bi, lam)


def _block_diag_dense(w):
    n, c, _ = w.shape
    eye = jnp.eye(n, dtype=w.dtype)
    return (eye[:, None, :, None] * w[:, :, None, :]).reshape(n * c, n * c)


def kernel(x_prompt, x_sample, cache_k, cache_v, state_lru_h, state_conv, w_in, w_out, lambda_q1, lambda_k1,
           lambda_q2, lambda_k2, subln_g, conv_w, conv_b, w_rgate, b_rgate, w_igate, b_igate, lru_lambda,
           ffn1_w_gate, ffn1_w_up, ffn1_w_down, ffn2_w_gate, ffn2_w_up, ffn2_w_down,
           g_ffn1_pre, g_ffn1_post, g_mix_pre, g_mix_post, g_ffn2_pre, g_ffn2_post):
    bp, tp, _ = x_prompt.shape
    bs, ts, _ = x_sample.shape
    past = cache_k.shape[2]

    wg1, wu1, wd1 = ffn1_w_gate[0].astype(BF16), ffn1_w_up[0].astype(BF16), ffn1_w_down[0].astype(BF16)
    wg2, wu2, wd2 = ffn2_w_gate[0].astype(BF16), ffn2_w_up[0].astype(BF16), ffn2_w_down[0].astype(BF16)
    win = w_in[0].astype(BF16)
    wq, wkt, wv, wl = win[:, :D_ATT], win[:, D_ATT:2 * D_ATT].T, win[:, 2 * D_ATT:3 * D_ATT], win[:, 3 * D_ATT:]
    woa, wob = w_out[0, :D_ATT].astype(BF16), w_out[0, D_ATT:].astype(BF16)
    wr = _block_diag_dense(w_rgate[0]).astype(BF16)
    wi = _block_diag_dense(w_igate[0]).astype(BF16)
    slopes = jnp.asarray(2.0 ** (-8.0 * np.arange(1, N_HEADS + 1) / N_HEADS), dtype=F32)
    lam_vecs = (lambda_q1, lambda_k1, lambda_q2, lambda_k2)
    ffn1 = (g_ffn1_pre, g_ffn1_post, g_mix_pre, wg1, wu1, wd1, wq, wkt, wv, wl)
    lru_params = (conv_w[0], conv_b, wr, b_rgate, wi, b_igate, lru_lambda)
    tail_pad = 8 - (CONV_W - 1)

    def mix_and_ffn2(b, t, x1, att, lx, lg, h0, conv_hist):
        lx3 = lx.reshape(b, t, D_LRU)
        cbuf = jnp.pad(conv_hist, ((0, 0), (tail_pad, 0), (0, 0)))
        lru_out, h_last = _lru(lx3, lg.reshape(b, t, D_LRU), h0.reshape(b, 1, D_LRU), cbuf, *lru_params)
        y = _outproj_ffn(att.reshape(b * t, D_ATT), lru_out.reshape(b * t, D_LRU), x1, woa, wob,
                         g_mix_post, g_ffn2_pre, g_ffn2_post, wg2, wu2, wd2)
        conv_new = jnp.concatenate([conv_hist, lx3], axis=1)[:, -(CONV_W - 1):]
        return y.reshape(b, t, D_MODEL), h_last.reshape(1, b, D_LRU), conv_new[None]

    x1, qb, kt, ktb, v_rows, vb, lx, lg = _ffn_inproj(x_prompt.reshape(bp * tp, D_MODEL), *ffn1, seq_len=tp)
    att = _attn_prompt(slopes, qb.reshape(bp, tp, D_ATT), ktb, vb.reshape(bp, tp, D_ATT), *lam_vecs, subln_g)
    yp, hp, cp = mix_and_ffn2(bp, tp, x1, att, lx, lg, jnp.zeros((bp, D_LRU), F32),
                              jnp.zeros((bp, CONV_W - 1, D_LRU), F32))
    kp = kt.reshape(bp, N_HEADS, 2, D_HEAD_QK, tp).transpose(0, 4, 1, 2, 3)[None]
    vp = v_rows.reshape(1, bp, tp, N_HEADS, D_HEAD_V)

    x1, qb, kf, kb, vf, vb, lx, lg = _ffn_inproj(x_sample.reshape(bs * ts, D_MODEL), *ffn1)
    cache_kt = cache_k[0].transpose(0, 2, 3, 4, 1).reshape(bs, D_ATT, past)
    cache_v_rows = cache_v[0].reshape(bs, past * N_HEADS, D_HEAD_V)
    att = _attn_sample(slopes, qb.reshape(bs, ts, D_ATT), cache_kt, cache_v_rows, kb.reshape(bs, ts, D_ATT),
                       vb.reshape(bs, ts, D_ATT), *lam_vecs, subln_g)
    ys, hs, cs = mix_and_ffn2(bs, ts, x1, att, lx, lg, state_lru_h[0], state_conv[0])
    ks = kf.reshape(1, bs, ts, N_HEADS, 2, D_HEAD_QK)
    vs = vf.reshape(1, bs, ts, N_HEADS, D_HEAD_V)
    return (yp, ys, kp, vp, hp, cp, ks, vs, hs, cs)
```

```python
import functools
import math

import jax
import jax.numpy as jnp
import numpy as np
from jax import lax
from jax.experimental import pallas as pl
from jax.experimental.pallas import tpu as pltpu

F32 = jnp.float32
BF16 = jnp.bfloat16

D_MODEL = 1024
D_ATT = 512
D_LRU = 512
N_HEADS = 4
D_HEAD_V = 128
D_HEAD_QK = 64
CONV_W = 4
LRU_C = 8.0
D_FF = 2816
CHUNK = 64
RMS_EPS = 1e-6
NEG_INF = -1e30
LAMBDA_INIT = 0.8 - 0.6 * math.exp(-0.3 * 0)
LOG2E = math.log2(math.e)

LANES = 128
SUBLANES = 8
FF_CHUNK = 256
N_FF_CHUNKS = D_FF // FF_CHUNK
ROW_TILE = 512
ATTN_HEADS_PER_STEP = 2
LRU_TILE = 1024
CACHE_TILE = 4096
VMEM_LIMIT = 56 * 1024 * 1024


def _dot(a, b):
    return jnp.dot(a, b, preferred_element_type=F32)


def _dot_nt(a, b):
    return lax.dot_general(a, b, (((1,), (1,)), ((), ())), preferred_element_type=F32)


def _rms(x, g):
    return x * lax.rsqrt(jnp.mean(x * x, axis=-1, keepdims=True) + RMS_EPS) * g


def _swiglu_ffn(x, g_pre, g_post, wg_ref, wu_ref, wd_ref):
    xn = _rms(x, g_pre).astype(BF16)
    acc = jnp.zeros(x.shape, F32)
    for j in range(N_FF_CHUNKS):
        cols = slice(j * FF_CHUNK, (j + 1) * FF_CHUNK)
        g = _dot(xn, wg_ref[:, cols])
        u = _dot(xn, wu_ref[:, cols])
        h = (g * jax.nn.sigmoid(g) * u).astype(BF16)
        acc = acc + _dot(h, wd_ref[cols, :])
    return x + 0.5 * _rms(acc, g_post)


def _resident(shape):
    nd = len(shape)
    return pl.BlockSpec(shape, lambda i: (0,) * nd, pipeline_mode=pl.Buffered(1))


def _ffn_inproj_kernel(x_ref, g1a_ref, g1b_ref, gma_ref, wg_ref, wu_ref, wd_ref, wq_ref, wkt_ref, wv_ref, wl_ref,
                       x1_ref, qb_ref, kf_ref, kb_ref, vf_ref, vb_ref, lx_ref, lg_ref, *, seq_layout):
    x1 = _swiglu_ffn(x_ref[...], g1a_ref[...], g1b_ref[...], wg_ref, wu_ref, wd_ref)
    x1_ref[...] = x1
    xm = _rms(x1, gma_ref[...]).astype(BF16)
    tm = xm.shape[0]
    qb_ref[...] = (_dot(xm, wq_ref[...]) * (LOG2E / math.sqrt(D_HEAD_QK))).astype(BF16)
    v = _dot(xm, wv_ref[...])
    vb_ref[...] = v.astype(BF16)
    if seq_layout:
        kt = _dot_nt(wkt_ref[...], xm)
        kf_ref[0] = kt
        kb_ref[0, 0] = kt.astype(BF16)
        for h in range(N_HEADS):
            vf_ref[pl.ds(h, tm, stride=N_HEADS), :] = v[:, h * D_HEAD_V:(h + 1) * D_HEAD_V]
    else:
        k = _dot_nt(xm, wkt_ref[...])
        kf_ref[...] = k
        kb_ref[...] = k.astype(BF16)
        vf_ref[...] = v
    lru = _dot(xm, wl_ref[...])
    lx_ref[...] = lru[:, :D_LRU]
    lg_ref[...] = lru[:, D_LRU:]


def _ffn_inproj(x, g1a, g1b, gma, wg, wu, wd, wq, wkt, wv, wl, seq_len=None):
    m = x.shape[0]
    tm = min(ROW_TILE, m)
    rows = lambda d: pl.BlockSpec((tm, d), lambda i: (i, 0))
    row_out = lambda d, dt: jax.ShapeDtypeStruct((m, d), dt)
    seq_layout = seq_len is not None
    if seq_layout:
        nt = seq_len // tm
        b = m // seq_len
        k_shapes = (jax.ShapeDtypeStruct((b, D_ATT, seq_len), F32), jax.ShapeDtypeStruct((b, nt, D_ATT, tm), BF16))
        k_specs = (pl.BlockSpec((1, D_ATT, tm), lambda i: (i // nt, 0, i % nt)),
                   pl.BlockSpec((1, 1, D_ATT, tm), lambda i: (i // nt, i % nt, 0, 0)))
        vf_shape = jax.ShapeDtypeStruct((m * N_HEADS, D_HEAD_V), F32)
        vf_spec = pl.BlockSpec((tm * N_HEADS, D_HEAD_V), lambda i: (i, 0))
    else:
        k_shapes = (row_out(D_ATT, F32), row_out(D_ATT, BF16))
        k_specs = (rows(D_ATT), rows(D_ATT))
        vf_shape = row_out(D_ATT, F32)
        vf_spec = rows(D_ATT)
    out_shape = (row_out(D_MODEL, F32), row_out(D_ATT, BF16), *k_shapes, vf_shape, row_out(D_ATT, BF16),
                 row_out(D_LRU, F32), row_out(D_LRU, F32))
    out_specs = (rows(D_MODEL), rows(D_ATT), *k_specs, vf_spec, rows(D_ATT), rows(D_LRU), rows(D_LRU))
    weights = (g1a, g1b, gma, wg, wu, wd, wq, wkt, wv, wl)
    return pl.pallas_call(
        functools.partial(_ffn_inproj_kernel, seq_layout=seq_layout),
        grid=(m // tm,),
        in_specs=[rows(D_MODEL)] + [_resident(w.shape) for w in weights],
        out_specs=out_specs,
        out_shape=out_shape,
        compiler_params=pltpu.CompilerParams(dimension_semantics=("arbitrary",), vmem_limit_bytes=VMEM_LIMIT),
        name="ffn_inproj",
    )(x, *weights)


def _outproj_ffn_kernel(att_ref, lru_ref, x1_ref, woa_ref, wob_ref, gmb_ref, g2a_ref, g2b_ref,
                        wg_ref, wu_ref, wd_ref, y_ref):
    mix = _dot(att_ref[...], woa_ref[...]) + _dot(lru_ref[...], wob_ref[...])
    x2 = x1_ref[...] + _rms(mix, gmb_ref[...])
    y_ref[...] = _swiglu_ffn(x2, g2a_ref[...], g2b_ref[...], wg_ref, wu_ref, wd_ref)


def _outproj_ffn(att, lru, x1, woa, wob, gmb, g2a, g2b, wg, wu, wd):
    m = x1.shape[0]
    tm = min(ROW_TILE, m)
    rows = lambda d: pl.BlockSpec((tm, d), lambda i: (i, 0))
    weights = (woa, wob, gmb, g2a, g2b, wg, wu, wd)
    return pl.pallas_call(
        _outproj_ffn_kernel,
        grid=(m // tm,),
        in_specs=[rows(D_ATT), rows(D_LRU), rows(D_MODEL)] + [_resident(w.shape) for w in weights],
        out_specs=rows(D_MODEL),
        out_shape=jax.ShapeDtypeStruct((m, D_MODEL), F32),
        compiler_params=pltpu.CompilerParams(dimension_semantics=("arbitrary",), vmem_limit_bytes=VMEM_LIMIT),
        name="outproj_ffn",
    )(att, lru, x1, *weights)


def _split_maps(q):
    lane = lax.broadcasted_iota(jnp.int32, q.shape, 1)
    zero = jnp.zeros_like(q)
    return jnp.where(lane < D_HEAD_QK, q, zero), jnp.where(lane >= D_HEAD_QK, q, zero)


def _init_softmax_state(m_ref, l_ref, acc_ref):
    m_ref[...] = jnp.full(m_ref.shape, NEG_INF, F32)
    l_ref[...] = jnp.zeros(l_ref.shape, F32)
    acc_ref[...] = jnp.zeros(acc_ref.shape, F32)


def _bf16_split3(x):
    hi = x.astype(BF16)
    r1 = x - hi.astype(F32)
    lo = r1.astype(BF16)
    lo2 = (r1 - lo.astype(F32)).astype(BF16)
    return hi, lo, lo2


def _attn_prompt_kernel(slopes_ref, q_ref, kt_ref, v_ref, lq1, lk1, lq2, lk2, g_ref, o_ref,
                        kta_ref, va_ref, mask_ref, m_ref, acc_ref, pa_ref, pb_ref, *, tq, nh):
    hg = pl.program_id(1)
    qi = pl.program_id(2)
    n_kt = kt_ref.shape[1]
    half = D_HEAD_QK
    slopes = [slopes_ref[hg * nh + g] * LOG2E for g in range(nh)]
    n_slots = 2 * nh

    @pl.when(qi == 0)
    def _():
        row = lax.broadcasted_iota(jnp.int32, (LANES, tq), 0)
        col = lax.broadcasted_iota(jnp.int32, (1, tq), 1)
        qrow = lax.broadcasted_iota(jnp.int32, (tq, tq), 0)
        kcol = lax.broadcasted_iota(jnp.int32, (tq, tq), 1)
        zero = jnp.zeros((LANES, tq), F32)
        for g in range(nh):
            terms = _bf16_split3(slopes[g] * col.astype(F32))

            def bias_rows(first):
                blk = zero
                for i, t in enumerate(terms):
                    blk = jnp.where(row == first + i, jnp.broadcast_to(t.astype(F32), (LANES, tq)), blk)
                return blk

            bias0, bias1 = bias_rows(half), bias_rows(0)

            def fill(n, carry):
                kt = kt_ref[0, n, g * LANES:(g + 1) * LANES, :].astype(F32)
                kta_ref[2 * g, n] = jnp.where(row < half, kt, bias0).astype(BF16)
                kta_ref[2 * g + 1, n] = jnp.where(row >= half, kt, bias1).astype(BF16)
                r0 = pl.multiple_of(n * tq, tq)
                va_ref[g, pl.ds(r0, tq), 0:D_HEAD_V] = v_ref[0, pl.ds(r0, tq), g * LANES:(g + 1) * LANES]
                va_ref[g, pl.ds(r0, tq), D_HEAD_V:2 * D_HEAD_V] = jnp.ones((tq, D_HEAD_V), BF16)
                return carry

            lax.fori_loop(0, n_kt, fill, 0)
            mask_ref[g] = jnp.where((qrow // CHUNK) >= (kcol // CHUNK),
                                    slopes[g] * jnp.minimum(2 * (qrow - kcol), 0).astype(F32), NEG_INF)
        acc_ref[...] = jnp.zeros(acc_ref.shape, F32)
        pb_ref[...] = jnp.zeros(pb_ref.shape, BF16)

    lane = lax.broadcasted_iota(jnp.int32, (tq, LANES), 1)
    ones0 = ((lane >= half) & (lane < half + 3)).astype(F32).astype(BF16)
    ones1 = (lane < 3).astype(F32).astype(BF16)
    qa = []
    for g in range(nh):
        q = q_ref[0, :, g * LANES:(g + 1) * LANES]
        qa += [jnp.where(lane < half, q, ones0), jnp.where(lane >= half, q, ones1)]
    m_ref[...] = jnp.full(m_ref.shape, NEG_INF, F32)

    def values(g, j):
        return va_ref[g, pl.ds(pl.multiple_of(j * tq, tq), tq), :]

    def softmax_tile(s, m_prev, tile_bias):
        m_new = jnp.maximum(m_prev, jnp.max(s, axis=1, keepdims=True) + tile_bias)
        shift = m_new - tile_bias
        p = jnp.concatenate([jnp.exp2((s[:, b * LANES:(b + 1) * LANES] - shift).astype(BF16))
                             for b in range(s.shape[1] // LANES)], axis=1)
        return m_new, p

    def update(j, p_in_ref, p_out_ref, diagonal=False):
        logits = [_dot(qa[k], kta_ref[k, j]) for k in range(n_slots)]
        if diagonal:
            logits = [logits[k] + mask_ref[k // 2] for k in range(n_slots)]
        j_prev = jnp.maximum(j - 1, 0)
        pv = [_dot(p_in_ref[k], values(k // 2, j_prev)) for k in range(n_slots)]
        m_prev = [m_ref[k] for k in range(n_slots)]
        acc_prev = [acc_ref[k] for k in range(n_slots)]
        m_next, acc_next, p_next = [], [], []
        for k in range(n_slots):
            tile_bias = slopes[k // 2] * ((j - qi) * tq).astype(F32)
            m_new, p = softmax_tile(logits[k], m_prev[k], tile_bias)
            alpha = jnp.exp2(m_prev[k] - m_new)
            acc = jnp.concatenate([alpha, alpha], axis=1) * (acc_prev[k] + pv[k])
            if diagonal:
                acc = acc + _dot(p, values(k // 2, j))
            acc_next.append(acc)
            m_next.append(m_new)
            p_next.append(p)
        for k in range(n_slots):
            acc_ref[k] = acc_next[k]
            m_ref[k] = m_next[k]
            if not diagonal:
                p_out_ref[k] = p_next[k]

    odd = qi % 2

    @pl.when(odd == 1)
    def _():
        for k in range(n_slots):
            tile_bias = slopes[k // 2] * (-qi * tq).astype(F32)
            m_new, p = softmax_tile(_dot(qa[k], kta_ref[k, 0]), m_ref[k], tile_bias)
            m_ref[k] = m_new
            pb_ref[k] = p
        acc_ref[...] = jnp.zeros(acc_ref.shape, F32)

    def pair(i, carry):
        j = odd + 2 * i
        update(j, pb_ref, pa_ref)

        @pl.when(j + 1 < qi)
        def _():
            update(j + 1, pa_ref, pb_ref)

        return carry

    lax.fori_loop(0, qi // 2, pair, 0)
    update(qi, pb_ref, None, diagonal=True)

    lam = (jnp.exp(jnp.sum(lq1[...] * lk1[...], axis=1, keepdims=True))
           - jnp.exp(jnp.sum(lq2[...] * lk2[...], axis=1, keepdims=True)) + LAMBDA_INIT)
    for g in range(nh):
        a0, a1 = acc_ref[2 * g], acc_ref[2 * g + 1]
        o = a0[:, :D_HEAD_V] / a0[:, D_HEAD_V:] - lam * (a1[:, :D_HEAD_V] / a1[:, D_HEAD_V:])
        o_ref[0, :, g * LANES:(g + 1) * LANES] = (_rms(o, g_ref[...]) * (1.0 - LAMBDA_INIT)).astype(o_ref.dtype)


def _attn_prompt(slopes, q, kt, v, lq1, lk1, lq2, lk2, g):
    b, t, _ = q.shape
    n_kt, tq = kt.shape[1], kt.shape[3]
    nh = ATTN_HEADS_PER_STEP
    w = nh * LANES
    small = lambda a: pl.BlockSpec(a.shape, lambda bi, hi, qi: (0, 0))
    return pl.pallas_call(
        functools.partial(_attn_prompt_kernel, tq=tq, nh=nh),
        grid=(b, N_HEADS // nh, t // tq),
        in_specs=[pl.BlockSpec(memory_space=pltpu.SMEM),
                  pl.BlockSpec((1, tq, w), lambda bi, hi, qi: (bi, qi, hi)),
                  pl.BlockSpec((1, n_kt, w, tq), lambda bi, hi, qi: (bi, 0, hi, 0)),
                  pl.BlockSpec((1, t, w), lambda bi, hi, qi: (bi, 0, hi)),
                  small(lq1), small(lk1), small(lq2), small(lk2), small(g)],
        out_specs=pl.BlockSpec((1, tq, w), lambda bi, hi, qi: (bi, qi, hi)),
        out_shape=jax.ShapeDtypeStruct((b, t, D_ATT), BF16),
        scratch_shapes=[pltpu.VMEM((2 * nh, n_kt, LANES, tq), BF16), pltpu.VMEM((nh, t, 2 * D_HEAD_V), BF16),
                        pltpu.VMEM((nh, tq, tq), F32), pltpu.VMEM((2 * nh, tq, LANES), F32),
                        pltpu.VMEM((2 * nh, tq, 2 * D_HEAD_V), F32)] + [pltpu.VMEM((2 * nh, tq, tq), BF16)] * 2,
        compiler_params=pltpu.CompilerParams(dimension_semantics=("arbitrary",) * 3, vmem_limit_bytes=VMEM_LIMIT),
        name="attn_prompt",
    )(slopes, q, kt, v, lq1, lk1, lq2, lk2, g)


def _attn_sample_kernel(slopes_ref, q_ref, ckt_ref, cv_ref, kn_ref, vn_ref, lq1, lk1, lq2, lk2, g_ref, o_ref,
                        m_ref, l_ref, acc_ref, *, past, tk):
    j = pl.program_id(1)
    tq = q_ref.shape[1]

    @pl.when(j == 0)
    def _():
        _init_softmax_state(m_ref, l_ref, acc_ref)

    def attend(n, k_start, n_valid, scores, values):
        q_pos = past + lax.broadcasted_iota(jnp.int32, (tq, n), 0)
        k_pos = k_start + lax.broadcasted_iota(jnp.int32, (tq, n), 1)
        dist = jnp.abs(q_pos - k_pos).astype(F32)
        dist = jnp.concatenate([dist, dist], axis=0)
        visible = ((q_pos // CHUNK) >= (k_pos // CHUNK)) & (k_pos < k_start + n_valid)
        visible = jnp.concatenate([visible, visible], axis=0)
        state = [(m_ref[h], l_ref[h], acc_ref[h]) for h in range(N_HEADS)]
        for h in range(N_HEADS):
            m_prev, l_prev, acc_prev = state[h]
            q2 = jnp.concatenate(_split_maps(q_ref[0, :, h * LANES:(h + 1) * LANES]), axis=0)
            s = jnp.where(visible, scores(q2, h) - (slopes_ref[h] * LOG2E) * dist, NEG_INF)
            m_new = jnp.maximum(m_prev, jnp.max(s, axis=1, keepdims=True))
            alpha = jnp.exp2(m_prev - m_new)
            ps = [jnp.exp2(s[:, b * LANES:(b + 1) * LANES] - m_new) for b in range(n // LANES)]
            p = jnp.concatenate(ps, axis=1).astype(BF16)
            state[h] = (m_new, alpha * l_prev + functools.reduce(lambda x, y: x + y, ps),
                        alpha * acc_prev + _dot(p, values(h)))
        for h in range(N_HEADS):
            m_ref[h], l_ref[h], acc_ref[h] = state[h]

    attend(tk, j * tk, tk,
           lambda q2, h: _dot(q2, ckt_ref[0, h * LANES:(h + 1) * LANES, :].astype(BF16)),
           lambda h: cv_ref[0, pl.ds(h, tk, stride=N_HEADS), :].astype(BF16))

    @pl.when(j == pl.num_programs(1) - 1)
    def _():
        attend(kn_ref.shape[1], past, tq,
               lambda q2, h: _dot_nt(q2, kn_ref[0, :, h * LANES:(h + 1) * LANES]),
               lambda h: vn_ref[0, :, h * LANES:(h + 1) * LANES])
        lam = (jnp.exp(jnp.sum(lq1[...] * lk1[...], axis=1, keepdims=True))
               - jnp.exp(jnp.sum(lq2[...] * lk2[...], axis=1, keepdims=True)) + LAMBDA_INIT)
        for h in range(N_HEADS):
            sm = acc_ref[h] / jnp.sum(l_ref[h], axis=1, keepdims=True)
            o = _rms(sm[:tq] - lam * sm[tq:], g_ref[...]) * (1.0 - LAMBDA_INIT)
            o_ref[0, :, h * LANES:(h + 1) * LANES] = o.astype(o_ref.dtype)


def _attn_sample(slopes, q, cache_kt, cache_v_rows, k_new, v_new, lq1, lk1, lq2, lk2, g):
    b, t, _ = q.shape
    past = cache_kt.shape[2]
    tk = min(CACHE_TILE, past)
    k_new = jnp.pad(k_new, ((0, 0), (0, -t % LANES), (0, 0)))
    v_new = jnp.pad(v_new, ((0, 0), (0, -t % LANES), (0, 0)))
    small = lambda a: pl.BlockSpec(a.shape, lambda bi, j: (0, 0))
    per_stream = lambda a: pl.BlockSpec((1,) + a.shape[1:], lambda bi, j: (bi, 0, 0))
    return pl.pallas_call(
        functools.partial(_attn_sample_kernel, past=past, tk=tk),
        grid=(b, past // tk),
        in_specs=[pl.BlockSpec(memory_space=pltpu.SMEM),
                  per_stream(q),
                  pl.BlockSpec((1, D_ATT, tk), lambda bi, j: (bi, 0, j)),
                  pl.BlockSpec((1, tk * N_HEADS, D_HEAD_V), lambda bi, j: (bi, j, 0)),
                  per_stream(k_new), per_stream(v_new),
                  small(lq1), small(lk1), small(lq2), small(lk2), small(g)],
        out_specs=pl.BlockSpec((1, t, D_ATT), lambda bi, j: (bi, 0, 0)),
        out_shape=jax.ShapeDtypeStruct((b, t, D_ATT), BF16),
        scratch_shapes=[pltpu.VMEM((N_HEADS, 2 * t, LANES), F32)] * 3,
        compiler_params=pltpu.CompilerParams(dimension_semantics=("arbitrary",) * 2, vmem_limit_bytes=VMEM_LIMIT),
        name="attn_sample",
    )(slopes, q, cache_kt, cache_v_rows, k_new, v_new, lq1, lk1, lq2, lk2, g)


def _sigmoid(x):
    return 0.5 * (1.0 + jnp.tanh(0.5 * x))


def _lru_kernel(x_ref, gate_ref, h0_ref, cbuf_ref, cw_ref, cb_ref, wr_ref, br_ref, wi_ref, bi_ref, lam_ref,
                out_ref, hlast_ref, xbuf_ref, h_ref, a_ref, b_ref, p_ref, hs_ref, *, tt, n_seg):
    ti = pl.program_id(1)
    pad = xbuf_ref.shape[0] - tt
    seg = tt // n_seg

    @pl.when(ti == 0)
    def _():
        xbuf_ref[0:pad, :] = cbuf_ref[0]
        h_ref[...] = h0_ref[0]

    @pl.when(ti > 0)
    def _():
        xbuf_ref[0:pad, :] = xbuf_ref[tt:tt + pad, :]

    xbuf_ref[pad:pad + tt, :] = x_ref[0]
    xc = cb_ref[...] + xbuf_ref[pad:pad + tt, :] * cw_ref[CONV_W - 1:CONV_W, :]
    for j in range(CONV_W - 1):
        back = CONV_W - 1 - j
        xc = xc + xbuf_ref[pad - back:pad - back + tt, :] * cw_ref[j:j + 1, :]

    xb = xc.astype(BF16)
    r = _sigmoid(_dot(xb, wr_ref[...]) + br_ref[...])
    i = _sigmoid(_dot(xb, wi_ref[...]) + bi_ref[...])
    neg_lam = -lam_ref[...]
    softplus = jnp.maximum(neg_lam, 0.0) + jnp.log1p(jnp.exp(-jnp.abs(neg_lam)))
    log_a = -LRU_C * r * softplus
    a = jnp.exp(log_a)
    u = -jnp.tanh(log_a) * (1.0 + a * a)
    b = jnp.where(u > 0.0, u * lax.rsqrt(u), 0.0) * (i * xc)
    n_blk = D_LRU // LANES
    pitch = a_ref.shape[1] // n_seg
    for s in range(n_seg):
        for l in range(n_blk):
            a_ref[l, s * pitch:s * pitch + seg, :] = a[s * seg:(s + 1) * seg, l * LANES:(l + 1) * LANES]
            b_ref[l, s * pitch:s * pitch + seg, :] = b[s * seg:(s + 1) * seg, l * LANES:(l + 1) * LANES]

    h = [jnp.zeros((n_seg, LANES), F32)] * n_blk
    p = [jnp.ones((n_seg, LANES), F32)] * n_blk
    for j in range(seg):
        rows = pl.ds(j, n_seg, stride=pitch)
        for l in range(n_blk):
            a_j = a_ref[l, rows, :]
            h[l] = a_j * h[l] + b_ref[l, rows, :]
            p[l] = a_j * p[l]
            hs_ref[l, rows, :] = h[l]
            p_ref[l, rows, :] = p[l]

    h_end = jnp.concatenate(h, axis=1)
    p_end = jnp.concatenate(p, axis=1)
    carry = h_ref[...]
    for s in range(n_seg):
        rows = slice(s * pitch, s * pitch + seg)
        p_rows = jnp.concatenate([p_ref[l, rows, :] for l in range(n_blk)], axis=1)
        hs_rows = jnp.concatenate([hs_ref[l, rows, :] for l in range(n_blk)], axis=1)
        hs = p_rows * carry + hs_rows
        gate = gate_ref[0, s * seg:(s + 1) * seg, :]
        gelu = 0.5 * gate * (1.0 + jnp.tanh(math.sqrt(2.0 / math.pi) * (gate + 0.044715 * (gate * gate * gate))))
        out_ref[0, s * seg:(s + 1) * seg, :] = (hs * gelu).astype(out_ref.dtype)
        carry = p_end[s:s + 1, :] * carry + h_end[s:s + 1, :]
    h_ref[...] = carry
    hlast_ref[0] = carry


def _lru(x, gate, h0, cbuf, cw, cb, wr, br, wi, bi, lam):
    b, t, _ = x.shape
    tt = min(LRU_TILE, t)
    pad = cbuf.shape[1]
    small = lambda a: pl.BlockSpec(a.shape, lambda bi_, ti: (0, 0))
    tile = pl.BlockSpec((1, tt, D_LRU), lambda bi_, ti: (bi_, ti, 0))
    per_stream = lambda a: pl.BlockSpec((1,) + a.shape[1:], lambda bi_, ti: (bi_, 0, 0))
    return pl.pallas_call(
        functools.partial(_lru_kernel, tt=tt, n_seg=min(SUBLANES, tt // SUBLANES)),
        grid=(b, t // tt),
        in_specs=[tile, tile, per_stream(h0), per_stream(cbuf), small(cw), small(cb),
                  small(wr), small(br), small(wi), small(bi), small(lam)],
        out_specs=(tile, pl.BlockSpec((1, 1, D_LRU), lambda bi_, ti: (bi_, 0, 0))),
        out_shape=(jax.ShapeDtypeStruct((b, t, D_LRU), BF16), jax.ShapeDtypeStruct((b, 1, D_LRU), F32)),
        scratch_shapes=[pltpu.VMEM((tt + pad, D_LRU), F32), pltpu.VMEM((1, D_LRU), F32)]
        + [pltpu.VMEM((D_LRU // LANES, tt + 4 * min(SUBLANES, tt // SUBLANES), LANES), F32)] * 4,
        compiler_params=pltpu.CompilerParams(dimension_semantics=("arbitrary",) * 2, vmem_limit_bytes=VMEM_LIMIT),
        name="lru",
    )(x, gate, h0, cbuf, cw, cb, wr, br, wi, bi, lam)


def _block_diag_dense(w):
    n, c, _ = w.shape
    eye = jnp.eye(n, dtype=w.dtype)
    return (eye[:, None, :, None] * w[:, :, None, :]).reshape(n * c, n * c)


def kernel(x_prompt, x_sample, cache_k, cache_v, state_lru_h, state_conv, w_in, w_out, lambda_q1, lambda_k1,
           lambda_q2, lambda_k2, subln_g, conv_w, conv_b, w_rgate, b_rgate, w_igate, b_igate, lru_lambda,
           ffn1_w_gate, ffn1_w_up, ffn1_w_down, ffn2_w_gate, ffn2_w_up, ffn2_w_down,
           g_ffn1_pre, g_ffn1_post, g_mix_pre, g_mix_post, g_ffn2_pre, g_ffn2_post):
    bp, tp, _ = x_prompt.shape
    bs, ts, _ = x_sample.shape
    past = cache_k.shape[2]

    wg1, wu1, wd1 = ffn1_w_gate[0].astype(BF16), ffn1_w_up[0].astype(BF16), ffn1_w_down[0].astype(BF16)
    wg2, wu2, wd2 = ffn2_w_gate[0].astype(BF16), ffn2_w_up[0].astype(BF16), ffn2_w_down[0].astype(BF16)
    win = w_in[0].astype(BF16)
    wq, wkt, wv, wl = win[:, :D_ATT], win[:, D_ATT:2 * D_ATT].T, win[:, 2 * D_ATT:3 * D_ATT], win[:, 3 * D_ATT:]
    woa, wob = w_out[0, :D_ATT].astype(BF16), w_out[0, D_ATT:].astype(BF16)
    wr = _block_diag_dense(w_rgate[0]).astype(BF16)
    wi = _block_diag_dense(w_igate[0]).astype(BF16)
    slopes = jnp.asarray(2.0 ** (-8.0 * np.arange(1, N_HEADS + 1) / N_HEADS), dtype=F32)
    lam_vecs = (lambda_q1, lambda_k1, lambda_q2, lambda_k2)
    ffn1 = (g_ffn1_pre, g_ffn1_post, g_mix_pre, wg1, wu1, wd1, wq, wkt, wv, wl)
    lru_params = (conv_w[0], conv_b, wr, b_rgate, wi, b_igate, lru_lambda)
    tail_pad = 8 - (CONV_W - 1)

    def mix_and_ffn2(b, t, x1, att, lx, lg, h0, conv_hist):
        lx3 = lx.reshape(b, t, D_LRU)
        cbuf = jnp.pad(conv_hist, ((0, 0), (tail_pad, 0), (0, 0)))
        lru_out, h_last = _lru(lx3, lg.reshape(b, t, D_LRU), h0.reshape(b, 1, D_LRU), cbuf, *lru_params)
        y = _outproj_ffn(att.reshape(b * t, D_ATT), lru_out.reshape(b * t, D_LRU), x1, woa, wob,
                         g_mix_post, g_ffn2_pre, g_ffn2_post, wg2, wu2, wd2)
        conv_new = jnp.concatenate([conv_hist, lx3], axis=1)[:, -(CONV_W - 1):]
        return y.reshape(b, t, D_MODEL), h_last.reshape(1, b, D_LRU), conv_new[None]

    x1, qb, kt, ktb, v_rows, vb, lx, lg = _ffn_inproj(x_prompt.reshape(bp * tp, D_MODEL), *ffn1, seq_len=tp)
    att = _attn_prompt(slopes, qb.reshape(bp, tp, D_ATT), ktb, vb.reshape(bp, tp, D_ATT), *lam_vecs, subln_g)
    yp, hp, cp = mix_and_ffn2(bp, tp, x1, att, lx, lg, jnp.zeros((bp, D_LRU), F32),
                              jnp.zeros((bp, CONV_W - 1, D_LRU), F32))
    kp = kt.reshape(bp, N_HEADS, 2, D_HEAD_QK, tp).transpose(0, 4, 1, 2, 3)[None]
    vp = v_rows.reshape(1, bp, tp, N_HEADS, D_HEAD_V)

    x1, qb, kf, kb, vf, vb, lx, lg = _ffn_inproj(x_sample.reshape(bs * ts, D_MODEL), *ffn1)
    cache_kt = cache_k[0].transpose(0, 2, 3, 4, 1).reshape(bs, D_ATT, past)
    cache_v_rows = cache_v[0].reshape(bs, past * N_HEADS, D_HEAD_V)
    att = _attn_sample(slopes, qb.reshape(bs, ts, D_ATT), cache_kt, cache_v_rows, kb.reshape(bs, ts, D_ATT),
                       vb.reshape(bs, ts, D_ATT), *lam_vecs, subln_g)
    ys, hs, cs = mix_and_ffn2(bs, ts, x1, att, lx, lg, state_lru_h[0], state_conv[0])
    ks = kf.reshape(1, bs, ts, N_HEADS, 2, D_HEAD_QK)
    vs = vf.reshape(1, bs, ts, N_HEADS, D_HEAD_V)
    return (yp, ys, kp, vp, hp, cp, ks, vs, hs, cs)
```

```python
import functools
import math

import jax
import jax.numpy as jnp
import numpy as np
from jax import lax
from jax.experimental import pallas as pl
from jax.experimental.pallas import tpu as pltpu

F32 = jnp.float32
BF16 = jnp.bfloat16

D_MODEL = 1024
D_ATT = 512
D_LRU = 512
N_HEADS = 4
D_HEAD_V = 128
D_HEAD_QK = 64
CONV_W = 4
LRU_C = 8.0
D_FF = 2816
CHUNK = 64
RMS_EPS = 1e-6
NEG_INF = -1e30
LAMBDA_INIT = 0.8 - 0.6 * math.exp(-0.3 * 0)
LOG2E = math.log2(math.e)

LANES = 128
SUBLANES = 8
FF_CHUNK = 256
N_FF_CHUNKS = D_FF // FF_CHUNK
ROW_TILE = 512
OUT_ROW_TILE = 1024
ATTN_HEADS_PER_STEP = 2
LRU_TILE = 1024
CACHE_TILE = 4096
VMEM_LIMIT = 56 * 1024 * 1024


def _dot(a, b):
    return jnp.dot(a, b, preferred_element_type=F32)


def _dot_nt(a, b):
    return lax.dot_general(a, b, (((1,), (1,)), ((), ())), preferred_element_type=F32)


def _rms(x, g):
    return x * lax.rsqrt(jnp.mean(x * x, axis=-1, keepdims=True) + RMS_EPS) * g


def _swiglu_ffn(x, g_pre, g_post, wg_ref, wu_ref, wd_ref):
    xn = _rms(x, g_pre).astype(BF16)
    acc = jnp.zeros(x.shape, F32)
    for j in range(N_FF_CHUNKS):
        cols = slice(j * FF_CHUNK, (j + 1) * FF_CHUNK)
        g = _dot(xn, wg_ref[:, cols])
        u = _dot(xn, wu_ref[:, cols])
        h = (g * jax.nn.sigmoid(g) * u).astype(BF16)
        acc = acc + _dot(h, wd_ref[cols, :])
    return x + 0.5 * _rms(acc, g_post)


def _resident(shape):
    nd = len(shape)
    return pl.BlockSpec(shape, lambda i: (0,) * nd, pipeline_mode=pl.Buffered(1))


def _ffn_inproj_kernel(x_ref, g1a_ref, g1b_ref, gma_ref, wg_ref, wu_ref, wd_ref, wq_ref, wkt_ref, wv_ref, wl_ref,
                       x1_ref, qb_ref, kf_ref, kb_ref, vf_ref, vb_ref, lx_ref, lg_ref, *, seq_layout):
    x1 = _swiglu_ffn(x_ref[...], g1a_ref[...], g1b_ref[...], wg_ref, wu_ref, wd_ref)
    x1_ref[...] = x1
    xm = _rms(x1, gma_ref[...]).astype(BF16)
    tm = xm.shape[0]
    qb_ref[...] = (_dot(xm, wq_ref[...]) * (LOG2E / math.sqrt(D_HEAD_QK))).astype(BF16)
    v = _dot(xm, wv_ref[...])
    vb_ref[...] = v.astype(BF16)
    if seq_layout:
        kt = _dot_nt(wkt_ref[...], xm)
        kf_ref[0] = kt
        kb_ref[0, 0] = kt.astype(BF16)
        for h in range(N_HEADS):
            vf_ref[pl.ds(h, tm, stride=N_HEADS), :] = v[:, h * D_HEAD_V:(h + 1) * D_HEAD_V]
    else:
        k = _dot_nt(xm, wkt_ref[...])
        kf_ref[...] = k
        kb_ref[...] = k.astype(BF16)
        vf_ref[...] = v
    lru = _dot(xm, wl_ref[...])
    lx_ref[...] = lru[:, :D_LRU]
    lg_ref[...] = lru[:, D_LRU:]


def _ffn_inproj(x, g1a, g1b, gma, wg, wu, wd, wq, wkt, wv, wl, seq_len=None):
    m = x.shape[0]
    tm = min(ROW_TILE, m)
    rows = lambda d: pl.BlockSpec((tm, d), lambda i: (i, 0))
    row_out = lambda d, dt: jax.ShapeDtypeStruct((m, d), dt)
    seq_layout = seq_len is not None
    if seq_layout:
        nt = seq_len // tm
        b = m // seq_len
        k_shapes = (jax.ShapeDtypeStruct((b, D_ATT, seq_len), F32), jax.ShapeDtypeStruct((b, nt, D_ATT, tm), BF16))
        k_specs = (pl.BlockSpec((1, D_ATT, tm), lambda i: (i // nt, 0, i % nt)),
                   pl.BlockSpec((1, 1, D_ATT, tm), lambda i: (i // nt, i % nt, 0, 0)))
        vf_shape = jax.ShapeDtypeStruct((m * N_HEADS, D_HEAD_V), F32)
        vf_spec = pl.BlockSpec((tm * N_HEADS, D_HEAD_V), lambda i: (i, 0))
    else:
        k_shapes = (row_out(D_ATT, F32), row_out(D_ATT, BF16))
        k_specs = (rows(D_ATT), rows(D_ATT))
        vf_shape = row_out(D_ATT, F32)
        vf_spec = rows(D_ATT)
    out_shape = (row_out(D_MODEL, F32), row_out(D_ATT, BF16), *k_shapes, vf_shape, row_out(D_ATT, BF16),
                 row_out(D_LRU, F32), row_out(D_LRU, F32))
    out_specs = (rows(D_MODEL), rows(D_ATT), *k_specs, vf_spec, rows(D_ATT), rows(D_LRU), rows(D_LRU))
    weights = (g1a, g1b, gma, wg, wu, wd, wq, wkt, wv, wl)
    return pl.pallas_call(
        functools.partial(_ffn_inproj_kernel, seq_layout=seq_layout),
        grid=(m // tm,),
        in_specs=[rows(D_MODEL)] + [_resident(w.shape) for w in weights],
        out_specs=out_specs,
        out_shape=out_shape,
        compiler_params=pltpu.CompilerParams(dimension_semantics=("arbitrary",), vmem_limit_bytes=VMEM_LIMIT),
        name="ffn_inproj",
    )(x, *weights)


def _outproj_ffn_kernel(att_ref, lru_ref, x1_ref, woa_ref, wob_ref, gmb_ref, g2a_ref, g2b_ref,
                        wg_ref, wu_ref, wd_ref, y_ref):
    mix = _dot(att_ref[...], woa_ref[...]) + _dot(lru_ref[...], wob_ref[...])
    x2 = x1_ref[...] + _rms(mix, gmb_ref[...])
    y_ref[...] = _swiglu_ffn(x2, g2a_ref[...], g2b_ref[...], wg_ref, wu_ref, wd_ref)


def _outproj_ffn(att, lru, x1, woa, wob, gmb, g2a, g2b, wg, wu, wd):
    m = x1.shape[0]
    tm = min(OUT_ROW_TILE, m)
    rows = lambda d: pl.BlockSpec((tm, d), lambda i: (i, 0))
    weights = (woa, wob, gmb, g2a, g2b, wg, wu, wd)
    return pl.pallas_call(
        _outproj_ffn_kernel,
        grid=(m // tm,),
        in_specs=[rows(D_ATT), rows(D_LRU), rows(D_MODEL)] + [_resident(w.shape) for w in weights],
        out_specs=rows(D_MODEL),
        out_shape=jax.ShapeDtypeStruct((m, D_MODEL), F32),
        compiler_params=pltpu.CompilerParams(dimension_semantics=("arbitrary",), vmem_limit_bytes=VMEM_LIMIT),
        name="outproj_ffn",
    )(att, lru, x1, *weights)


def _split_maps(q):
    lane = lax.broadcasted_iota(jnp.int32, q.shape, 1)
    zero = jnp.zeros_like(q)
    return jnp.where(lane < D_HEAD_QK, q, zero), jnp.where(lane >= D_HEAD_QK, q, zero)


def _init_softmax_state(m_ref, l_ref, acc_ref):
    m_ref[...] = jnp.full(m_ref.shape, NEG_INF, F32)
    l_ref[...] = jnp.zeros(l_ref.shape, F32)
    acc_ref[...] = jnp.zeros(acc_ref.shape, F32)


def _bf16_split3(x):
    hi = x.astype(BF16)
    r1 = x - hi.astype(F32)
    lo = r1.astype(BF16)
    lo2 = (r1 - lo.astype(F32)).astype(BF16)
    return hi, lo, lo2


def _attn_prompt_kernel(slopes_ref, q_ref, kt_ref, v_ref, lq1, lk1, lq2, lk2, g_ref, o_ref,
                        kta_ref, va_ref, mask_ref, m_ref, acc_ref, pa_ref, pb_ref, *, tq, nh):
    hg = pl.program_id(1)
    qi = pl.program_id(2)
    n_kt = kt_ref.shape[1]
    half = D_HEAD_QK
    slopes = [slopes_ref[hg * nh + g] * LOG2E for g in range(nh)]
    n_slots = 2 * nh

    @pl.when(qi == 0)
    def _():
        row = lax.broadcasted_iota(jnp.int32, (LANES, tq), 0)
        col = lax.broadcasted_iota(jnp.int32, (1, tq), 1)
        qrow = lax.broadcasted_iota(jnp.int32, (tq, tq), 0)
        kcol = lax.broadcasted_iota(jnp.int32, (tq, tq), 1)
        zero = jnp.zeros((LANES, tq), F32)
        for g in range(nh):
            terms = _bf16_split3(slopes[g] * col.astype(F32))

            def bias_rows(first):
                blk = zero
                for i, t in enumerate(terms):
                    blk = jnp.where(row == first + i, jnp.broadcast_to(t.astype(F32), (LANES, tq)), blk)
                return blk

            bias0, bias1 = bias_rows(half), bias_rows(0)

            def fill(n, carry):
                kt = kt_ref[0, n, g * LANES:(g + 1) * LANES, :].astype(F32)
                kta_ref[2 * g, n] = jnp.where(row < half, kt, bias0).astype(BF16)
                kta_ref[2 * g + 1, n] = jnp.where(row >= half, kt, bias1).astype(BF16)
                r0 = pl.multiple_of(n * tq, tq)
                va_ref[g, pl.ds(r0, tq), 0:D_HEAD_V] = v_ref[0, pl.ds(r0, tq), g * LANES:(g + 1) * LANES]
                va_ref[g, pl.ds(r0, tq), D_HEAD_V:2 * D_HEAD_V] = jnp.ones((tq, D_HEAD_V), BF16)
                return carry

            lax.fori_loop(0, n_kt, fill, 0)
            mask_ref[g] = jnp.where((qrow // CHUNK) >= (kcol // CHUNK),
                                    slopes[g] * jnp.minimum(2 * (qrow - kcol), 0).astype(F32), NEG_INF)
        acc_ref[...] = jnp.zeros(acc_ref.shape, F32)
        pb_ref[...] = jnp.zeros(pb_ref.shape, BF16)

    lane = lax.broadcasted_iota(jnp.int32, (tq, LANES), 1)
    ones0 = ((lane >= half) & (lane < half + 3)).astype(F32).astype(BF16)
    ones1 = (lane < 3).astype(F32).astype(BF16)
    qa = []
    for g in range(nh):
        q = q_ref[0, :, g * LANES:(g + 1) * LANES]
        qa += [jnp.where(lane < half, q, ones0), jnp.where(lane >= half, q, ones1)]
    m_ref[...] = jnp.full(m_ref.shape, NEG_INF, F32)

    def values(g, j):
        return va_ref[g, pl.ds(pl.multiple_of(j * tq, tq), tq), :]

    def softmax_tile(s, m_prev, tile_bias):
        m_new = jnp.maximum(m_prev, jnp.max(s, axis=1, keepdims=True) + tile_bias)
        shift = m_new - tile_bias
        p = jnp.concatenate([jnp.exp2((s[:, b * LANES:(b + 1) * LANES] - shift).astype(BF16))
                             for b in range(s.shape[1] // LANES)], axis=1)
        return m_new, p

    def update(j, p_in_ref, p_out_ref, diagonal=False):
        logits = [_dot(qa[k], kta_ref[k, j]) for k in range(n_slots)]
        if diagonal:
            logits = [logits[k] + mask_ref[k // 2] for k in range(n_slots)]
        j_prev = jnp.maximum(j - 1, 0)
        pv = [_dot(p_in_ref[k], values(k // 2, j_prev)) for k in range(n_slots)]
        m_prev = [m_ref[k] for k in range(n_slots)]
        acc_prev = [acc_ref[k] for k in range(n_slots)]
        m_next, acc_next, p_next = [], [], []
        for k in range(n_slots):
            tile_bias = slopes[k // 2] * ((j - qi) * tq).astype(F32)
            m_new, p = softmax_tile(logits[k], m_prev[k], tile_bias)
            alpha = jnp.exp2(m_prev[k] - m_new)
            acc = jnp.concatenate([alpha, alpha], axis=1) * (acc_prev[k] + pv[k])
            if diagonal:
                acc = acc + _dot(p, values(k // 2, j))
            acc_next.append(acc)
            m_next.append(m_new)
            p_next.append(p)
        for k in range(n_slots):
            acc_ref[k] = acc_next[k]
            m_ref[k] = m_next[k]
            if not diagonal:
                p_out_ref[k] = p_next[k]

    odd = qi % 2

    @pl.when(odd == 1)
    def _():
        for k in range(n_slots):
            tile_bias = slopes[k // 2] * (-qi * tq).astype(F32)
            m_new, p = softmax_tile(_dot(qa[k], kta_ref[k, 0]), m_ref[k], tile_bias)
            m_ref[k] = m_new
            pb_ref[k] = p
        acc_ref[...] = jnp.zeros(acc_ref.shape, F32)

    def pair(i, carry):
        j = odd + 2 * i
        update(j, pb_ref, pa_ref)

        @pl.when(j + 1 < qi)
        def _():
            update(j + 1, pa_ref, pb_ref)

        return carry

    lax.fori_loop(0, qi // 2, pair, 0)
    update(qi, pb_ref, None, diagonal=True)

    lam = (jnp.exp(jnp.sum(lq1[...] * lk1[...], axis=1, keepdims=True))
           - jnp.exp(jnp.sum(lq2[...] * lk2[...], axis=1, keepdims=True)) + LAMBDA_INIT)
    for g in range(nh):
        a0, a1 = acc_ref[2 * g], acc_ref[2 * g + 1]
        o = a0[:, :D_HEAD_V] / a0[:, D_HEAD_V:] - lam * (a1[:, :D_HEAD_V] / a1[:, D_HEAD_V:])
        o_ref[0, :, g * LANES:(g + 1) * LANES] = (_rms(o, g_ref[...]) * (1.0 - LAMBDA_INIT)).astype(o_ref.dtype)


def _attn_prompt(slopes, q, kt, v, lq1, lk1, lq2, lk2, g):
    b, t, _ = q.shape
    n_kt, tq = kt.shape[1], kt.shape[3]
    nh = ATTN_HEADS_PER_STEP
    w = nh * LANES
    small = lambda a: pl.BlockSpec(a.shape, lambda bi, hi, qi: (0, 0))
    return pl.pallas_call(
        functools.partial(_attn_prompt_kernel, tq=tq, nh=nh),
        grid=(b, N_HEADS // nh, t // tq),
        in_specs=[pl.BlockSpec(memory_space=pltpu.SMEM),
                  pl.BlockSpec((1, tq, w), lambda bi, hi, qi: (bi, qi, hi)),
                  pl.BlockSpec((1, n_kt, w, tq), lambda bi, hi, qi: (bi, 0, hi, 0)),
                  pl.BlockSpec((1, t, w), lambda bi, hi, qi: (bi, 0, hi)),
                  small(lq1), small(lk1), small(lq2), small(lk2), small(g)],
        out_specs=pl.BlockSpec((1, tq, w), lambda bi, hi, qi: (bi, qi, hi)),
        out_shape=jax.ShapeDtypeStruct((b, t, D_ATT), BF16),
        scratch_shapes=[pltpu.VMEM((2 * nh, n_kt, LANES, tq), BF16), pltpu.VMEM((nh, t, 2 * D_HEAD_V), BF16),
                        pltpu.VMEM((nh, tq, tq), F32), pltpu.VMEM((2 * nh, tq, LANES), F32),
                        pltpu.VMEM((2 * nh, tq, 2 * D_HEAD_V), F32)] + [pltpu.VMEM((2 * nh, tq, tq), BF16)] * 2,
        compiler_params=pltpu.CompilerParams(dimension_semantics=("arbitrary",) * 3, vmem_limit_bytes=VMEM_LIMIT),
        name="attn_prompt",
    )(slopes, q, kt, v, lq1, lk1, lq2, lk2, g)


def _attn_sample_kernel(slopes_ref, q_ref, ckt_ref, cv_ref, kn_ref, vn_ref, lq1, lk1, lq2, lk2, g_ref, o_ref,
                        m_ref, l_ref, acc_ref, *, past, tk):
    j = pl.program_id(1)
    tq = q_ref.shape[1]

    @pl.when(j == 0)
    def _():
        _init_softmax_state(m_ref, l_ref, acc_ref)

    def attend(n, k_start, n_valid, scores, values):
        q_pos = past + lax.broadcasted_iota(jnp.int32, (tq, n), 0)
        k_pos = k_start + lax.broadcasted_iota(jnp.int32, (tq, n), 1)
        dist = jnp.abs(q_pos - k_pos).astype(F32)
        dist = jnp.concatenate([dist, dist], axis=0)
        visible = ((q_pos // CHUNK) >= (k_pos // CHUNK)) & (k_pos < k_start + n_valid)
        visible = jnp.concatenate([visible, visible], axis=0)
        state = [(m_ref[h], l_ref[h], acc_ref[h]) for h in range(N_HEADS)]
        for h in range(N_HEADS):
            m_prev, l_prev, acc_prev = state[h]
            q2 = jnp.concatenate(_split_maps(q_ref[0, :, h * LANES:(h + 1) * LANES]), axis=0)
            s = jnp.where(visible, scores(q2, h) - (slopes_ref[h] * LOG2E) * dist, NEG_INF)
            m_new = jnp.maximum(m_prev, jnp.max(s, axis=1, keepdims=True))
            alpha = jnp.exp2(m_prev - m_new)
            ps = [jnp.exp2(s[:, b * LANES:(b + 1) * LANES] - m_new) for b in range(n // LANES)]
            p = jnp.concatenate(ps, axis=1).astype(BF16)
            state[h] = (m_new, alpha * l_prev + functools.reduce(lambda x, y: x + y, ps),
                        alpha * acc_prev + _dot(p, values(h)))
        for h in range(N_HEADS):
            m_ref[h], l_ref[h], acc_ref[h] = state[h]

    attend(tk, j * tk, tk,
           lambda q2, h: _dot(q2, ckt_ref[0, h * LANES:(h + 1) * LANES, :].astype(BF16)),
           lambda h: cv_ref[0, pl.ds(h, tk, stride=N_HEADS), :].astype(BF16))

    @pl.when(j == pl.num_programs(1) - 1)
    def _():
        attend(kn_ref.shape[1], past, tq,
               lambda q2, h: _dot_nt(q2, kn_ref[0, :, h * LANES:(h + 1) * LANES]),
               lambda h: vn_ref[0, :, h * LANES:(h + 1) * LANES])
        lam = (jnp.exp(jnp.sum(lq1[...] * lk1[...], axis=1, keepdims=True))
               - jnp.exp(jnp.sum(lq2[...] * lk2[...], axis=1, keepdims=True)) + LAMBDA_INIT)
        for h in range(N_HEADS):
            sm = acc_ref[h] / jnp.sum(l_ref[h], axis=1, keepdims=True)
            o = _rms(sm[:tq] - lam * sm[tq:], g_ref[...]) * (1.0 - LAMBDA_INIT)
            o_ref[0, :, h * LANES:(h + 1) * LANES] = o.astype(o_ref.dtype)


def _attn_sample(slopes, q, cache_kt, cache_v_rows, k_new, v_new, lq1, lk1, lq2, lk2, g):
    b, t, _ = q.shape
    past = cache_kt.shape[2]
    tk = min(CACHE_TILE, past)
    k_new = jnp.pad(k_new, ((0, 0), (0, -t % LANES), (0, 0)))
    v_new = jnp.pad(v_new, ((0, 0), (0, -t % LANES), (0, 0)))
    small = lambda a: pl.BlockSpec(a.shape, lambda bi, j: (0, 0))
    per_stream = lambda a: pl.BlockSpec((1,) + a.shape[1:], lambda bi, j: (bi, 0, 0))
    return pl.pallas_call(
        functools.partial(_attn_sample_kernel, past=past, tk=tk),
        grid=(b, past // tk),
        in_specs=[pl.BlockSpec(memory_space=pltpu.SMEM),
                  per_stream(q),
                  pl.BlockSpec((1, D_ATT, tk), lambda bi, j: (bi, 0, j)),
                  pl.BlockSpec((1, tk * N_HEADS, D_HEAD_V), lambda bi, j: (bi, j, 0)),
                  per_stream(k_new), per_stream(v_new),
                  small(lq1), small(lk1), small(lq2), small(lk2), small(g)],
        out_specs=pl.BlockSpec((1, t, D_ATT), lambda bi, j: (bi, 0, 0)),
        out_shape=jax.ShapeDtypeStruct((b, t, D_ATT), BF16),
        scratch_shapes=[pltpu.VMEM((N_HEADS, 2 * t, LANES), F32)] * 3,
        compiler_params=pltpu.CompilerParams(dimension_semantics=("arbitrary",) * 2, vmem_limit_bytes=VMEM_LIMIT),
        name="attn_sample",
    )(slopes, q, cache_kt, cache_v_rows, k_new, v_new, lq1, lk1, lq2, lk2, g)


def _sigmoid(x):
    return 0.5 * (1.0 + jnp.tanh(0.5 * x))


def _lru_kernel(x_ref, gate_ref, h0_ref, cbuf_ref, cw_ref, cb_ref, wr_ref, br_ref, wi_ref, bi_ref, lam_ref,
                out_ref, hlast_ref, xbuf_ref, h_ref, a_ref, b_ref, p_ref, hs_ref, *, tt, n_seg):
    ti = pl.program_id(1)
    pad = xbuf_ref.shape[0] - tt
    seg = tt // n_seg

    @pl.when(ti == 0)
    def _():
        xbuf_ref[0:pad, :] = cbuf_ref[0]
        h_ref[...] = h0_ref[0]

    @pl.when(ti > 0)
    def _():
        xbuf_ref[0:pad, :] = xbuf_ref[tt:tt + pad, :]

    xbuf_ref[pad:pad + tt, :] = x_ref[0]
    xc = cb_ref[...] + xbuf_ref[pad:pad + tt, :] * cw_ref[CONV_W - 1:CONV_W, :]
    for j in range(CONV_W - 1):
        back = CONV_W - 1 - j
        xc = xc + xbuf_ref[pad - back:pad - back + tt, :] * cw_ref[j:j + 1, :]

    xb = xc.astype(BF16)
    r = _sigmoid(_dot(xb, wr_ref[...]) + br_ref[...])
    i = _sigmoid(_dot(xb, wi_ref[...]) + bi_ref[...])
    neg_lam = -lam_ref[...]
    softplus = jnp.maximum(neg_lam, 0.0) + jnp.log1p(jnp.exp(-jnp.abs(neg_lam)))
    log_a = -LRU_C * r * softplus
    a = jnp.exp(log_a)
    u = -jnp.tanh(log_a) * (1.0 + a * a)
    b = jnp.where(u > 0.0, u * lax.rsqrt(u), 0.0) * (i * xc)
    n_blk = D_LRU // LANES
    pitch = a_ref.shape[1] // n_seg
    for s in range(n_seg):
        for l in range(n_blk):
            a_ref[l, s * pitch:s * pitch + seg, :] = a[s * seg:(s + 1) * seg, l * LANES:(l + 1) * LANES]
            b_ref[l, s * pitch:s * pitch + seg, :] = b[s * seg:(s + 1) * seg, l * LANES:(l + 1) * LANES]

    h = [jnp.zeros((n_seg, LANES), F32)] * n_blk
    p = [jnp.ones((n_seg, LANES), F32)] * n_blk
    for j in range(seg):
        rows = pl.ds(j, n_seg, stride=pitch)
        for l in range(n_blk):
            a_j = a_ref[l, rows, :]
            h[l] = a_j * h[l] + b_ref[l, rows, :]
            p[l] = a_j * p[l]
            hs_ref[l, rows, :] = h[l]
            p_ref[l, rows, :] = p[l]

    h_end = jnp.concatenate(h, axis=1)
    p_end = jnp.concatenate(p, axis=1)
    carry = h_ref[...]
    for s in range(n_seg):
        rows = slice(s * pitch, s * pitch + seg)
        p_rows = jnp.concatenate([p_ref[l, rows, :] for l in range(n_blk)], axis=1)
        hs_rows = jnp.concatenate([hs_ref[l, rows, :] for l in range(n_blk)], axis=1)
        hs = p_rows * carry + hs_rows
        gate = gate_ref[0, s * seg:(s + 1) * seg, :]
        gelu = 0.5 * gate * (1.0 + jnp.tanh(math.sqrt(2.0 / math.pi) * (gate + 0.044715 * (gate * gate * gate))))
        out_ref[0, s * seg:(s + 1) * seg, :] = (hs * gelu).astype(out_ref.dtype)
        carry = p_end[s:s + 1, :] * carry + h_end[s:s + 1, :]
    h_ref[...] = carry
    hlast_ref[0] = carry


def _lru(x, gate, h0, cbuf, cw, cb, wr, br, wi, bi, lam):
    b, t, _ = x.shape
    tt = min(LRU_TILE, t)
    pad = cbuf.shape[1]
    small = lambda a: pl.BlockSpec(a.shape, lambda bi_, ti: (0, 0))
    tile = pl.BlockSpec((1, tt, D_LRU), lambda bi_, ti: (bi_, ti, 0))
    per_stream = lambda a: pl.BlockSpec((1,) + a.shape[1:], lambda bi_, ti: (bi_, 0, 0))
    return pl.pallas_call(
        functools.partial(_lru_kernel, tt=tt, n_seg=min(SUBLANES, tt // SUBLANES)),
        grid=(b, t // tt),
        in_specs=[tile, tile, per_stream(h0), per_stream(cbuf), small(cw), small(cb),
                  small(wr), small(br), small(wi), small(bi), small(lam)],
        out_specs=(tile, pl.BlockSpec((1, 1, D_LRU), lambda bi_, ti: (bi_, 0, 0))),
        out_shape=(jax.ShapeDtypeStruct((b, t, D_LRU), BF16), jax.ShapeDtypeStruct((b, 1, D_LRU), F32)),
        scratch_shapes=[pltpu.VMEM((tt + pad, D_LRU), F32), pltpu.VMEM((1, D_LRU), F32)]
        + [pltpu.VMEM((D_LRU // LANES, tt + 4 * min(SUBLANES, tt // SUBLANES), LANES), F32)] * 4,
        compiler_params=pltpu.CompilerParams(dimension_semantics=("arbitrary",) * 2, vmem_limit_bytes=VMEM_LIMIT),
        name="lru",
    )(x, gate, h0, cbuf, cw, cb, wr, br, wi, bi, lam)


def _block_diag_dense(w):
    n, c, _ = w.shape
    eye = jnp.eye(n, dtype=w.dtype)
    return (eye[:, None, :, None] * w[:, :, None, :]).reshape(n * c, n * c)


def kernel(x_prompt, x_sample, cache_k, cache_v, state_lru_h, state_conv, w_in, w_out, lambda_q1, lambda_k1,
           lambda_q2, lambda_k2, subln_g, conv_w, conv_b, w_rgate, b_rgate, w_igate, b_igate, lru_lambda,
           ffn1_w_gate, ffn1_w_up, ffn1_w_down, ffn2_w_gate, ffn2_w_up, ffn2_w_down,
           g_ffn1_pre, g_ffn1_post, g_mix_pre, g_mix_post, g_ffn2_pre, g_ffn2_post):
    bp, tp, _ = x_prompt.shape
    bs, ts, _ = x_sample.shape
    past = cache_k.shape[2]
    assert w_in.shape[0] == 1, "one layer: LAMBDA_INIT is the depth-0 value"
    assert tp % ROW_TILE == 0 and tp % LRU_TILE == 0 and (bp * tp) % OUT_ROW_TILE == 0
    assert bs * ts <= ROW_TILE and ts % (2 * SUBLANES) == 0
    assert past % min(CACHE_TILE, past) == 0

    wg1, wu1, wd1 = ffn1_w_gate[0].astype(BF16), ffn1_w_up[0].astype(BF16), ffn1_w_down[0].astype(BF16)
    wg2, wu2, wd2 = ffn2_w_gate[0].astype(BF16), ffn2_w_up[0].astype(BF16), ffn2_w_down[0].astype(BF16)
    win = w_in[0].astype(BF16)
    wq, wkt, wv, wl = win[:, :D_ATT], win[:, D_ATT:2 * D_ATT].T, win[:, 2 * D_ATT:3 * D_ATT], win[:, 3 * D_ATT:]
    woa, wob = w_out[0, :D_ATT].astype(BF16), w_out[0, D_ATT:].astype(BF16)
    wr = _block_diag_dense(w_rgate[0]).astype(BF16)
    wi = _block_diag_dense(w_igate[0]).astype(BF16)
    slopes = jnp.asarray(2.0 ** (-8.0 * np.arange(1, N_HEADS + 1) / N_HEADS), dtype=F32)
    lam_vecs = (lambda_q1, lambda_k1, lambda_q2, lambda_k2)
    ffn1 = (g_ffn1_pre, g_ffn1_post, g_mix_pre, wg1, wu1, wd1, wq, wkt, wv, wl)
    lru_params = (conv_w[0], conv_b, wr, b_rgate, wi, b_igate, lru_lambda)
    tail_pad = 8 - (CONV_W - 1)

    def mix_and_ffn2(b, t, x1, att, lx, lg, h0, conv_hist):
        lx3 = lx.reshape(b, t, D_LRU)
        cbuf = jnp.pad(conv_hist, ((0, 0), (tail_pad, 0), (0, 0)))
        lru_out, h_last = _lru(lx3, lg.reshape(b, t, D_LRU), h0.reshape(b, 1, D_LRU), cbuf, *lru_params)
        y = _outproj_ffn(att.reshape(b * t, D_ATT), lru_out.reshape(b * t, D_LRU), x1, woa, wob,
                         g_mix_post, g_ffn2_pre, g_ffn2_post, wg2, wu2, wd2)
        conv_new = jnp.concatenate([conv_hist, lx3], axis=1)[:, -(CONV_W - 1):]
        return y.reshape(b, t, D_MODEL), h_last.reshape(1, b, D_LRU), conv_new[None]

    x1, qb, kt, ktb, v_rows, vb, lx, lg = _ffn_inproj(x_prompt.reshape(bp * tp, D_MODEL), *ffn1, seq_len=tp)
    att = _attn_prompt(slopes, qb.reshape(bp, tp, D_ATT), ktb, vb.reshape(bp, tp, D_ATT), *lam_vecs, subln_g)
    yp, hp, cp = mix_and_ffn2(bp, tp, x1, att, lx, lg, jnp.zeros((bp, D_LRU), F32),
                              jnp.zeros((bp, CONV_W - 1, D_LRU), F32))
    kp = kt.reshape(bp, N_HEADS, 2, D_HEAD_QK, tp).transpose(0, 4, 1, 2, 3)[None]
    vp = v_rows.reshape(1, bp, tp, N_HEADS, D_HEAD_V)

    x1, qb, kf, kb, vf, vb, lx, lg = _ffn_inproj(x_sample.reshape(bs * ts, D_MODEL), *ffn1)
    cache_kt = cache_k[0].transpose(0, 2, 3, 4, 1).reshape(bs, D_ATT, past)
    cache_v_rows = cache_v[0].reshape(bs, past * N_HEADS, D_HEAD_V)
    att = _attn_sample(slopes, qb.reshape(bs, ts, D_ATT), cache_kt, cache_v_rows, kb.reshape(bs, ts, D_ATT),
                       vb.reshape(bs, ts, D_ATT), *lam_vecs, subln_g)
    ys, hs, cs = mix_and_ffn2(bs, ts, x1, att, lx, lg, state_lru_h[0], state_conv[0])
    ks = kf.reshape(1, bs, ts, N_HEADS, 2, D_HEAD_QK)
    vs = vf.reshape(1, bs, ts, N_HEADS, D_HEAD_V)
    return (yp, ys, kp, vp, hp, cp, ks, vs, hs, cs)
```

```python
import functools
import math

import jax
import jax.numpy as jnp
import numpy as np
from jax import lax
from jax.experimental import pallas as pl
from jax.experimental.pallas import tpu as pltpu

F32 = jnp.float32
BF16 = jnp.bfloat16

D_MODEL = 1024
D_ATT = 512
D_LRU = 512
N_HEADS = 4
D_HEAD_V = 128
D_HEAD_QK = 64
CONV_W = 4
LRU_C = 8.0
D_FF = 2816
CHUNK = 64
RMS_EPS = 1e-6
NEG_INF = -1e30
LAMBDA_INIT = 0.8 - 0.6 * math.exp(-0.3 * 0)
LOG2E = math.log2(math.e)

LANES = 128
SUBLANES = 8
FF_CHUNK = 256
N_FF_CHUNKS = D_FF // FF_CHUNK
ROW_TILE = 512
OUT_ROW_TILE = 1024
ATTN_HEADS_PER_STEP = 2
LRU_TILE = 1024
CACHE_TILE = 4096
VMEM_LIMIT = 56 * 1024 * 1024


def _dot(a, b):
    return jnp.dot(a, b, preferred_element_type=F32)


def _dot_nt(a, b):
    return lax.dot_general(a, b, (((1,), (1,)), ((), ())), preferred_element_type=F32)


def _rms(x, g):
    return x * lax.rsqrt(jnp.mean(x * x, axis=-1, keepdims=True) + RMS_EPS) * g


def _swiglu_ffn(x, g_pre, g_post, wg_ref, wu_ref, wd_ref):
    xn = _rms(x, g_pre).astype(BF16)
    acc = jnp.zeros(x.shape, F32)
    for j in range(N_FF_CHUNKS):
        cols = slice(j * FF_CHUNK, (j + 1) * FF_CHUNK)
        g = _dot(xn, wg_ref[:, cols])
        u = _dot(xn, wu_ref[:, cols])
        h = (g * jax.nn.sigmoid(g) * u).astype(BF16)
        acc = acc + _dot(h, wd_ref[cols, :])
    return x + 0.5 * _rms(acc, g_post)


def _resident(shape):
    nd = len(shape)
    return pl.BlockSpec(shape, lambda i: (0,) * nd, pipeline_mode=pl.Buffered(1))


def _sigmoid(x):
    return 0.5 * (1.0 + jnp.tanh(0.5 * x))


def _lru_begin(is_first, h0, cbuf, tt, xbuf_ref, h_ref, *_):
    pad = xbuf_ref.shape[0] - tt

    @pl.when(is_first)
    def _():
        xbuf_ref[0:pad, :] = cbuf
        h_ref[...] = h0

    @pl.when(jnp.logical_not(is_first))
    def _():
        xbuf_ref[0:pad, :] = xbuf_ref[tt:tt + pad, :]


def _lru_tile(x, gate, cw, cb, wr, br, wi, bi, lam, store_out, xbuf_ref, h_ref, a_ref, b_ref, p_ref, hs_ref):
    tt = x.shape[0]
    pad = xbuf_ref.shape[0] - tt
    n_seg = min(SUBLANES, tt // SUBLANES)
    seg = tt // n_seg
    xbuf_ref[pad:pad + tt, :] = x
    xc = cb + xbuf_ref[pad:pad + tt, :] * cw[CONV_W - 1:CONV_W, :]
    for j in range(CONV_W - 1):
        back = CONV_W - 1 - j
        xc = xc + xbuf_ref[pad - back:pad - back + tt, :] * cw[j:j + 1, :]

    xb = xc.astype(BF16)
    r = _sigmoid(_dot(xb, wr) + br)
    i = _sigmoid(_dot(xb, wi) + bi)
    neg_lam = -lam
    softplus = jnp.maximum(neg_lam, 0.0) + jnp.log1p(jnp.exp(-jnp.abs(neg_lam)))
    log_a = -LRU_C * r * softplus
    a = jnp.exp(log_a)
    u = -jnp.tanh(log_a) * (1.0 + a * a)
    b = jnp.where(u > 0.0, u * lax.rsqrt(u), 0.0) * (i * xc)
    n_blk = D_LRU // LANES
    pitch = a_ref.shape[1] // n_seg
    for s in range(n_seg):
        for l in range(n_blk):
            a_ref[l, s * pitch:s * pitch + seg, :] = a[s * seg:(s + 1) * seg, l * LANES:(l + 1) * LANES]
            b_ref[l, s * pitch:s * pitch + seg, :] = b[s * seg:(s + 1) * seg, l * LANES:(l + 1) * LANES]

    h = [jnp.zeros((n_seg, LANES), F32)] * n_blk
    p = [jnp.ones((n_seg, LANES), F32)] * n_blk
    for j in range(seg):
        rows = pl.ds(j, n_seg, stride=pitch)
        for l in range(n_blk):
            a_j = a_ref[l, rows, :]
            h[l] = a_j * h[l] + b_ref[l, rows, :]
            p[l] = a_j * p[l]
            hs_ref[l, rows, :] = h[l]
            p_ref[l, rows, :] = p[l]

    h_end = jnp.concatenate(h, axis=1)
    p_end = jnp.concatenate(p, axis=1)
    carry = h_ref[...]
    for s in range(n_seg):
        rows = slice(s * pitch, s * pitch + seg)
        p_rows = jnp.concatenate([p_ref[l, rows, :] for l in range(n_blk)], axis=1)
        hs_rows = jnp.concatenate([hs_ref[l, rows, :] for l in range(n_blk)], axis=1)
        hs = p_rows * carry + hs_rows
        g = gate[s * seg:(s + 1) * seg, :]
        gelu = 0.5 * g * (1.0 + jnp.tanh(math.sqrt(2.0 / math.pi) * (g + 0.044715 * (g * g * g))))
        store_out(slice(s * seg, (s + 1) * seg), hs * gelu)
        carry = p_end[s:s + 1, :] * carry + h_end[s:s + 1, :]
    h_ref[...] = carry
    return carry


def _lru_scratch(tt):
    n_seg = min(SUBLANES, tt // SUBLANES)
    return ([pltpu.VMEM((tt + SUBLANES, D_LRU), F32), pltpu.VMEM((1, D_LRU), F32)]
            + [pltpu.VMEM((D_LRU // LANES, tt + 4 * n_seg, LANES), F32)] * 4)


def _lru_kernel(x_ref, gate_ref, h0_ref, cbuf_ref, cw_ref, cb_ref, wr_ref, br_ref, wi_ref, bi_ref, lam_ref,
                out_ref, hlast_ref, *scratch):
    def store_out(rows, value):
        out_ref[0, rows, :] = value.astype(out_ref.dtype)

    _lru_begin(pl.program_id(1) == 0, h0_ref[0], cbuf_ref[0], x_ref.shape[1], *scratch)
    hlast_ref[0] = _lru_tile(x_ref[0], gate_ref[0], cw_ref[...], cb_ref[...], wr_ref[...], br_ref[...],
                             wi_ref[...], bi_ref[...], lam_ref[...], store_out, *scratch)


def _lru(x, gate, h0, cbuf, cw, cb, wr, br, wi, bi, lam):
    b, t, _ = x.shape
    tt = min(LRU_TILE, t)
    small = lambda a: pl.BlockSpec(a.shape, lambda bi_, ti: (0, 0))
    tile = pl.BlockSpec((1, tt, D_LRU), lambda bi_, ti: (bi_, ti, 0))
    per_stream = lambda a: pl.BlockSpec((1,) + a.shape[1:], lambda bi_, ti: (bi_, 0, 0))
    return pl.pallas_call(
        _lru_kernel,
        grid=(b, t // tt),
        in_specs=[tile, tile, per_stream(h0), per_stream(cbuf), small(cw), small(cb),
                  small(wr), small(br), small(wi), small(bi), small(lam)],
        out_specs=(tile, pl.BlockSpec((1, 1, D_LRU), lambda bi_, ti: (bi_, 0, 0))),
        out_shape=(jax.ShapeDtypeStruct((b, t, D_LRU), BF16), jax.ShapeDtypeStruct((b, 1, D_LRU), F32)),
        scratch_shapes=_lru_scratch(tt),
        compiler_params=pltpu.CompilerParams(dimension_semantics=("arbitrary",) * 2, vmem_limit_bytes=VMEM_LIMIT),
        name="lru",
    )(x, gate, h0, cbuf, cw, cb, wr, br, wi, bi, lam)


def _ffn_inproj_kernel(x_ref, g1a_ref, g1b_ref, gma_ref, wg_ref, wu_ref, wd_ref, wq_ref, wkt_ref, wv_ref, wl_ref,
                       *refs, seq_tiles):
    tm = x_ref.shape[0]
    if seq_tiles is None:
        x1_ref, qb_ref, kf_ref, kb_ref, vf_ref, vb_ref, lx_ref, lg_ref = refs
    else:
        (h0_ref, cbuf_ref, cw_ref, cb_ref, wr_ref, br_ref, wi_ref, bi_ref, lam_ref,
         x1_ref, qb_ref, kf_ref, kb_ref, vf_ref, vb_ref, lx_ref, lo_ref, hlast_ref, *scratch) = refs
        _lru_begin(pl.program_id(0) % seq_tiles == 0, h0_ref[0], cbuf_ref[0], tm, *scratch)
    x1 = _swiglu_ffn(x_ref[...], g1a_ref[...], g1b_ref[...], wg_ref, wu_ref, wd_ref)
    x1_ref[...] = x1
    xm = _rms(x1, gma_ref[...]).astype(BF16)
    lru = _dot(xm, wl_ref[...])
    lx_ref[...] = lru[:, :D_LRU]
    qb_ref[...] = (_dot(xm, wq_ref[...]) * (LOG2E / math.sqrt(D_HEAD_QK))).astype(BF16)
    v = _dot(xm, wv_ref[...])
    vb_ref[...] = v.astype(BF16)
    if seq_tiles is None:
        lg_ref[...] = lru[:, D_LRU:]
        k = _dot_nt(xm, wkt_ref[...])
        kf_ref[...] = k
        kb_ref[...] = k.astype(BF16)
        vf_ref[...] = v
    else:
        kt = _dot_nt(wkt_ref[...], xm)
        kf_ref[0] = kt
        kb_ref[0, 0] = kt.astype(BF16)
        for h in range(N_HEADS):
            vf_ref[pl.ds(h, tm, stride=N_HEADS), :] = v[:, h * D_HEAD_V:(h + 1) * D_HEAD_V]

        def store_out(rows, value):
            lo_ref[rows, :] = value.astype(lo_ref.dtype)

        hlast_ref[0] = _lru_tile(lru[:, :D_LRU], lru[:, D_LRU:], cw_ref[...], cb_ref[...], wr_ref[...], br_ref[...],
                                 wi_ref[...], bi_ref[...], lam_ref[...], store_out, *scratch)


def _ffn_inproj(x, g1a, g1b, gma, wg, wu, wd, wq, wkt, wv, wl, seq_len=None, lru_state=None, lru_params=None):
    m = x.shape[0]
    tm = min(ROW_TILE, m)
    rows = lambda d: pl.BlockSpec((tm, d), lambda i: (i, 0))
    row_out = lambda d, dt: jax.ShapeDtypeStruct((m, d), dt)
    weights = (g1a, g1b, gma, wg, wu, wd, wq, wkt, wv, wl)
    in_specs = [rows(D_MODEL)] + [_resident(w.shape) for w in weights]
    operands = (x, *weights)
    scratch = []
    if seq_len is not None:
        nt = seq_len // tm
        b = m // seq_len
        k_shapes = (jax.ShapeDtypeStruct((b, D_ATT, seq_len), F32), jax.ShapeDtypeStruct((b, nt, D_ATT, tm), BF16))
        k_specs = (pl.BlockSpec((1, D_ATT, tm), lambda i: (i // nt, 0, i % nt)),
                   pl.BlockSpec((1, 1, D_ATT, tm), lambda i: (i // nt, i % nt, 0, 0)))
        vf_shape = jax.ShapeDtypeStruct((m * N_HEADS, D_HEAD_V), F32)
        vf_spec = pl.BlockSpec((tm * N_HEADS, D_HEAD_V), lambda i: (i, 0))
        per_seq = lambda a: pl.BlockSpec((1,) + a.shape[1:], lambda i: (i // nt, 0, 0))
        tail_shapes = (row_out(D_LRU, F32), row_out(D_LRU, BF16), jax.ShapeDtypeStruct((b, 1, D_LRU), F32))
        tail_specs = (rows(D_LRU), rows(D_LRU), pl.BlockSpec((1, 1, D_LRU), lambda i: (i // nt, 0, 0)))
        in_specs += [per_seq(a) for a in lru_state] + [_resident(p.shape) for p in lru_params]
        operands += (*lru_state, *lru_params)
        scratch = _lru_scratch(tm)
        seq_tiles = nt
    else:
        k_shapes = (row_out(D_ATT, F32), row_out(D_ATT, BF16))
        k_specs = (rows(D_ATT), rows(D_ATT))
        vf_shape = row_out(D_ATT, F32)
        vf_spec = rows(D_ATT)
        tail_shapes = (row_out(D_LRU, F32), row_out(D_LRU, F32))
        tail_specs = (rows(D_LRU), rows(D_LRU))
        seq_tiles = None
    out_shape = (row_out(D_MODEL, F32), row_out(D_ATT, BF16), *k_shapes, vf_shape, row_out(D_ATT, BF16), *tail_shapes)
    out_specs = (rows(D_MODEL), rows(D_ATT), *k_specs, vf_spec, rows(D_ATT), *tail_specs)
    return pl.pallas_call(
        functools.partial(_ffn_inproj_kernel, seq_tiles=seq_tiles),
        grid=(m // tm,),
        in_specs=in_specs,
        out_specs=out_specs,
        out_shape=out_shape,
        scratch_shapes=scratch,
        compiler_params=pltpu.CompilerParams(dimension_semantics=("arbitrary",), vmem_limit_bytes=VMEM_LIMIT),
        name="ffn_inproj",
    )(*operands)


def _outproj_ffn_kernel(att_ref, lru_ref, x1_ref, woa_ref, wob_ref, gmb_ref, g2a_ref, g2b_ref,
                        wg_ref, wu_ref, wd_ref, y_ref):
    mix = _dot(att_ref[...], woa_ref[...]) + _dot(lru_ref[...], wob_ref[...])
    x2 = x1_ref[...] + _rms(mix, gmb_ref[...])
    y_ref[...] = _swiglu_ffn(x2, g2a_ref[...], g2b_ref[...], wg_ref, wu_ref, wd_ref)


def _outproj_ffn(att, lru, x1, woa, wob, gmb, g2a, g2b, wg, wu, wd):
    m = x1.shape[0]
    tm = min(OUT_ROW_TILE, m)
    rows = lambda d: pl.BlockSpec((tm, d), lambda i: (i, 0))
    weights = (woa, wob, gmb, g2a, g2b, wg, wu, wd)
    return pl.pallas_call(
        _outproj_ffn_kernel,
        grid=(m // tm,),
        in_specs=[rows(D_ATT), rows(D_LRU), rows(D_MODEL)] + [_resident(w.shape) for w in weights],
        out_specs=rows(D_MODEL),
        out_shape=jax.ShapeDtypeStruct((m, D_MODEL), F32),
        compiler_params=pltpu.CompilerParams(dimension_semantics=("arbitrary",), vmem_limit_bytes=VMEM_LIMIT),
        name="outproj_ffn",
    )(att, lru, x1, *weights)


def _split_maps(q):
    lane = lax.broadcasted_iota(jnp.int32, q.shape, 1)
    zero = jnp.zeros_like(q)
    return jnp.where(lane < D_HEAD_QK, q, zero), jnp.where(lane >= D_HEAD_QK, q, zero)


def _init_softmax_state(m_ref, l_ref, acc_ref):
    m_ref[...] = jnp.full(m_ref.shape, NEG_INF, F32)
    l_ref[...] = jnp.zeros(l_ref.shape, F32)
    acc_ref[...] = jnp.zeros(acc_ref.shape, F32)


def _bf16_split3(x):
    hi = x.astype(BF16)
    r1 = x - hi.astype(F32)
    lo = r1.astype(BF16)
    lo2 = (r1 - lo.astype(F32)).astype(BF16)
    return hi, lo, lo2


def _attn_prompt_kernel(slopes_ref, q_ref, kt_ref, v_ref, lq1, lk1, lq2, lk2, g_ref, o_ref,
                        kta_ref, va_ref, mask_ref, m_ref, acc_ref, pa_ref, pb_ref, *, tq, nh):
    hg = pl.program_id(1)
    qi = pl.program_id(2)
    n_kt = kt_ref.shape[1]
    half = D_HEAD_QK
    slopes = [slopes_ref[hg * nh + g] * LOG2E for g in range(nh)]
    n_slots = 2 * nh

    @pl.when(qi == 0)
    def _():
        row = lax.broadcasted_iota(jnp.int32, (LANES, tq), 0)
        col = lax.broadcasted_iota(jnp.int32, (1, tq), 1)
        qrow = lax.broadcasted_iota(jnp.int32, (tq, tq), 0)
        kcol = lax.broadcasted_iota(jnp.int32, (tq, tq), 1)
        zero = jnp.zeros((LANES, tq), F32)
        for g in range(nh):
            terms = _bf16_split3(slopes[g] * col.astype(F32))

            def bias_rows(first):
                blk = zero
                for i, t in enumerate(terms):
                    blk = jnp.where(row == first + i, jnp.broadcast_to(t.astype(F32), (LANES, tq)), blk)
                return blk

            bias0, bias1 = bias_rows(half), bias_rows(0)

            def fill(n, carry):
                kt = kt_ref[0, n, g * LANES:(g + 1) * LANES, :].astype(F32)
                kta_ref[2 * g, n] = jnp.where(row < half, kt, bias0).astype(BF16)
                kta_ref[2 * g + 1, n] = jnp.where(row >= half, kt, bias1).astype(BF16)
                r0 = pl.multiple_of(n * tq, tq)
                va_ref[g, pl.ds(r0, tq), 0:D_HEAD_V] = v_ref[0, pl.ds(r0, tq), g * LANES:(g + 1) * LANES]
                va_ref[g, pl.ds(r0, tq), D_HEAD_V:2 * D_HEAD_V] = jnp.ones((tq, D_HEAD_V), BF16)
                return carry

            lax.fori_loop(0, n_kt, fill, 0)
            mask_ref[g] = jnp.where((qrow // CHUNK) >= (kcol // CHUNK),
                                    slopes[g] * jnp.minimum(2 * (qrow - kcol), 0).astype(F32), NEG_INF)
        acc_ref[...] = jnp.zeros(acc_ref.shape, F32)
        pb_ref[...] = jnp.zeros(pb_ref.shape, BF16)

    lane = lax.broadcasted_iota(jnp.int32, (tq, LANES), 1)
    ones0 = ((lane >= half) & (lane < half + 3)).astype(F32).astype(BF16)
    ones1 = (lane < 3).astype(F32).astype(BF16)
    qa = []
    for g in range(nh):
        q = q_ref[0, :, g * LANES:(g + 1) * LANES]
        qa += [jnp.where(lane < half, q, ones0), jnp.where(lane >= half, q, ones1)]
    m_ref[...] = jnp.full(m_ref.shape, NEG_INF, F32)

    def values(g, j):
        return va_ref[g, pl.ds(pl.multiple_of(j * tq, tq), tq), :]

    def softmax_tile(s, m_prev, tile_bias):
        m_new = jnp.maximum(m_prev, jnp.max(s, axis=1, keepdims=True) + tile_bias)
        shift = m_new - tile_bias
        p = jnp.concatenate([jnp.exp2((s[:, b * LANES:(b + 1) * LANES] - shift).astype(BF16))
                             for b in range(s.shape[1] // LANES)], axis=1)
        return m_new, p

    def update(j, p_in_ref, p_out_ref, diagonal=False):
        logits = [_dot(qa[k], kta_ref[k, j]) for k in range(n_slots)]
        if diagonal:
            logits = [logits[k] + mask_ref[k // 2] for k in range(n_slots)]
        j_prev = jnp.maximum(j - 1, 0)
        pv = [_dot(p_in_ref[k], values(k // 2, j_prev)) for k in range(n_slots)]
        m_prev = [m_ref[k] for k in range(n_slots)]
        acc_prev = [acc_ref[k] for k in range(n_slots)]
        m_next, acc_next, p_next = [], [], []
        for k in range(n_slots):
            tile_bias = slopes[k // 2] * ((j - qi) * tq).astype(F32)
            m_new, p = softmax_tile(logits[k], m_prev[k], tile_bias)
            alpha = jnp.exp2(m_prev[k] - m_new)
            acc = jnp.concatenate([alpha, alpha], axis=1) * (acc_prev[k] + pv[k])
            if diagonal:
                acc = acc + _dot(p, values(k // 2, j))
            acc_next.append(acc)
            m_next.append(m_new)
            p_next.append(p)
        for k in range(n_slots):
            acc_ref[k] = acc_next[k]
            m_ref[k] = m_next[k]
            if not diagonal:
                p_out_ref[k] = p_next[k]

    odd = qi % 2

    @pl.when(odd == 1)
    def _():
        for k in range(n_slots):
            tile_bias = slopes[k // 2] * (-qi * tq).astype(F32)
            m_new, p = softmax_tile(_dot(qa[k], kta_ref[k, 0]), m_ref[k], tile_bias)
            m_ref[k] = m_new
            pb_ref[k] = p
        acc_ref[...] = jnp.zeros(acc_ref.shape, F32)

    def pair(i, carry):
        j = odd + 2 * i
        update(j, pb_ref, pa_ref)

        @pl.when(j + 1 < qi)
        def _():
            update(j + 1, pa_ref, pb_ref)

        return carry

    lax.fori_loop(0, qi // 2, pair, 0)
    update(qi, pb_ref, None, diagonal=True)

    lam = (jnp.exp(jnp.sum(lq1[...] * lk1[...], axis=1, keepdims=True))
           - jnp.exp(jnp.sum(lq2[...] * lk2[...], axis=1, keepdims=True)) + LAMBDA_INIT)
    for g in range(nh):
        a0, a1 = acc_ref[2 * g], acc_ref[2 * g + 1]
        o = a0[:, :D_HEAD_V] / a0[:, D_HEAD_V:] - lam * (a1[:, :D_HEAD_V] / a1[:, D_HEAD_V:])
        o_ref[0, :, g * LANES:(g + 1) * LANES] = (_rms(o, g_ref[...]) * (1.0 - LAMBDA_INIT)).astype(o_ref.dtype)


def _attn_prompt(slopes, q, kt, v, lq1, lk1, lq2, lk2, g):
    b, t, _ = q.shape
    n_kt, tq = kt.shape[1], kt.shape[3]
    nh = ATTN_HEADS_PER_STEP
    w = nh * LANES
    small = lambda a: pl.BlockSpec(a.shape, lambda bi, hi, qi: (0, 0))
    return pl.pallas_call(
        functools.partial(_attn_prompt_kernel, tq=tq, nh=nh),
        grid=(b, N_HEADS // nh, t // tq),
        in_specs=[pl.BlockSpec(memory_space=pltpu.SMEM),
                  pl.BlockSpec((1, tq, w), lambda bi, hi, qi: (bi, qi, hi)),
                  pl.BlockSpec((1, n_kt, w, tq), lambda bi, hi, qi: (bi, 0, hi, 0)),
                  pl.BlockSpec((1, t, w), lambda bi, hi, qi: (bi, 0, hi)),
                  small(lq1), small(lk1), small(lq2), small(lk2), small(g)],
        out_specs=pl.BlockSpec((1, tq, w), lambda bi, hi, qi: (bi, qi, hi)),
        out_shape=jax.ShapeDtypeStruct((b, t, D_ATT), BF16),
        scratch_shapes=[pltpu.VMEM((2 * nh, n_kt, LANES, tq), BF16), pltpu.VMEM((nh, t, 2 * D_HEAD_V), BF16),
                        pltpu.VMEM((nh, tq, tq), F32), pltpu.VMEM((2 * nh, tq, LANES), F32),
                        pltpu.VMEM((2 * nh, tq, 2 * D_HEAD_V), F32)] + [pltpu.VMEM((2 * nh, tq, tq), BF16)] * 2,
        compiler_params=pltpu.CompilerParams(dimension_semantics=("arbitrary",) * 3, vmem_limit_bytes=VMEM_LIMIT),
        name="attn_prompt",
    )(slopes, q, kt, v, lq1, lk1, lq2, lk2, g)


def _attn_sample_kernel(slopes_ref, q_ref, ckt_ref, cv_ref, kn_ref, vn_ref, lq1, lk1, lq2, lk2, g_ref, o_ref,
                        m_ref, l_ref, acc_ref, *, past, tk):
    j = pl.program_id(1)
    tq = q_ref.shape[1]

    @pl.when(j == 0)
    def _():
        _init_softmax_state(m_ref, l_ref, acc_ref)

    def attend(n, k_start, n_valid, scores, values):
        q_pos = past + lax.broadcasted_iota(jnp.int32, (tq, n), 0)
        k_pos = k_start + lax.broadcasted_iota(jnp.int32, (tq, n), 1)
        dist = jnp.abs(q_pos - k_pos).astype(F32)
        dist = jnp.concatenate([dist, dist], axis=0)
        visible = ((q_pos // CHUNK) >= (k_pos // CHUNK)) & (k_pos < k_start + n_valid)
        visible = jnp.concatenate([visible, visible], axis=0)
        state = [(m_ref[h], l_ref[h], acc_ref[h]) for h in range(N_HEADS)]
        for h in range(N_HEADS):
            m_prev, l_prev, acc_prev = state[h]
            q2 = jnp.concatenate(_split_maps(q_ref[0, :, h * LANES:(h + 1) * LANES]), axis=0)
            s = jnp.where(visible, scores(q2, h) - (slopes_ref[h] * LOG2E) * dist, NEG_INF)
            m_new = jnp.maximum(m_prev, jnp.max(s, axis=1, keepdims=True))
            alpha = jnp.exp2(m_prev - m_new)
            ps = [jnp.exp2(s[:, b * LANES:(b + 1) * LANES] - m_new) for b in range(n // LANES)]
            p = jnp.concatenate(ps, axis=1).astype(BF16)
            state[h] = (m_new, alpha * l_prev + functools.reduce(lambda x, y: x + y, ps),
                        alpha * acc_prev + _dot(p, values(h)))
        for h in range(N_HEADS):
            m_ref[h], l_ref[h], acc_ref[h] = state[h]

    attend(tk, j * tk, tk,
           lambda q2, h: _dot(q2, ckt_ref[0, h * LANES:(h + 1) * LANES, :].astype(BF16)),
           lambda h: cv_ref[0, pl.ds(h, tk, stride=N_HEADS), :].astype(BF16))

    @pl.when(j == pl.num_programs(1) - 1)
    def _():
        attend(kn_ref.shape[1], past, tq,
               lambda q2, h: _dot_nt(q2, kn_ref[0, :, h * LANES:(h + 1) * LANES]),
               lambda h: vn_ref[0, :, h * LANES:(h + 1) * LANES])
        lam = (jnp.exp(jnp.sum(lq1[...] * lk1[...], axis=1, keepdims=True))
               - jnp.exp(jnp.sum(lq2[...] * lk2[...], axis=1, keepdims=True)) + LAMBDA_INIT)
        for h in range(N_HEADS):
            sm = acc_ref[h] / jnp.sum(l_ref[h], axis=1, keepdims=True)
            o = _rms(sm[:tq] - lam * sm[tq:], g_ref[...]) * (1.0 - LAMBDA_INIT)
            o_ref[0, :, h * LANES:(h + 1) * LANES] = o.astype(o_ref.dtype)


def _attn_sample(slopes, q, cache_kt, cache_v_rows, k_new, v_new, lq1, lk1, lq2, lk2, g):
    b, t, _ = q.shape
    past = cache_kt.shape[2]
    tk = min(CACHE_TILE, past)
    k_new = jnp.pad(k_new, ((0, 0), (0, -t % LANES), (0, 0)))
    v_new = jnp.pad(v_new, ((0, 0), (0, -t % LANES), (0, 0)))
    small = lambda a: pl.BlockSpec(a.shape, lambda bi, j: (0, 0))
    per_stream = lambda a: pl.BlockSpec((1,) + a.shape[1:], lambda bi, j: (bi, 0, 0))
    return pl.pallas_call(
        functools.partial(_attn_sample_kernel, past=past, tk=tk),
        grid=(b, past // tk),
        in_specs=[pl.BlockSpec(memory_space=pltpu.SMEM),
                  per_stream(q),
                  pl.BlockSpec((1, D_ATT, tk), lambda bi, j: (bi, 0, j)),
                  pl.BlockSpec((1, tk * N_HEADS, D_HEAD_V), lambda bi, j: (bi, j, 0)),
                  per_stream(k_new), per_stream(v_new),
                  small(lq1), small(lk1), small(lq2), small(lk2), small(g)],
        out_specs=pl.BlockSpec((1, t, D_ATT), lambda bi, j: (bi, 0, 0)),
        out_shape=jax.ShapeDtypeStruct((b, t, D_ATT), BF16),
        scratch_shapes=[pltpu.VMEM((N_HEADS, 2 * t, LANES), F32)] * 3,
        compiler_params=pltpu.CompilerParams(dimension_semantics=("arbitrary",) * 2, vmem_limit_bytes=VMEM_LIMIT),
        name="attn_sample",
    )(slopes, q, cache_kt, cache_v_rows, k_new, v_new, lq1, lk1, lq2, lk2, g)


def _block_diag_dense(w):
    n, c, _ = w.shape
    eye = jnp.eye(n, dtype=w.dtype)
    return (eye[:, None, :, None] * w[:, :, None, :]).reshape(n * c, n * c)


def kernel(x_prompt, x_sample, cache_k, cache_v, state_lru_h, state_conv, w_in, w_out, lambda_q1, lambda_k1,
           lambda_q2, lambda_k2, subln_g, conv_w, conv_b, w_rgate, b_rgate, w_igate, b_igate, lru_lambda,
           ffn1_w_gate, ffn1_w_up, ffn1_w_down, ffn2_w_gate, ffn2_w_up, ffn2_w_down,
           g_ffn1_pre, g_ffn1_post, g_mix_pre, g_mix_post, g_ffn2_pre, g_ffn2_post):
    bp, tp, _ = x_prompt.shape
    bs, ts, _ = x_sample.shape
    past = cache_k.shape[2]
    assert w_in.shape[0] == 1, "one layer: LAMBDA_INIT is the depth-0 value"
    assert tp % ROW_TILE == 0 and (bp * tp) % OUT_ROW_TILE == 0
    assert bs * ts <= ROW_TILE and ts % (2 * SUBLANES) == 0
    assert past % min(CACHE_TILE, past) == 0

    wg1, wu1, wd1 = ffn1_w_gate[0].astype(BF16), ffn1_w_up[0].astype(BF16), ffn1_w_down[0].astype(BF16)
    wg2, wu2, wd2 = ffn2_w_gate[0].astype(BF16), ffn2_w_up[0].astype(BF16), ffn2_w_down[0].astype(BF16)
    win = w_in[0].astype(BF16)
    wq, wkt, wv, wl = win[:, :D_ATT], win[:, D_ATT:2 * D_ATT].T, win[:, 2 * D_ATT:3 * D_ATT], win[:, 3 * D_ATT:]
    woa, wob = w_out[0, :D_ATT].astype(BF16), w_out[0, D_ATT:].astype(BF16)
    wr = _block_diag_dense(w_rgate[0]).astype(BF16)
    wi = _block_diag_dense(w_igate[0]).astype(BF16)
    slopes = jnp.asarray(2.0 ** (-8.0 * np.arange(1, N_HEADS + 1) / N_HEADS), dtype=F32)
    lam_vecs = (lambda_q1, lambda_k1, lambda_q2, lambda_k2)
    ffn1 = (g_ffn1_pre, g_ffn1_post, g_mix_pre, wg1, wu1, wd1, wq, wkt, wv, wl)
    lru_params = (conv_w[0], conv_b, wr, b_rgate, wi, b_igate, lru_lambda)
    tail_pad = 8 - (CONV_W - 1)

    ffn2 = (woa, wob, g_mix_post, g_ffn2_pre, g_ffn2_post, wg2, wu2, wd2)

    def conv_state(conv_hist, lx, b, t):
        return jnp.concatenate([conv_hist, lx.reshape(b, t, D_LRU)], axis=1)[:, -(CONV_W - 1):][None]

    def pad_hist(conv_hist):
        return jnp.pad(conv_hist, ((0, 0), (tail_pad, 0), (0, 0)))

    hist_p = jnp.zeros((bp, CONV_W - 1, D_LRU), F32)
    x1, qb, kt, ktb, v_rows, vb, lx, lru_out, hp = _ffn_inproj(
        x_prompt.reshape(bp * tp, D_MODEL), *ffn1, seq_len=tp,
        lru_state=(jnp.zeros((bp, 1, D_LRU), F32), pad_hist(hist_p)), lru_params=lru_params)
    att = _attn_prompt(slopes, qb.reshape(bp, tp, D_ATT), ktb, vb.reshape(bp, tp, D_ATT), *lam_vecs, subln_g)
    yp = _outproj_ffn(att.reshape(bp * tp, D_ATT), lru_out, x1, *ffn2).reshape(bp, tp, D_MODEL)
    hp, cp = hp.reshape(1, bp, D_LRU), conv_state(hist_p, lx, bp, tp)
    kp = kt.reshape(bp, N_HEADS, 2, D_HEAD_QK, tp).transpose(0, 4, 1, 2, 3)[None]
    vp = v_rows.reshape(1, bp, tp, N_HEADS, D_HEAD_V)

    x1, qb, kf, kb, vf, vb, lx, lg = _ffn_inproj(x_sample.reshape(bs * ts, D_MODEL), *ffn1)
    cache_kt = cache_k[0].transpose(0, 2, 3, 4, 1).reshape(bs, D_ATT, past)
    cache_v_rows = cache_v[0].reshape(bs, past * N_HEADS, D_HEAD_V)
    att = _attn_sample(slopes, qb.reshape(bs, ts, D_ATT), cache_kt, cache_v_rows, kb.reshape(bs, ts, D_ATT),
                       vb.reshape(bs, ts, D_ATT), *lam_vecs, subln_g)
    lru_out, hs = _lru(lx.reshape(bs, ts, D_LRU), lg.reshape(bs, ts, D_LRU), state_lru_h[0].reshape(bs, 1, D_LRU),
                       pad_hist(state_conv[0]), *lru_params)
    ys = _outproj_ffn(att.reshape(bs * ts, D_ATT), lru_out.reshape(bs * ts, D_LRU), x1, *ffn2)
    ys, hs, cs = ys.reshape(bs, ts, D_MODEL), hs.reshape(1, bs, D_LRU), conv_state(state_conv[0], lx, bs, ts)
    ks = kf.reshape(1, bs, ts, N_HEADS, 2, D_HEAD_QK)
    vs = vf.reshape(1, bs, ts, N_HEADS, D_HEAD_V)
    return (yp, ys, kp, vp, hp, cp, ks, vs, hs, cs)
```

```python
import functools
import math

import jax
import jax.numpy as jnp
import numpy as np
from jax import lax
from jax.experimental import pallas as pl
from jax.experimental.pallas import tpu as pltpu

F32 = jnp.float32
BF16 = jnp.bfloat16

D_MODEL = 1024
D_ATT = 512
D_LRU = 512
N_HEADS = 4
D_HEAD_V = 128
D_HEAD_QK = 64
CONV_W = 4
LRU_C = 8.0
D_FF = 2816
CHUNK = 64
RMS_EPS = 1e-6
NEG_INF = -1e30
LAMBDA_INIT = 0.8 - 0.6 * math.exp(-0.3 * 0)
LOG2E = math.log2(math.e)

LANES = 128
SUBLANES = 8
FF_CHUNK = 256
N_FF_CHUNKS = D_FF // FF_CHUNK
ROW_TILE = 512
N_LATER_WEIGHTS = 4
OUT_ROW_TILE = 1024
ATTN_HEADS_PER_STEP = 2
LRU_TILE = 1024
CACHE_TILE = 4096
VMEM_LIMIT = 56 * 1024 * 1024


def _dot(a, b):
    return jnp.dot(a, b, preferred_element_type=F32)


def _dot_nt(a, b):
    return lax.dot_general(a, b, (((1,), (1,)), ((), ())), preferred_element_type=F32)


def _rms(x, g):
    return x * lax.rsqrt(jnp.mean(x * x, axis=-1, keepdims=True) + RMS_EPS) * g


def _swiglu_ffn(x, g_pre, g_post, wg_ref, wu_ref, wd_ref):
    xn = _rms(x, g_pre).astype(BF16)
    acc = jnp.zeros(x.shape, F32)
    for j in range(N_FF_CHUNKS):
        cols = slice(j * FF_CHUNK, (j + 1) * FF_CHUNK)
        g = _dot(xn, wg_ref[:, cols])
        u = _dot(xn, wu_ref[:, cols])
        h = (g * jax.nn.sigmoid(g) * u).astype(BF16)
        acc = acc + _dot(h, wd_ref[cols, :])
    return x + 0.5 * _rms(acc, g_post)


def _resident(shape):
    nd = len(shape)
    return pl.BlockSpec(shape, lambda i: (0,) * nd, pipeline_mode=pl.Buffered(1))


def _sigmoid(x):
    return 0.5 * (1.0 + jnp.tanh(0.5 * x))


def _lru_begin(is_first, h0, cbuf, tt, xbuf_ref, h_ref, *_):
    pad = xbuf_ref.shape[0] - tt

    @pl.when(is_first)
    def _():
        xbuf_ref[0:pad, :] = cbuf
        h_ref[...] = h0

    @pl.when(jnp.logical_not(is_first))
    def _():
        xbuf_ref[0:pad, :] = xbuf_ref[tt:tt + pad, :]


def _lru_tile(x, gate, cw, cb, wr, br, wi, bi, lam, store_out, xbuf_ref, h_ref, a_ref, b_ref, p_ref, hs_ref):
    tt = x.shape[0]
    pad = xbuf_ref.shape[0] - tt
    n_seg = min(SUBLANES, tt // SUBLANES)
    seg = tt // n_seg
    xbuf_ref[pad:pad + tt, :] = x
    xc = cb + xbuf_ref[pad:pad + tt, :] * cw[CONV_W - 1:CONV_W, :]
    for j in range(CONV_W - 1):
        back = CONV_W - 1 - j
        xc = xc + xbuf_ref[pad - back:pad - back + tt, :] * cw[j:j + 1, :]

    xb = xc.astype(BF16)
    r = _sigmoid(_dot(xb, wr) + br)
    i = _sigmoid(_dot(xb, wi) + bi)
    neg_lam = -lam
    softplus = jnp.maximum(neg_lam, 0.0) + jnp.log1p(jnp.exp(-jnp.abs(neg_lam)))
    log_a = -LRU_C * r * softplus
    a = jnp.exp(log_a)
    u = -jnp.tanh(log_a) * (1.0 + a * a)
    b = jnp.where(u > 0.0, u * lax.rsqrt(u), 0.0) * (i * xc)
    n_blk = D_LRU // LANES
    pitch = a_ref.shape[1] // n_seg
    for s in range(n_seg):
        for l in range(n_blk):
            a_ref[l, s * pitch:s * pitch + seg, :] = a[s * seg:(s + 1) * seg, l * LANES:(l + 1) * LANES]
            b_ref[l, s * pitch:s * pitch + seg, :] = b[s * seg:(s + 1) * seg, l * LANES:(l + 1) * LANES]

    h = [jnp.zeros((n_seg, LANES), F32)] * n_blk
    p = [jnp.ones((n_seg, LANES), F32)] * n_blk
    for j in range(seg):
        rows = pl.ds(j, n_seg, stride=pitch)
        for l in range(n_blk):
            a_j = a_ref[l, rows, :]
            h[l] = a_j * h[l] + b_ref[l, rows, :]
            p[l] = a_j * p[l]
            hs_ref[l, rows, :] = h[l]
            p_ref[l, rows, :] = p[l]

    h_end = jnp.concatenate(h, axis=1)
    p_end = jnp.concatenate(p, axis=1)
    carry = h_ref[...]
    for s in range(n_seg):
        rows = slice(s * pitch, s * pitch + seg)
        p_rows = jnp.concatenate([p_ref[l, rows, :] for l in range(n_blk)], axis=1)
        hs_rows = jnp.concatenate([hs_ref[l, rows, :] for l in range(n_blk)], axis=1)
        hs = p_rows * carry + hs_rows
        g = gate[s * seg:(s + 1) * seg, :]
        gelu = 0.5 * g * (1.0 + jnp.tanh(math.sqrt(2.0 / math.pi) * (g + 0.044715 * (g * g * g))))
        store_out(slice(s * seg, (s + 1) * seg), hs * gelu)
        carry = p_end[s:s + 1, :] * carry + h_end[s:s + 1, :]
    h_ref[...] = carry
    return carry


def _lru_scratch(tt):
    n_seg = min(SUBLANES, tt // SUBLANES)
    return ([pltpu.VMEM((tt + SUBLANES, D_LRU), F32), pltpu.VMEM((1, D_LRU), F32)]
            + [pltpu.VMEM((D_LRU // LANES, tt + 4 * n_seg, LANES), F32)] * 4)


def _lru_kernel(x_ref, gate_ref, h0_ref, cbuf_ref, cw_ref, cb_ref, wr_ref, br_ref, wi_ref, bi_ref, lam_ref,
                out_ref, hlast_ref, *scratch):
    def store_out(rows, value):
        out_ref[0, rows, :] = value.astype(out_ref.dtype)

    _lru_begin(pl.program_id(1) == 0, h0_ref[0], cbuf_ref[0], x_ref.shape[1], *scratch)
    hlast_ref[0] = _lru_tile(x_ref[0], gate_ref[0], cw_ref[...], cb_ref[...], wr_ref[...], br_ref[...],
                             wi_ref[...], bi_ref[...], lam_ref[...], store_out, *scratch)


def _lru(x, gate, h0, cbuf, cw, cb, wr, br, wi, bi, lam):
    b, t, _ = x.shape
    tt = min(LRU_TILE, t)
    small = lambda a: pl.BlockSpec(a.shape, lambda bi_, ti: (0, 0))
    tile = pl.BlockSpec((1, tt, D_LRU), lambda bi_, ti: (bi_, ti, 0))
    per_stream = lambda a: pl.BlockSpec((1,) + a.shape[1:], lambda bi_, ti: (bi_, 0, 0))
    return pl.pallas_call(
        _lru_kernel,
        grid=(b, t // tt),
        in_specs=[tile, tile, per_stream(h0), per_stream(cbuf), small(cw), small(cb),
                  small(wr), small(br), small(wi), small(bi), small(lam)],
        out_specs=(tile, pl.BlockSpec((1, 1, D_LRU), lambda bi_, ti: (bi_, 0, 0))),
        out_shape=(jax.ShapeDtypeStruct((b, t, D_LRU), BF16), jax.ShapeDtypeStruct((b, 1, D_LRU), F32)),
        scratch_shapes=_lru_scratch(tt),
        compiler_params=pltpu.CompilerParams(dimension_semantics=("arbitrary",) * 2, vmem_limit_bytes=VMEM_LIMIT),
        name="lru",
    )(x, gate, h0, cbuf, cw, cb, wr, br, wi, bi, lam)


def _ffn_inproj_kernel(x_ref, g1a_ref, g1b_ref, gma_ref, wg_ref, wu_ref, wd_ref, wq_ref, wkt_ref, wv_ref, wl_ref,
                       *refs, seq_tiles):
    tm = x_ref.shape[0]
    if seq_tiles is None:
        x1_ref, qb_ref, kf_ref, kb_ref, vf_ref, vb_ref, lx_ref, lg_ref = refs
    else:
        n_cast = N_LATER_WEIGHTS
        (h0_ref, cbuf_ref, cw_ref, cb_ref, wr_ref, br_ref, wi_ref, bi_ref, lam_ref, *rest) = refs
        cast_in, rest = rest[:n_cast], rest[n_cast:]
        (x1_ref, qb_ref, kf_ref, kb_ref, vf_ref, vb_ref, lx_ref, lo_ref, hlast_ref, *rest) = rest
        cast_out, scratch = rest[:n_cast], rest[n_cast:]
        for src, dst in zip(cast_in, cast_out):
            dst[...] = src[...].astype(BF16)
        _lru_begin(pl.program_id(0) % seq_tiles == 0, h0_ref[0], cbuf_ref[0], tm, *scratch)
    x1 = _swiglu_ffn(x_ref[...], g1a_ref[...], g1b_ref[...], wg_ref, wu_ref, wd_ref)
    x1_ref[...] = x1
    xm = _rms(x1, gma_ref[...]).astype(BF16)
    lru = _dot(xm, wl_ref[...])
    lx_ref[...] = lru[:, :D_LRU]
    qb_ref[...] = (_dot(xm, wq_ref[...]) * (LOG2E / math.sqrt(D_HEAD_QK))).astype(BF16)
    v = _dot(xm, wv_ref[...])
    vb_ref[...] = v.astype(BF16)
    if seq_tiles is None:
        lg_ref[...] = lru[:, D_LRU:]
        k = _dot_nt(xm, wkt_ref[...])
        kf_ref[...] = k
        kb_ref[...] = k.astype(BF16)
        vf_ref[...] = v
    else:
        kt = _dot_nt(wkt_ref[...], xm)
        kf_ref[0] = kt
        kb_ref[0, 0] = kt.astype(BF16)
        for h in range(N_HEADS):
            vf_ref[pl.ds(h, tm, stride=N_HEADS), :] = v[:, h * D_HEAD_V:(h + 1) * D_HEAD_V]

        def store_out(rows, value):
            lo_ref[rows, :] = value.astype(lo_ref.dtype)

        hlast_ref[0] = _lru_tile(lru[:, :D_LRU], lru[:, D_LRU:], cw_ref[...], cb_ref[...], wr_ref[...], br_ref[...],
                                 wi_ref[...], bi_ref[...], lam_ref[...], store_out, *scratch)


def _ffn_inproj(x, g1a, g1b, gma, wg, wu, wd, wq, wkt, wv, wl, seq_len=None, lru_state=None, lru_params=None,
                later_weights=()):
    m = x.shape[0]
    tm = min(ROW_TILE, m)
    rows = lambda d: pl.BlockSpec((tm, d), lambda i: (i, 0))
    row_out = lambda d, dt: jax.ShapeDtypeStruct((m, d), dt)
    weights = (g1a, g1b, gma, wg, wu, wd, wq, wkt, wv, wl)
    in_specs = [rows(D_MODEL)] + [_resident(w.shape) for w in weights]
    operands = (x, *weights)
    scratch = []
    if seq_len is not None:
        nt = seq_len // tm
        b = m // seq_len
        k_shapes = (jax.ShapeDtypeStruct((b, D_ATT, seq_len), F32), jax.ShapeDtypeStruct((b, nt, D_ATT, tm), BF16))
        k_specs = (pl.BlockSpec((1, D_ATT, tm), lambda i: (i // nt, 0, i % nt)),
                   pl.BlockSpec((1, 1, D_ATT, tm), lambda i: (i // nt, i % nt, 0, 0)))
        vf_shape = jax.ShapeDtypeStruct((m * N_HEADS, D_HEAD_V), F32)
        vf_spec = pl.BlockSpec((tm * N_HEADS, D_HEAD_V), lambda i: (i, 0))
        per_seq = lambda a: pl.BlockSpec((1,) + a.shape[1:], lambda i: (i // nt, 0, 0))
        tail_shapes = (row_out(D_LRU, F32), row_out(D_LRU, BF16), jax.ShapeDtypeStruct((b, 1, D_LRU), F32))
        tail_specs = (rows(D_LRU), rows(D_LRU), pl.BlockSpec((1, 1, D_LRU), lambda i: (i // nt, 0, 0)))
        assert len(later_weights) == N_LATER_WEIGHTS
        n_steps = m // tm
        cast_specs, cast_shapes = [], []
        for w in later_weights:
            r = next(r for r in range(2 * SUBLANES, w.shape[0] + 1, 2 * SUBLANES)
                     if w.shape[0] % r == 0 and w.shape[0] // r <= n_steps)
            last = w.shape[0] // r - 1
            cast_specs.append(pl.BlockSpec((r, w.shape[1]), lambda i, last=last: (jnp.minimum(i, last), 0)))
            cast_shapes.append(jax.ShapeDtypeStruct(w.shape, BF16))
        in_specs += [per_seq(a) for a in lru_state] + [_resident(p.shape) for p in lru_params] + cast_specs
        operands += (*lru_state, *lru_params, *later_weights)
        tail_shapes += tuple(cast_shapes)
        tail_specs += tuple(cast_specs)
        scratch = _lru_scratch(tm)
        seq_tiles = nt
    else:
        k_shapes = (row_out(D_ATT, F32), row_out(D_ATT, BF16))
        k_specs = (rows(D_ATT), rows(D_ATT))
        vf_shape = row_out(D_ATT, F32)
        vf_spec = rows(D_ATT)
        tail_shapes = (row_out(D_LRU, F32), row_out(D_LRU, F32))
        tail_specs = (rows(D_LRU), rows(D_LRU))
        seq_tiles = None
    out_shape = (row_out(D_MODEL, F32), row_out(D_ATT, BF16), *k_shapes, vf_shape, row_out(D_ATT, BF16), *tail_shapes)
    out_specs = (rows(D_MODEL), rows(D_ATT), *k_specs, vf_spec, rows(D_ATT), *tail_specs)
    return pl.pallas_call(
        functools.partial(_ffn_inproj_kernel, seq_tiles=seq_tiles),
        grid=(m // tm,),
        in_specs=in_specs,
        out_specs=out_specs,
        out_shape=out_shape,
        scratch_shapes=scratch,
        compiler_params=pltpu.CompilerParams(dimension_semantics=("arbitrary",), vmem_limit_bytes=VMEM_LIMIT),
        name="ffn_inproj",
    )(*operands)


def _outproj_ffn_kernel(att_ref, lru_ref, x1_ref, wo_ref, gmb_ref, g2a_ref, g2b_ref,
                        wg_ref, wu_ref, wd_ref, y_ref):
    mix = _dot(att_ref[...], wo_ref[0:D_ATT, :]) + _dot(lru_ref[...], wo_ref[D_ATT:D_ATT + D_LRU, :])
    x2 = x1_ref[...] + _rms(mix, gmb_ref[...])
    y_ref[...] = _swiglu_ffn(x2, g2a_ref[...], g2b_ref[...], wg_ref, wu_ref, wd_ref)


def _outproj_ffn(att, lru, x1, wo, gmb, g2a, g2b, wg, wu, wd):
    m = x1.shape[0]
    tm = min(OUT_ROW_TILE, m)
    rows = lambda d: pl.BlockSpec((tm, d), lambda i: (i, 0))
    weights = (wo, gmb, g2a, g2b, wg, wu, wd)
    return pl.pallas_call(
        _outproj_ffn_kernel,
        grid=(m // tm,),
        in_specs=[rows(D_ATT), rows(D_LRU), rows(D_MODEL)] + [_resident(w.shape) for w in weights],
        out_specs=rows(D_MODEL),
        out_shape=jax.ShapeDtypeStruct((m, D_MODEL), F32),
        compiler_params=pltpu.CompilerParams(dimension_semantics=("arbitrary",), vmem_limit_bytes=VMEM_LIMIT),
        name="outproj_ffn",
    )(att, lru, x1, *weights)


def _split_maps(q):
    lane = lax.broadcasted_iota(jnp.int32, q.shape, 1)
    zero = jnp.zeros_like(q)
    return jnp.where(lane < D_HEAD_QK, q, zero), jnp.where(lane >= D_HEAD_QK, q, zero)


def _init_softmax_state(m_ref, l_ref, acc_ref):
    m_ref[...] = jnp.full(m_ref.shape, NEG_INF, F32)
    l_ref[...] = jnp.zeros(l_ref.shape, F32)
    acc_ref[...] = jnp.zeros(acc_ref.shape, F32)


def _bf16_split3(x):
    hi = x.astype(BF16)
    r1 = x - hi.astype(F32)
    lo = r1.astype(BF16)
    lo2 = (r1 - lo.astype(F32)).astype(BF16)
    return hi, lo, lo2


def _attn_prompt_kernel(slopes_ref, q_ref, kt_ref, v_ref, lq1, lk1, lq2, lk2, g_ref, o_ref,
                        kta_ref, va_ref, mask_ref, m_ref, acc_ref, pa_ref, pb_ref, *, tq, nh):
    hg = pl.program_id(1)
    qi = pl.program_id(2)
    n_kt = kt_ref.shape[1]
    half = D_HEAD_QK
    slopes = [slopes_ref[hg * nh + g] * LOG2E for g in range(nh)]
    n_slots = 2 * nh

    @pl.when(qi == 0)
    def _():
        row = lax.broadcasted_iota(jnp.int32, (LANES, tq), 0)
        col = lax.broadcasted_iota(jnp.int32, (1, tq), 1)
        qrow = lax.broadcasted_iota(jnp.int32, (tq, tq), 0)
        kcol = lax.broadcasted_iota(jnp.int32, (tq, tq), 1)
        zero = jnp.zeros((LANES, tq), F32)
        for g in range(nh):
            terms = _bf16_split3(slopes[g] * col.astype(F32))

            def bias_rows(first):
                blk = zero
                for i, t in enumerate(terms):
                    blk = jnp.where(row == first + i, jnp.broadcast_to(t.astype(F32), (LANES, tq)), blk)
                return blk

            bias0, bias1 = bias_rows(half), bias_rows(0)

            def fill(n, carry):
                kt = kt_ref[0, n, g * LANES:(g + 1) * LANES, :].astype(F32)
                kta_ref[2 * g, n] = jnp.where(row < half, kt, bias0).astype(BF16)
                kta_ref[2 * g + 1, n] = jnp.where(row >= half, kt, bias1).astype(BF16)
                r0 = pl.multiple_of(n * tq, tq)
                va_ref[g, pl.ds(r0, tq), 0:D_HEAD_V] = v_ref[0, pl.ds(r0, tq), g * LANES:(g + 1) * LANES]
                va_ref[g, pl.ds(r0, tq), D_HEAD_V:2 * D_HEAD_V] = jnp.ones((tq, D_HEAD_V), BF16)
                return carry

            lax.fori_loop(0, n_kt, fill, 0)
            mask_ref[g] = jnp.where((qrow // CHUNK) >= (kcol // CHUNK),
                                    slopes[g] * jnp.minimum(2 * (qrow - kcol), 0).astype(F32), NEG_INF)
        acc_ref[...] = jnp.zeros(acc_ref.shape, F32)
        pb_ref[...] = jnp.zeros(pb_ref.shape, BF16)

    lane = lax.broadcasted_iota(jnp.int32, (tq, LANES), 1)
    ones0 = ((lane >= half) & (lane < half + 3)).astype(F32).astype(BF16)
    ones1 = (lane < 3).astype(F32).astype(BF16)
    qa = []
    for g in range(nh):
        q = q_ref[0, :, g * LANES:(g + 1) * LANES]
        qa += [jnp.where(lane < half, q, ones0), jnp.where(lane >= half, q, ones1)]
    m_ref[...] = jnp.full(m_ref.shape, NEG_INF, F32)

    def values(g, j):
        return va_ref[g, pl.ds(pl.multiple_of(j * tq, tq), tq), :]

    def softmax_tile(s, m_prev, tile_bias):
        m_new = jnp.maximum(m_prev, jnp.max(s, axis=1, keepdims=True) + tile_bias)
        shift = m_new - tile_bias
        p = jnp.concatenate([jnp.exp2((s[:, b * LANES:(b + 1) * LANES] - shift).astype(BF16))
                             for b in range(s.shape[1] // LANES)], axis=1)
        return m_new, p

    def update(j, p_in_ref, p_out_ref, diagonal=False):
        logits = [_dot(qa[k], kta_ref[k, j]) for k in range(n_slots)]
        if diagonal:
            logits = [logits[k] + mask_ref[k // 2] for k in range(n_slots)]
        j_prev = jnp.maximum(j - 1, 0)
        pv = [_dot(p_in_ref[k], values(k // 2, j_prev)) for k in range(n_slots)]
        m_prev = [m_ref[k] for k in range(n_slots)]
        acc_prev = [acc_ref[k] for k in range(n_slots)]
        m_next, acc_next, p_next = [], [], []
        for k in range(n_slots):
            tile_bias = slopes[k // 2] * ((j - qi) * tq).astype(F32)
            m_new, p = softmax_tile(logits[k], m_prev[k], tile_bias)
            alpha = jnp.exp2(m_prev[k] - m_new)
            acc = jnp.concatenate([alpha, alpha], axis=1) * (acc_prev[k] + pv[k])
            if diagonal:
                acc = acc + _dot(p, values(k // 2, j))
            acc_next.append(acc)
            m_next.append(m_new)
            p_next.append(p)
        for k in range(n_slots):
            acc_ref[k] = acc_next[k]
            m_ref[k] = m_next[k]
            if not diagonal:
                p_out_ref[k] = p_next[k]

    odd = qi % 2

    @pl.when(odd == 1)
    def _():
        for k in range(n_slots):
            tile_bias = slopes[k // 2] * (-qi * tq).astype(F32)
            m_new, p = softmax_tile(_dot(qa[k], kta_ref[k, 0]), m_ref[k], tile_bias)
            m_ref[k] = m_new
            pb_ref[k] = p
        acc_ref[...] = jnp.zeros(acc_ref.shape, F32)

    def pair(i, carry):
        j = odd + 2 * i
        update(j, pb_ref, pa_ref)

        @pl.when(j + 1 < qi)
        def _():
            update(j + 1, pa_ref, pb_ref)

        return carry

    lax.fori_loop(0, qi // 2, pair, 0)
    update(qi, pb_ref, None, diagonal=True)

    lam = (jnp.exp(jnp.sum(lq1[...] * lk1[...], axis=1, keepdims=True))
           - jnp.exp(jnp.sum(lq2[...] * lk2[...], axis=1, keepdims=True)) + LAMBDA_INIT)
    for g in range(nh):
        a0, a1 = acc_ref[2 * g], acc_ref[2 * g + 1]
        o = a0[:, :D_HEAD_V] / a0[:, D_HEAD_V:] - lam * (a1[:, :D_HEAD_V] / a1[:, D_HEAD_V:])
        o_ref[0, :, g * LANES:(g + 1) * LANES] = (_rms(o, g_ref[...]) * (1.0 - LAMBDA_INIT)).astype(o_ref.dtype)


def _attn_prompt(slopes, q, kt, v, lq1, lk1, lq2, lk2, g):
    b, t, _ = q.shape
    n_kt, tq = kt.shape[1], kt.shape[3]
    nh = ATTN_HEADS_PER_STEP
    w = nh * LANES
    small = lambda a: pl.BlockSpec(a.shape, lambda bi, hi, qi: (0, 0))
    return pl.pallas_call(
        functools.partial(_attn_prompt_kernel, tq=tq, nh=nh),
        grid=(b, N_HEADS // nh, t // tq),
        in_specs=[pl.BlockSpec(memory_space=pltpu.SMEM),
                  pl.BlockSpec((1, tq, w), lambda bi, hi, qi: (bi, qi, hi)),
                  pl.BlockSpec((1, n_kt, w, tq), lambda bi, hi, qi: (bi, 0, hi, 0)),
                  pl.BlockSpec((1, t, w), lambda bi, hi, qi: (bi, 0, hi)),
                  small(lq1), small(lk1), small(lq2), small(lk2), small(g)],
        out_specs=pl.BlockSpec((1, tq, w), lambda bi, hi, qi: (bi, qi, hi)),
        out_shape=jax.ShapeDtypeStruct((b, t, D_ATT), BF16),
        scratch_shapes=[pltpu.VMEM((2 * nh, n_kt, LANES, tq), BF16), pltpu.VMEM((nh, t, 2 * D_HEAD_V), BF16),
                        pltpu.VMEM((nh, tq, tq), F32), pltpu.VMEM((2 * nh, tq, LANES), F32),
                        pltpu.VMEM((2 * nh, tq, 2 * D_HEAD_V), F32)] + [pltpu.VMEM((2 * nh, tq, tq), BF16)] * 2,
        compiler_params=pltpu.CompilerParams(dimension_semantics=("arbitrary",) * 3, vmem_limit_bytes=VMEM_LIMIT),
        name="attn_prompt",
    )(slopes, q, kt, v, lq1, lk1, lq2, lk2, g)


def _attn_sample_kernel(slopes_ref, q_ref, ckt_ref, cv_ref, kn_ref, vn_ref, lq1, lk1, lq2, lk2, g_ref, o_ref,
                        m_ref, l_ref, acc_ref, *, past, tk):
    j = pl.program_id(1)
    tq = q_ref.shape[1]

    @pl.when(j == 0)
    def _():
        _init_softmax_state(m_ref, l_ref, acc_ref)

    def attend(n, k_start, n_valid, scores, values):
        q_pos = past + lax.broadcasted_iota(jnp.int32, (tq, n), 0)
        k_pos = k_start + lax.broadcasted_iota(jnp.int32, (tq, n), 1)
        dist = jnp.abs(q_pos - k_pos).astype(F32)
        dist = jnp.concatenate([dist, dist], axis=0)
        visible = ((q_pos // CHUNK) >= (k_pos // CHUNK)) & (k_pos < k_start + n_valid)
        visible = jnp.concatenate([visible, visible], axis=0)
        state = [(m_ref[h], l_ref[h], acc_ref[h]) for h in range(N_HEADS)]
        for h in range(N_HEADS):
            m_prev, l_prev, acc_prev = state[h]
            q2 = jnp.concatenate(_split_maps(q_ref[0, :, h * LANES:(h + 1) * LANES]), axis=0)
            s = jnp.where(visible, scores(q2, h) - (slopes_ref[h] * LOG2E) * dist, NEG_INF)
            m_new = jnp.maximum(m_prev, jnp.max(s, axis=1, keepdims=True))
            alpha = jnp.exp2(m_prev - m_new)
            ps = [jnp.exp2(s[:, b * LANES:(b + 1) * LANES] - m_new) for b in range(n // LANES)]
            p = jnp.concatenate(ps, axis=1).astype(BF16)
            state[h] = (m_new, alpha * l_prev + functools.reduce(lambda x, y: x + y, ps),
                        alpha * acc_prev + _dot(p, values(h)))
        for h in range(N_HEADS):
            m_ref[h], l_ref[h], acc_ref[h] = state[h]

    attend(tk, j * tk, tk,
           lambda q2, h: _dot(q2, ckt_ref[0, h * LANES:(h + 1) * LANES, :].astype(BF16)),
           lambda h: cv_ref[0, pl.ds(h, tk, stride=N_HEADS), :].astype(BF16))

    @pl.when(j == pl.num_programs(1) - 1)
    def _():
        attend(kn_ref.shape[1], past, tq,
               lambda q2, h: _dot_nt(q2, kn_ref[0, :, h * LANES:(h + 1) * LANES]),
               lambda h: vn_ref[0, :, h * LANES:(h + 1) * LANES])
        lam = (jnp.exp(jnp.sum(lq1[...] * lk1[...], axis=1, keepdims=True))
               - jnp.exp(jnp.sum(lq2[...] * lk2[...], axis=1, keepdims=True)) + LAMBDA_INIT)
        for h in range(N_HEADS):
            sm = acc_ref[h] / jnp.sum(l_ref[h], axis=1, keepdims=True)
            o = _rms(sm[:tq] - lam * sm[tq:], g_ref[...]) * (1.0 - LAMBDA_INIT)
            o_ref[0, :, h * LANES:(h + 1) * LANES] = o.astype(o_ref.dtype)


def _attn_sample(slopes, q, cache_kt, cache_v_rows, k_new, v_new, lq1, lk1, lq2, lk2, g):
    b, t, _ = q.shape
    past = cache_kt.shape[2]
    tk = min(CACHE_TILE, past)
    k_new = jnp.pad(k_new, ((0, 0), (0, -t % LANES), (0, 0)))
    v_new = jnp.pad(v_new, ((0, 0), (0, -t % LANES), (0, 0)))
    small = lambda a: pl.BlockSpec(a.shape, lambda bi, j: (0, 0))
    per_stream = lambda a: pl.BlockSpec((1,) + a.shape[1:], lambda bi, j: (bi, 0, 0))
    return pl.pallas_call(
        functools.partial(_attn_sample_kernel, past=past, tk=tk),
        grid=(b, past // tk),
        in_specs=[pl.BlockSpec(memory_space=pltpu.SMEM),
                  per_stream(q),
                  pl.BlockSpec((1, D_ATT, tk), lambda bi, j: (bi, 0, j)),
                  pl.BlockSpec((1, tk * N_HEADS, D_HEAD_V), lambda bi, j: (bi, j, 0)),
                  per_stream(k_new), per_stream(v_new),
                  small(lq1), small(lk1), small(lq2), small(lk2), small(g)],
        out_specs=pl.BlockSpec((1, t, D_ATT), lambda bi, j: (bi, 0, 0)),
        out_shape=jax.ShapeDtypeStruct((b, t, D_ATT), BF16),
        scratch_shapes=[pltpu.VMEM((N_HEADS, 2 * t, LANES), F32)] * 3,
        compiler_params=pltpu.CompilerParams(dimension_semantics=("arbitrary",) * 2, vmem_limit_bytes=VMEM_LIMIT),
        name="attn_sample",
    )(slopes, q, cache_kt, cache_v_rows, k_new, v_new, lq1, lk1, lq2, lk2, g)


def _block_diag_dense(w):
    n, c, _ = w.shape
    eye = jnp.eye(n, dtype=w.dtype)
    return (eye[:, None, :, None] * w[:, :, None, :]).reshape(n * c, n * c)


def kernel(x_prompt, x_sample, cache_k, cache_v, state_lru_h, state_conv, w_in, w_out, lambda_q1, lambda_k1,
           lambda_q2, lambda_k2, subln_g, conv_w, conv_b, w_rgate, b_rgate, w_igate, b_igate, lru_lambda,
           ffn1_w_gate, ffn1_w_up, ffn1_w_down, ffn2_w_gate, ffn2_w_up, ffn2_w_down,
           g_ffn1_pre, g_ffn1_post, g_mix_pre, g_mix_post, g_ffn2_pre, g_ffn2_post):
    bp, tp, _ = x_prompt.shape
    bs, ts, _ = x_sample.shape
    past = cache_k.shape[2]
    assert w_in.shape[0] == 1, "one layer: LAMBDA_INIT is the depth-0 value"
    assert tp % ROW_TILE == 0 and (bp * tp) % OUT_ROW_TILE == 0
    assert bs * ts <= ROW_TILE and ts % (2 * SUBLANES) == 0
    assert past % min(CACHE_TILE, past) == 0

    wg1, wu1, wd1 = ffn1_w_gate[0].astype(BF16), ffn1_w_up[0].astype(BF16), ffn1_w_down[0].astype(BF16)
    win = w_in[0].astype(BF16)
    wq, wkt, wv, wl = win[:, :D_ATT], win[:, D_ATT:2 * D_ATT].T, win[:, 2 * D_ATT:3 * D_ATT], win[:, 3 * D_ATT:]
    wr = _block_diag_dense(w_rgate[0]).astype(BF16)
    wi = _block_diag_dense(w_igate[0]).astype(BF16)
    slopes = jnp.asarray(2.0 ** (-8.0 * np.arange(1, N_HEADS + 1) / N_HEADS), dtype=F32)
    lam_vecs = (lambda_q1, lambda_k1, lambda_q2, lambda_k2)
    ffn1 = (g_ffn1_pre, g_ffn1_post, g_mix_pre, wg1, wu1, wd1, wq, wkt, wv, wl)
    lru_params = (conv_w[0], conv_b, wr, b_rgate, wi, b_igate, lru_lambda)
    tail_pad = 8 - (CONV_W - 1)

    def conv_state(conv_hist, lx, b, t):
        return jnp.concatenate([conv_hist, lx.reshape(b, t, D_LRU)], axis=1)[:, -(CONV_W - 1):][None]

    def pad_hist(conv_hist):
        return jnp.pad(conv_hist, ((0, 0), (tail_pad, 0), (0, 0)))

    hist_p = jnp.zeros((bp, CONV_W - 1, D_LRU), F32)
    x1, qb, kt, ktb, v_rows, vb, lx, lru_out, hp, wg2, wu2, wd2, wo = _ffn_inproj(
        x_prompt.reshape(bp * tp, D_MODEL), *ffn1, seq_len=tp,
        lru_state=(jnp.zeros((bp, 1, D_LRU), F32), pad_hist(hist_p)), lru_params=lru_params,
        later_weights=(ffn2_w_gate[0], ffn2_w_up[0], ffn2_w_down[0], w_out[0]))
    ffn2 = (wo, g_mix_post, g_ffn2_pre, g_ffn2_post, wg2, wu2, wd2)
    att = _attn_prompt(slopes, qb.reshape(bp, tp, D_ATT), ktb, vb.reshape(bp, tp, D_ATT), *lam_vecs, subln_g)
    yp = _outproj_ffn(att.reshape(bp * tp, D_ATT), lru_out, x1, *ffn2).reshape(bp, tp, D_MODEL)
    hp, cp = hp.reshape(1, bp, D_LRU), conv_state(hist_p, lx, bp, tp)
    kp = kt.reshape(bp, N_HEADS, 2, D_HEAD_QK, tp).transpose(0, 4, 1, 2, 3)[None]
    vp = v_rows.reshape(1, bp, tp, N_HEADS, D_HEAD_V)

    x1, qb, kf, kb, vf, vb, lx, lg = _ffn_inproj(x_sample.reshape(bs * ts, D_MODEL), *ffn1)
    cache_kt = cache_k[0].transpose(0, 2, 3, 4, 1).reshape(bs, D_ATT, past)
    cache_v_rows = cache_v[0].reshape(bs, past * N_HEADS, D_HEAD_V)
    att = _attn_sample(slopes, qb.reshape(bs, ts, D_ATT), cache_kt, cache_v_rows, kb.reshape(bs, ts, D_ATT),
                       vb.reshape(bs, ts, D_ATT), *lam_vecs, subln_g)
    lru_out, hs = _lru(lx.reshape(bs, ts, D_LRU), lg.reshape(bs, ts, D_LRU), state_lru_h[0].reshape(bs, 1, D_LRU),
                       pad_hist(state_conv[0]), *lru_params)
    ys = _outproj_ffn(att.reshape(bs * ts, D_ATT), lru_out.reshape(bs * ts, D_LRU), x1, *ffn2)
    ys, hs, cs = ys.reshape(bs, ts, D_MODEL), hs.reshape(1, bs, D_LRU), conv_state(state_conv[0], lx, bs, ts)
    ks = kf.reshape(1, bs, ts, N_HEADS, 2, D_HEAD_QK)
    vs = vf.reshape(1, bs, ts, N_HEADS, D_HEAD_V)
    return (yp, ys, kp, vp, hp, cp, ks, vs, hs, cs)
```

```python
import functools
import math

import jax
import jax.numpy as jnp
import numpy as np
from jax import lax
from jax.experimental import pallas as pl
from jax.experimental.pallas import tpu as pltpu

F32 = jnp.float32
BF16 = jnp.bfloat16

D_MODEL = 1024
D_ATT = 512
D_LRU = 512
N_HEADS = 4
D_HEAD_V = 128
D_HEAD_QK = 64
CONV_W = 4
LRU_C = 8.0
D_FF = 2816
CHUNK = 64
RMS_EPS = 1e-6
NEG_INF = -1e30
LAMBDA_INIT = 0.8 - 0.6 * math.exp(-0.3 * 0)
LOG2E = math.log2(math.e)

LANES = 128
SUBLANES = 8
FF_CHUNK = 256
N_FF_CHUNKS = D_FF // FF_CHUNK
ROW_TILE = 512
N_LATER_WEIGHTS = 4
OUT_ROW_TILE = 1024
ATTN_TILES_PER_STEP = 4
ATTN_HEADS_PER_STEP = 2
LRU_TILE = 1024
CACHE_TILE = 4096
VMEM_LIMIT = 56 * 1024 * 1024


def _dot(a, b):
    return jnp.dot(a, b, preferred_element_type=F32)


def _dot_nt(a, b):
    return lax.dot_general(a, b, (((1,), (1,)), ((), ())), preferred_element_type=F32)


def _rms(x, g):
    return x * lax.rsqrt(jnp.mean(x * x, axis=-1, keepdims=True) + RMS_EPS) * g


def _swiglu_ffn(x, g_pre, g_post, wg_ref, wu_ref, wd_ref):
    xn = _rms(x, g_pre).astype(BF16)
    acc = jnp.zeros(x.shape, F32)
    for j in range(N_FF_CHUNKS):
        cols = slice(j * FF_CHUNK, (j + 1) * FF_CHUNK)
        g = _dot(xn, wg_ref[:, cols])
        u = _dot(xn, wu_ref[:, cols])
        h = (g * jax.nn.sigmoid(g) * u).astype(BF16)
        acc = acc + _dot(h, wd_ref[cols, :])
    return x + 0.5 * _rms(acc, g_post)


def _resident(shape):
    nd = len(shape)
    return pl.BlockSpec(shape, lambda i: (0,) * nd, pipeline_mode=pl.Buffered(1))


def _sigmoid(x):
    return 0.5 * (1.0 + jnp.tanh(0.5 * x))


def _lru_begin(is_first, h0, cbuf, tt, xbuf_ref, h_ref, *_):
    pad = xbuf_ref.shape[0] - tt

    @pl.when(is_first)
    def _():
        xbuf_ref[0:pad, :] = cbuf
        h_ref[...] = h0

    @pl.when(jnp.logical_not(is_first))
    def _():
        xbuf_ref[0:pad, :] = xbuf_ref[tt:tt + pad, :]


def _lru_tile(x, gate, cw, cb, wr, br, wi, bi, lam, store_out, xbuf_ref, h_ref, a_ref, b_ref, p_ref, hs_ref):
    tt = x.shape[0]
    pad = xbuf_ref.shape[0] - tt
    n_seg = min(SUBLANES, tt // SUBLANES)
    seg = tt // n_seg
    xbuf_ref[pad:pad + tt, :] = x
    xc = cb + xbuf_ref[pad:pad + tt, :] * cw[CONV_W - 1:CONV_W, :]
    for j in range(CONV_W - 1):
        back = CONV_W - 1 - j
        xc = xc + xbuf_ref[pad - back:pad - back + tt, :] * cw[j:j + 1, :]

    xb = xc.astype(BF16)
    r = _sigmoid(_dot(xb, wr) + br)
    i = _sigmoid(_dot(xb, wi) + bi)
    neg_lam = -lam
    softplus = jnp.maximum(neg_lam, 0.0) + jnp.log1p(jnp.exp(-jnp.abs(neg_lam)))
    log_a = -LRU_C * r * softplus
    a = jnp.exp(log_a)
    u = -jnp.tanh(log_a) * (1.0 + a * a)
    b = jnp.where(u > 0.0, u * lax.rsqrt(u), 0.0) * (i * xc)
    n_blk = D_LRU // LANES
    pitch = a_ref.shape[1] // n_seg
    for s in range(n_seg):
        for l in range(n_blk):
            a_ref[l, s * pitch:s * pitch + seg, :] = a[s * seg:(s + 1) * seg, l * LANES:(l + 1) * LANES]
            b_ref[l, s * pitch:s * pitch + seg, :] = b[s * seg:(s + 1) * seg, l * LANES:(l + 1) * LANES]

    h = [jnp.zeros((n_seg, LANES), F32)] * n_blk
    p = [jnp.ones((n_seg, LANES), F32)] * n_blk
    for j in range(seg):
        rows = pl.ds(j, n_seg, stride=pitch)
        for l in range(n_blk):
            a_j = a_ref[l, rows, :]
            h[l] = a_j * h[l] + b_ref[l, rows, :]
            p[l] = a_j * p[l]
            hs_ref[l, rows, :] = h[l]
            p_ref[l, rows, :] = p[l]

    h_end = jnp.concatenate(h, axis=1)
    p_end = jnp.concatenate(p, axis=1)
    carry = h_ref[...]
    for s in range(n_seg):
        rows = slice(s * pitch, s * pitch + seg)
        p_rows = jnp.concatenate([p_ref[l, rows, :] for l in range(n_blk)], axis=1)
        hs_rows = jnp.concatenate([hs_ref[l, rows, :] for l in range(n_blk)], axis=1)
        hs = p_rows * carry + hs_rows
        g = gate[s * seg:(s + 1) * seg, :]
        gelu = 0.5 * g * (1.0 + jnp.tanh(math.sqrt(2.0 / math.pi) * (g + 0.044715 * (g * g * g))))
        store_out(slice(s * seg, (s + 1) * seg), hs * gelu)
        carry = p_end[s:s + 1, :] * carry + h_end[s:s + 1, :]
    h_ref[...] = carry
    return carry


def _lru_scratch(tt):
    n_seg = min(SUBLANES, tt // SUBLANES)
    return ([pltpu.VMEM((tt + SUBLANES, D_LRU), F32), pltpu.VMEM((1, D_LRU), F32)]
            + [pltpu.VMEM((D_LRU // LANES, tt + 4 * n_seg, LANES), F32)] * 4)


def _lru_kernel(x_ref, gate_ref, h0_ref, cbuf_ref, cw_ref, cb_ref, wr_ref, br_ref, wi_ref, bi_ref, lam_ref,
                out_ref, hlast_ref, *scratch):
    def store_out(rows, value):
        out_ref[0, rows, :] = value.astype(out_ref.dtype)

    _lru_begin(pl.program_id(1) == 0, h0_ref[0], cbuf_ref[0], x_ref.shape[1], *scratch)
    hlast_ref[0] = _lru_tile(x_ref[0], gate_ref[0], cw_ref[...], cb_ref[...], wr_ref[...], br_ref[...],
                             wi_ref[...], bi_ref[...], lam_ref[...], store_out, *scratch)


def _lru(x, gate, h0, cbuf, cw, cb, wr, br, wi, bi, lam):
    b, t, _ = x.shape
    tt = min(LRU_TILE, t)
    small = lambda a: pl.BlockSpec(a.shape, lambda bi_, ti: (0, 0))
    tile = pl.BlockSpec((1, tt, D_LRU), lambda bi_, ti: (bi_, ti, 0))
    per_stream = lambda a: pl.BlockSpec((1,) + a.shape[1:], lambda bi_, ti: (bi_, 0, 0))
    return pl.pallas_call(
        _lru_kernel,
        grid=(b, t // tt),
        in_specs=[tile, tile, per_stream(h0), per_stream(cbuf), small(cw), small(cb),
                  small(wr), small(br), small(wi), small(bi), small(lam)],
        out_specs=(tile, pl.BlockSpec((1, 1, D_LRU), lambda bi_, ti: (bi_, 0, 0))),
        out_shape=(jax.ShapeDtypeStruct((b, t, D_LRU), BF16), jax.ShapeDtypeStruct((b, 1, D_LRU), F32)),
        scratch_shapes=_lru_scratch(tt),
        compiler_params=pltpu.CompilerParams(dimension_semantics=("arbitrary",) * 2, vmem_limit_bytes=VMEM_LIMIT),
        name="lru",
    )(x, gate, h0, cbuf, cw, cb, wr, br, wi, bi, lam)


def _ffn_inproj_kernel(x_ref, g1a_ref, g1b_ref, gma_ref, wg_ref, wu_ref, wd_ref, wq_ref, wkt_ref, wv_ref, wl_ref,
                       *refs, seq_tiles):
    tm = x_ref.shape[0]
    if seq_tiles is None:
        x1_ref, qb_ref, kf_ref, kb_ref, vf_ref, vb_ref, lx_ref, lg_ref = refs
    else:
        n_cast = N_LATER_WEIGHTS
        (h0_ref, cbuf_ref, cw_ref, cb_ref, wr_ref, br_ref, wi_ref, bi_ref, lam_ref, *rest) = refs
        cast_in, rest = rest[:n_cast], rest[n_cast:]
        (x1_ref, qb_ref, kf_ref, kb_ref, vf_ref, vb_ref, lx_ref, lo_ref, hlast_ref, *rest) = rest
        cast_out, scratch = rest[:n_cast], rest[n_cast:]
        for src, dst in zip(cast_in, cast_out):
            dst[...] = src[...].astype(BF16)
        _lru_begin(pl.program_id(0) % seq_tiles == 0, h0_ref[0], cbuf_ref[0], tm, *scratch)
    x1 = _swiglu_ffn(x_ref[...], g1a_ref[...], g1b_ref[...], wg_ref, wu_ref, wd_ref)
    x1_ref[...] = x1
    xm = _rms(x1, gma_ref[...]).astype(BF16)
    lru = _dot(xm, wl_ref[...])
    lx_ref[...] = lru[:, :D_LRU]
    qb_ref[...] = (_dot(xm, wq_ref[...]) * (LOG2E / math.sqrt(D_HEAD_QK))).astype(BF16)
    v = _dot(xm, wv_ref[...])
    vb_ref[...] = v.astype(BF16)
    if seq_tiles is None:
        lg_ref[...] = lru[:, D_LRU:]
        k = _dot_nt(xm, wkt_ref[...])
        kf_ref[...] = k
        kb_ref[...] = k.astype(BF16)
        vf_ref[...] = v
    else:
        kt = _dot_nt(wkt_ref[...], xm)
        kf_ref[0] = kt
        kb_ref[0, 0] = kt.astype(BF16)
        for h in range(N_HEADS):
            vf_ref[pl.ds(h, tm, stride=N_HEADS), :] = v[:, h * D_HEAD_V:(h + 1) * D_HEAD_V]

        def store_out(rows, value):
            lo_ref[rows, :] = value.astype(lo_ref.dtype)

        hlast_ref[0] = _lru_tile(lru[:, :D_LRU], lru[:, D_LRU:], cw_ref[...], cb_ref[...], wr_ref[...], br_ref[...],
                                 wi_ref[...], bi_ref[...], lam_ref[...], store_out, *scratch)


def _ffn_inproj(x, g1a, g1b, gma, wg, wu, wd, wq, wkt, wv, wl, seq_len=None, lru_state=None, lru_params=None,
                later_weights=()):
    m = x.shape[0]
    tm = min(ROW_TILE, m)
    rows = lambda d: pl.BlockSpec((tm, d), lambda i: (i, 0))
    row_out = lambda d, dt: jax.ShapeDtypeStruct((m, d), dt)
    weights = (g1a, g1b, gma, wg, wu, wd, wq, wkt, wv, wl)
    in_specs = [rows(D_MODEL)] + [_resident(w.shape) for w in weights]
    operands = (x, *weights)
    scratch = []
    if seq_len is not None:
        nt = seq_len // tm
        b = m // seq_len
        k_shapes = (jax.ShapeDtypeStruct((b, D_ATT, seq_len), F32), jax.ShapeDtypeStruct((b, nt, D_ATT, tm), BF16))
        k_specs = (pl.BlockSpec((1, D_ATT, tm), lambda i: (i // nt, 0, i % nt)),
                   pl.BlockSpec((1, 1, D_ATT, tm), lambda i: (i // nt, i % nt, 0, 0)))
        vf_shape = jax.ShapeDtypeStruct((m * N_HEADS, D_HEAD_V), F32)
        vf_spec = pl.BlockSpec((tm * N_HEADS, D_HEAD_V), lambda i: (i, 0))
        per_seq = lambda a: pl.BlockSpec((1,) + a.shape[1:], lambda i: (i // nt, 0, 0))
        tail_shapes = (row_out(D_LRU, F32), row_out(D_LRU, BF16), jax.ShapeDtypeStruct((b, 1, D_LRU), F32))
        tail_specs = (rows(D_LRU), rows(D_LRU), pl.BlockSpec((1, 1, D_LRU), lambda i: (i // nt, 0, 0)))
        assert len(later_weights) == N_LATER_WEIGHTS
        n_steps = m // tm
        cast_specs, cast_shapes = [], []
        for w in later_weights:
            r = next(r for r in range(2 * SUBLANES, w.shape[0] + 1, 2 * SUBLANES)
                     if w.shape[0] % r == 0 and w.shape[0] // r <= n_steps)
            last = w.shape[0] // r - 1
            cast_specs.append(pl.BlockSpec((r, w.shape[1]), lambda i, last=last: (jnp.minimum(i, last), 0)))
            cast_shapes.append(jax.ShapeDtypeStruct(w.shape, BF16))
        in_specs += [per_seq(a) for a in lru_state] + [_resident(p.shape) for p in lru_params] + cast_specs
        operands += (*lru_state, *lru_params, *later_weights)
        tail_shapes += tuple(cast_shapes)
        tail_specs += tuple(cast_specs)
        scratch = _lru_scratch(tm)
        seq_tiles = nt
    else:
        k_shapes = (row_out(D_ATT, F32), row_out(D_ATT, BF16))
        k_specs = (rows(D_ATT), rows(D_ATT))
        vf_shape = row_out(D_ATT, F32)
        vf_spec = rows(D_ATT)
        tail_shapes = (row_out(D_LRU, F32), row_out(D_LRU, F32))
        tail_specs = (rows(D_LRU), rows(D_LRU))
        seq_tiles = None
    out_shape = (row_out(D_MODEL, F32), row_out(D_ATT, BF16), *k_shapes, vf_shape, row_out(D_ATT, BF16), *tail_shapes)
    out_specs = (rows(D_MODEL), rows(D_ATT), *k_specs, vf_spec, rows(D_ATT), *tail_specs)
    return pl.pallas_call(
        functools.partial(_ffn_inproj_kernel, seq_tiles=seq_tiles),
        grid=(m // tm,),
        in_specs=in_specs,
        out_specs=out_specs,
        out_shape=out_shape,
        scratch_shapes=scratch,
        compiler_params=pltpu.CompilerParams(dimension_semantics=("arbitrary",), vmem_limit_bytes=VMEM_LIMIT),
        name="ffn_inproj",
    )(*operands)


def _outproj_ffn_kernel(att_ref, lru_ref, x1_ref, wo_ref, gmb_ref, g2a_ref, g2b_ref,
                        wg_ref, wu_ref, wd_ref, y_ref):
    mix = _dot(att_ref[...], wo_ref[0:D_ATT, :]) + _dot(lru_ref[...], wo_ref[D_ATT:D_ATT + D_LRU, :])
    x2 = x1_ref[...] + _rms(mix, gmb_ref[...])
    y_ref[...] = _swiglu_ffn(x2, g2a_ref[...], g2b_ref[...], wg_ref, wu_ref, wd_ref)


def _outproj_ffn(att, lru, x1, wo, gmb, g2a, g2b, wg, wu, wd):
    m = x1.shape[0]
    tm = min(OUT_ROW_TILE, m)
    rows = lambda d: pl.BlockSpec((tm, d), lambda i: (i, 0))
    weights = (wo, gmb, g2a, g2b, wg, wu, wd)
    return pl.pallas_call(
        _outproj_ffn_kernel,
        grid=(m // tm,),
        in_specs=[rows(D_ATT), rows(D_LRU), rows(D_MODEL)] + [_resident(w.shape) for w in weights],
        out_specs=rows(D_MODEL),
        out_shape=jax.ShapeDtypeStruct((m, D_MODEL), F32),
        compiler_params=pltpu.CompilerParams(dimension_semantics=("arbitrary",), vmem_limit_bytes=VMEM_LIMIT),
        name="outproj_ffn",
    )(att, lru, x1, *weights)


def _split_maps(q):
    lane = lax.broadcasted_iota(jnp.int32, q.shape, 1)
    zero = jnp.zeros_like(q)
    return jnp.where(lane < D_HEAD_QK, q, zero), jnp.where(lane >= D_HEAD_QK, q, zero)


def _init_softmax_state(m_ref, l_ref, acc_ref):
    m_ref[...] = jnp.full(m_ref.shape, NEG_INF, F32)
    l_ref[...] = jnp.zeros(l_ref.shape, F32)
    acc_ref[...] = jnp.zeros(acc_ref.shape, F32)


def _bf16_split3(x):
    hi = x.astype(BF16)
    r1 = x - hi.astype(F32)
    lo = r1.astype(BF16)
    lo2 = (r1 - lo.astype(F32)).astype(BF16)
    return hi, lo, lo2


def _attn_prompt_kernel(*refs, tq, nh, n_sub):
    def one_query_tile(sub, carry):
        _attn_query_tile(pl.program_id(2) * n_sub + sub, pl.ds(pl.multiple_of(sub * tq, tq), tq), *refs, tq=tq, nh=nh)
        return carry

    lax.fori_loop(0, n_sub, one_query_tile, 0)


def _attn_query_tile(qi, rows, slopes_ref, q_ref, kt_ref, v_ref, lq1, lk1, lq2, lk2, g_ref, o_ref,
                     kta_ref, va_ref, mask_ref, m_ref, acc_ref, pa_ref, pb_ref, *, tq, nh):
    hg = pl.program_id(1)
    n_kt = kt_ref.shape[1]
    half = D_HEAD_QK
    slopes = [slopes_ref[hg * nh + g] * LOG2E for g in range(nh)]
    n_slots = 2 * nh

    @pl.when(qi == 0)
    def _():
        row = lax.broadcasted_iota(jnp.int32, (LANES, tq), 0)
        col = lax.broadcasted_iota(jnp.int32, (1, tq), 1)
        qrow = lax.broadcasted_iota(jnp.int32, (tq, tq), 0)
        kcol = lax.broadcasted_iota(jnp.int32, (tq, tq), 1)
        zero = jnp.zeros((LANES, tq), F32)
        for g in range(nh):
            terms = _bf16_split3(slopes[g] * col.astype(F32))

            def bias_rows(first):
                blk = zero
                for i, t in enumerate(terms):
                    blk = jnp.where(row == first + i, jnp.broadcast_to(t.astype(F32), (LANES, tq)), blk)
                return blk

            bias0, bias1 = bias_rows(half), bias_rows(0)

            def fill(n, carry):
                kt = kt_ref[0, n, g * LANES:(g + 1) * LANES, :].astype(F32)
                kta_ref[2 * g, n] = jnp.where(row < half, kt, bias0).astype(BF16)
                kta_ref[2 * g + 1, n] = jnp.where(row >= half, kt, bias1).astype(BF16)
                r0 = pl.multiple_of(n * tq, tq)
                va_ref[g, pl.ds(r0, tq), 0:D_HEAD_V] = v_ref[0, pl.ds(r0, tq), g * LANES:(g + 1) * LANES]
                va_ref[g, pl.ds(r0, tq), D_HEAD_V:2 * D_HEAD_V] = jnp.ones((tq, D_HEAD_V), BF16)
                return carry

            lax.fori_loop(0, n_kt, fill, 0)
            mask_ref[g] = jnp.where((qrow // CHUNK) >= (kcol // CHUNK),
                                    slopes[g] * jnp.minimum(2 * (qrow - kcol), 0).astype(F32), NEG_INF)
        acc_ref[...] = jnp.zeros(acc_ref.shape, F32)
        pb_ref[...] = jnp.zeros(pb_ref.shape, BF16)

    lane = lax.broadcasted_iota(jnp.int32, (tq, LANES), 1)
    ones0 = ((lane >= half) & (lane < half + 3)).astype(F32).astype(BF16)
    ones1 = (lane < 3).astype(F32).astype(BF16)
    qa = []
    for g in range(nh):
        q = q_ref[0, rows, g * LANES:(g + 1) * LANES]
        qa += [jnp.where(lane < half, q, ones0), jnp.where(lane >= half, q, ones1)]
    m_ref[...] = jnp.full(m_ref.shape, NEG_INF, F32)

    def values(g, j):
        return va_ref[g, pl.ds(pl.multiple_of(j * tq, tq), tq), :]

    def softmax_tile(s, m_prev, tile_bias):
        m_new = jnp.maximum(m_prev, jnp.max(s, axis=1, keepdims=True) + tile_bias)
        shift = m_new - tile_bias
        p = jnp.concatenate([jnp.exp2((s[:, b * LANES:(b + 1) * LANES] - shift).astype(BF16))
                             for b in range(s.shape[1] // LANES)], axis=1)
        return m_new, p

    def update(j, p_in_ref, p_out_ref, diagonal=False):
        logits = [_dot(qa[k], kta_ref[k, j]) for k in range(n_slots)]
        if diagonal:
            logits = [logits[k] + mask_ref[k // 2] for k in range(n_slots)]
        j_prev = jnp.maximum(j - 1, 0)
        pv = [_dot(p_in_ref[k], values(k // 2, j_prev)) for k in range(n_slots)]
        m_prev = [m_ref[k] for k in range(n_slots)]
        acc_prev = [acc_ref[k] for k in range(n_slots)]
        m_next, acc_next, p_next = [], [], []
        for k in range(n_slots):
            tile_bias = slopes[k // 2] * ((j - qi) * tq).astype(F32)
            m_new, p = softmax_tile(logits[k], m_prev[k], tile_bias)
            alpha = jnp.exp2(m_prev[k] - m_new)
            acc = jnp.concatenate([alpha, alpha], axis=1) * (acc_prev[k] + pv[k])
            if diagonal:
                acc = acc + _dot(p, values(k // 2, j))
            acc_next.append(acc)
            m_next.append(m_new)
            p_next.append(p)
        for k in range(n_slots):
            acc_ref[k] = acc_next[k]
            m_ref[k] = m_next[k]
            if not diagonal:
                p_out_ref[k] = p_next[k]

    odd = qi % 2

    @pl.when(odd == 1)
    def _():
        for k in range(n_slots):
            tile_bias = slopes[k // 2] * (-qi * tq).astype(F32)
            m_new, p = softmax_tile(_dot(qa[k], kta_ref[k, 0]), m_ref[k], tile_bias)
            m_ref[k] = m_new
            pb_ref[k] = p
        acc_ref[...] = jnp.zeros(acc_ref.shape, F32)

    def pair(i, carry):
        j = odd + 2 * i
        update(j, pb_ref, pa_ref)

        @pl.when(j + 1 < qi)
        def _():
            update(j + 1, pa_ref, pb_ref)

        return carry

    lax.fori_loop(0, qi // 2, pair, 0)
    update(qi, pb_ref, None, diagonal=True)

    lam = (jnp.exp(jnp.sum(lq1[...] * lk1[...], axis=1, keepdims=True))
           - jnp.exp(jnp.sum(lq2[...] * lk2[...], axis=1, keepdims=True)) + LAMBDA_INIT)
    for g in range(nh):
        a0, a1 = acc_ref[2 * g], acc_ref[2 * g + 1]
        o = a0[:, :D_HEAD_V] / a0[:, D_HEAD_V:] - lam * (a1[:, :D_HEAD_V] / a1[:, D_HEAD_V:])
        o_ref[0, rows, g * LANES:(g + 1) * LANES] = (_rms(o, g_ref[...]) * (1.0 - LAMBDA_INIT)).astype(o_ref.dtype)


def _attn_prompt(slopes, q, kt, v, lq1, lk1, lq2, lk2, g):
    b, t, _ = q.shape
    n_kt, tq = kt.shape[1], kt.shape[3]
    nh = ATTN_HEADS_PER_STEP
    w = nh * LANES
    n_sub = math.gcd(ATTN_TILES_PER_STEP, t // tq)
    small = lambda a: pl.BlockSpec(a.shape, lambda bi, hi, qi: (0, 0))
    return pl.pallas_call(
        functools.partial(_attn_prompt_kernel, tq=tq, nh=nh, n_sub=n_sub),
        grid=(b, N_HEADS // nh, t // (tq * n_sub)),
        in_specs=[pl.BlockSpec(memory_space=pltpu.SMEM),
                  pl.BlockSpec((1, tq * n_sub, w), lambda bi, hi, qi: (bi, qi, hi)),
                  pl.BlockSpec((1, n_kt, w, tq), lambda bi, hi, qi: (bi, 0, hi, 0)),
                  pl.BlockSpec((1, t, w), lambda bi, hi, qi: (bi, 0, hi)),
                  small(lq1), small(lk1), small(lq2), small(lk2), small(g)],
        out_specs=pl.BlockSpec((1, tq * n_sub, w), lambda bi, hi, qi: (bi, qi, hi)),
        out_shape=jax.ShapeDtypeStruct((b, t, D_ATT), BF16),
        scratch_shapes=[pltpu.VMEM((2 * nh, n_kt, LANES, tq), BF16), pltpu.VMEM((nh, t, 2 * D_HEAD_V), BF16),
                        pltpu.VMEM((nh, tq, tq), F32), pltpu.VMEM((2 * nh, tq, LANES), F32),
                        pltpu.VMEM((2 * nh, tq, 2 * D_HEAD_V), F32)] + [pltpu.VMEM((2 * nh, tq, tq), BF16)] * 2,
        compiler_params=pltpu.CompilerParams(dimension_semantics=("arbitrary",) * 3, vmem_limit_bytes=VMEM_LIMIT),
        name="attn_prompt",
    )(slopes, q, kt, v, lq1, lk1, lq2, lk2, g)


def _attn_sample_kernel(slopes_ref, q_ref, ckt_ref, cv_ref, kn_ref, vn_ref, lq1, lk1, lq2, lk2, g_ref, o_ref,
                        m_ref, l_ref, acc_ref, *, past, tk):
    j = pl.program_id(1)
    tq = q_ref.shape[1]

    @pl.when(j == 0)
    def _():
        _init_softmax_state(m_ref, l_ref, acc_ref)

    def attend(n, k_start, n_valid, scores, values):
        q_pos = past + lax.broadcasted_iota(jnp.int32, (tq, n), 0)
        k_pos = k_start + lax.broadcasted_iota(jnp.int32, (tq, n), 1)
        dist = jnp.abs(q_pos - k_pos).astype(F32)
        dist = jnp.concatenate([dist, dist], axis=0)
        visible = ((q_pos // CHUNK) >= (k_pos // CHUNK)) & (k_pos < k_start + n_valid)
        visible = jnp.concatenate([visible, visible], axis=0)
        state = [(m_ref[h], l_ref[h], acc_ref[h]) for h in range(N_HEADS)]
        for h in range(N_HEADS):
            m_prev, l_prev, acc_prev = state[h]
            q2 = jnp.concatenate(_split_maps(q_ref[0, :, h * LANES:(h + 1) * LANES]), axis=0)
            s = jnp.where(visible, scores(q2, h) - (slopes_ref[h] * LOG2E) * dist, NEG_INF)
            m_new = jnp.maximum(m_prev, jnp.max(s, axis=1, keepdims=True))
            alpha = jnp.exp2(m_prev - m_new)
            ps = [jnp.exp2(s[:, b * LANES:(b + 1) * LANES] - m_new) for b in range(n // LANES)]
            p = jnp.concatenate(ps, axis=1).astype(BF16)
            state[h] = (m_new, alpha * l_prev + functools.reduce(lambda x, y: x + y, ps),
                        alpha * acc_prev + _dot(p, values(h)))
        for h in range(N_HEADS):
            m_ref[h], l_ref[h], acc_ref[h] = state[h]

    attend(tk, j * tk, tk,
           lambda q2, h: _dot(q2, ckt_ref[0, h * LANES:(h + 1) * LANES, :].astype(BF16)),
           lambda h: cv_ref[0, pl.ds(h, tk, stride=N_HEADS), :].astype(BF16))

    @pl.when(j == pl.num_programs(1) - 1)
    def _():
        attend(kn_ref.shape[1], past, tq,
               lambda q2, h: _dot_nt(q2, kn_ref[0, :, h * LANES:(h + 1) * LANES]),
               lambda h: vn_ref[0, :, h * LANES:(h + 1) * LANES])
        lam = (jnp.exp(jnp.sum(lq1[...] * lk1[...], axis=1, keepdims=True))
               - jnp.exp(jnp.sum(lq2[...] * lk2[...], axis=1, keepdims=True)) + LAMBDA_INIT)
        for h in range(N_HEADS):
            sm = acc_ref[h] / jnp.sum(l_ref[h], axis=1, keepdims=True)
            o = _rms(sm[:tq] - lam * sm[tq:], g_ref[...]) * (1.0 - LAMBDA_INIT)
            o_ref[0, :, h * LANES:(h + 1) * LANES] = o.astype(o_ref.dtype)


def _attn_sample(slopes, q, cache_kt, cache_v_rows, k_new, v_new, lq1, lk1, lq2, lk2, g):
    b, t, _ = q.shape
    past = cache_kt.shape[2]
    tk = min(CACHE_TILE, past)
    k_new = jnp.pad(k_new, ((0, 0), (0, -t % LANES), (0, 0)))
    v_new = jnp.pad(v_new, ((0, 0), (0, -t % LANES), (0, 0)))
    small = lambda a: pl.BlockSpec(a.shape, lambda bi, j: (0, 0))
    per_stream = lambda a: pl.BlockSpec((1,) + a.shape[1:], lambda bi, j: (bi, 0, 0))
    return pl.pallas_call(
        functools.partial(_attn_sample_kernel, past=past, tk=tk),
        grid=(b, past // tk),
        in_specs=[pl.BlockSpec(memory_space=pltpu.SMEM),
                  per_stream(q),
                  pl.BlockSpec((1, D_ATT, tk), lambda bi, j: (bi, 0, j)),
                  pl.BlockSpec((1, tk * N_HEADS, D_HEAD_V), lambda bi, j: (bi, j, 0)),
                  per_stream(k_new), per_stream(v_new),
                  small(lq1), small(lk1), small(lq2), small(lk2), small(g)],
        out_specs=pl.BlockSpec((1, t, D_ATT), lambda bi, j: (bi, 0, 0)),
        out_shape=jax.ShapeDtypeStruct((b, t, D_ATT), BF16),
        scratch_shapes=[pltpu.VMEM((N_HEADS, 2 * t, LANES), F32)] * 3,
        compiler_params=pltpu.CompilerParams(dimension_semantics=("arbitrary",) * 2, vmem_limit_bytes=VMEM_LIMIT),
        name="attn_sample",
    )(slopes, q, cache_kt, cache_v_rows, k_new, v_new, lq1, lk1, lq2, lk2, g)


def _block_diag_dense(w):
    n, c, _ = w.shape
    eye = jnp.eye(n, dtype=w.dtype)
    return (eye[:, None, :, None] * w[:, :, None, :]).reshape(n * c, n * c)


def kernel(x_prompt, x_sample, cache_k, cache_v, state_lru_h, state_conv, w_in, w_out, lambda_q1, lambda_k1,
           lambda_q2, lambda_k2, subln_g, conv_w, conv_b, w_rgate, b_rgate, w_igate, b_igate, lru_lambda,
           ffn1_w_gate, ffn1_w_up, ffn1_w_down, ffn2_w_gate, ffn2_w_up, ffn2_w_down,
           g_ffn1_pre, g_ffn1_post, g_mix_pre, g_mix_post, g_ffn2_pre, g_ffn2_post):
    bp, tp, _ = x_prompt.shape
    bs, ts, _ = x_sample.shape
    past = cache_k.shape[2]
    assert w_in.shape[0] == 1, "one layer: LAMBDA_INIT is the depth-0 value"
    assert tp % ROW_TILE == 0 and (bp * tp) % OUT_ROW_TILE == 0
    assert bs * ts <= ROW_TILE and ts % (2 * SUBLANES) == 0
    assert past % min(CACHE_TILE, past) == 0

    wg1, wu1, wd1 = ffn1_w_gate[0].astype(BF16), ffn1_w_up[0].astype(BF16), ffn1_w_down[0].astype(BF16)
    win = w_in[0].astype(BF16)
    wq, wkt, wv, wl = win[:, :D_ATT], win[:, D_ATT:2 * D_ATT].T, win[:, 2 * D_ATT:3 * D_ATT], win[:, 3 * D_ATT:]
    wr = _block_diag_dense(w_rgate[0]).astype(BF16)
    wi = _block_diag_dense(w_igate[0]).astype(BF16)
    slopes = jnp.asarray(2.0 ** (-8.0 * np.arange(1, N_HEADS + 1) / N_HEADS), dtype=F32)
    lam_vecs = (lambda_q1, lambda_k1, lambda_q2, lambda_k2)
    ffn1 = (g_ffn1_pre, g_ffn1_post, g_mix_pre, wg1, wu1, wd1, wq, wkt, wv, wl)
    lru_params = (conv_w[0], conv_b, wr, b_rgate, wi, b_igate, lru_lambda)
    tail_pad = 8 - (CONV_W - 1)

    def conv_state(conv_hist, lx, b, t):
        return jnp.concatenate([conv_hist, lx.reshape(b, t, D_LRU)], axis=1)[:, -(CONV_W - 1):][None]

    def pad_hist(conv_hist):
        return jnp.pad(conv_hist, ((0, 0), (tail_pad, 0), (0, 0)))

    hist_p = jnp.zeros((bp, CONV_W - 1, D_LRU), F32)
    x1, qb, kt, ktb, v_rows, vb, lx, lru_out, hp, wg2, wu2, wd2, wo = _ffn_inproj(
        x_prompt.reshape(bp * tp, D_MODEL), *ffn1, seq_len=tp,
        lru_state=(jnp.zeros((bp, 1, D_LRU), F32), pad_hist(hist_p)), lru_params=lru_params,
        later_weights=(ffn2_w_gate[0], ffn2_w_up[0], ffn2_w_down[0], w_out[0]))
    ffn2 = (wo, g_mix_post, g_ffn2_pre, g_ffn2_post, wg2, wu2, wd2)
    att = _attn_prompt(slopes, qb.reshape(bp, tp, D_ATT), ktb, vb.reshape(bp, tp, D_ATT), *lam_vecs, subln_g)
    yp = _outproj_ffn(att.reshape(bp * tp, D_ATT), lru_out, x1, *ffn2).reshape(bp, tp, D_MODEL)
    hp, cp = hp.reshape(1, bp, D_LRU), conv_state(hist_p, lx, bp, tp)
    kp = kt.reshape(bp, N_HEADS, 2, D_HEAD_QK, tp).transpose(0, 4, 1, 2, 3)[None]
    vp = v_rows.reshape(1, bp, tp, N_HEADS, D_HEAD_V)

    x1, qb, kf, kb, vf, vb, lx, lg = _ffn_inproj(x_sample.reshape(bs * ts, D_MODEL), *ffn1)
    cache_kt = cache_k[0].transpose(0, 2, 3, 4, 1).reshape(bs, D_ATT, past)
    cache_v_rows = cache_v[0].reshape(bs, past * N_HEADS, D_HEAD_V)
    att = _attn_sample(slopes, qb.reshape(bs, ts, D_ATT), cache_kt, cache_v_rows, kb.reshape(bs, ts, D_ATT),
                       vb.reshape(bs, ts, D_ATT), *lam_vecs, subln_g)
    lru_out, hs = _lru(lx.reshape(bs, ts, D_LRU), lg.reshape(bs, ts, D_LRU), state_lru_h[0].reshape(bs, 1, D_LRU),
                       pad_hist(state_conv[0]), *lru_params)
    ys = _outproj_ffn(att.reshape(bs * ts, D_ATT), lru_out.reshape(bs * ts, D_LRU), x1, *ffn2)
    ys, hs, cs = ys.reshape(bs, ts, D_MODEL), hs.reshape(1, bs, D_LRU), conv_state(state_conv[0], lx, bs, ts)
    ks = kf.reshape(1, bs, ts, N_HEADS, 2, D_HEAD_QK)
    vs = vf.reshape(1, bs, ts, N_HEADS, D_HEAD_V)
    return (yp, ys, kp, vp, hp, cp, ks, vs, hs, cs)
```

```python
import functools
import math

import jax
import jax.numpy as jnp
import numpy as np
from jax import lax
from jax.experimental import pallas as pl
from jax.experimental.pallas import tpu as pltpu

F32 = jnp.float32
BF16 = jnp.bfloat16

D_MODEL = 1024
D_ATT = 512
D_LRU = 512
N_HEADS = 4
D_HEAD_V = 128
D_HEAD_QK = 64
CONV_W = 4
LRU_C = 8.0
D_FF = 2816
CHUNK = 64
RMS_EPS = 1e-6
NEG_INF = -1e30
LAMBDA_INIT = 0.8 - 0.6 * math.exp(-0.3 * 0)
LOG2E = math.log2(math.e)

LANES = 128
SUBLANES = 8
FF_CHUNK = 256
N_FF_CHUNKS = D_FF // FF_CHUNK
ROW_TILE = 512
N_LATER_WEIGHTS = 4
OUT_ROW_TILE = 1024
ATTN_HEADS_PER_STEP = 2
LRU_TILE = 1024
CACHE_TILE = 4096
VMEM_LIMIT = 56 * 1024 * 1024


def _dot(a, b):
    return jnp.dot(a, b, preferred_element_type=F32)


def _dot_nt(a, b):
    return lax.dot_general(a, b, (((1,), (1,)), ((), ())), preferred_element_type=F32)


def _rms(x, g):
    return x * lax.rsqrt(jnp.mean(x * x, axis=-1, keepdims=True) + RMS_EPS) * g


def _swiglu_ffn(x, g_pre, g_post, wg_ref, wu_ref, wd_ref):
    xn = _rms(x, g_pre).astype(BF16)
    acc = jnp.zeros(x.shape, F32)
    for j in range(N_FF_CHUNKS):
        cols = slice(j * FF_CHUNK, (j + 1) * FF_CHUNK)
        g = _dot(xn, wg_ref[:, cols])
        u = _dot(xn, wu_ref[:, cols])
        h = (g * jax.nn.sigmoid(g) * u).astype(BF16)
        acc = acc + _dot(h, wd_ref[cols, :])
    return x + 0.5 * _rms(acc, g_post)


def _resident(shape):
    nd = len(shape)
    return pl.BlockSpec(shape, lambda i: (0,) * nd, pipeline_mode=pl.Buffered(1))


def _sigmoid(x):
    return 0.5 * (1.0 + jnp.tanh(0.5 * x))


def _lru_begin(is_first, h0, cbuf, tt, xbuf_ref, h_ref, *_):
    pad = xbuf_ref.shape[0] - tt

    @pl.when(is_first)
    def _():
        xbuf_ref[0:pad, :] = cbuf
        h_ref[...] = h0

    @pl.when(jnp.logical_not(is_first))
    def _():
        xbuf_ref[0:pad, :] = xbuf_ref[tt:tt + pad, :]


def _lru_tile(x, gate, cw, cb, wr, br, wi, bi, lam, store_out, xbuf_ref, h_ref, a_ref, b_ref, p_ref, hs_ref):
    tt = x.shape[0]
    pad = xbuf_ref.shape[0] - tt
    n_seg = min(SUBLANES, tt // SUBLANES)
    seg = tt // n_seg
    xbuf_ref[pad:pad + tt, :] = x
    xc = cb + xbuf_ref[pad:pad + tt, :] * cw[CONV_W - 1:CONV_W, :]
    for j in range(CONV_W - 1):
        back = CONV_W - 1 - j
        xc = xc + xbuf_ref[pad - back:pad - back + tt, :] * cw[j:j + 1, :]

    xb = xc.astype(BF16)
    r = _sigmoid(_dot(xb, wr) + br)
    i = _sigmoid(_dot(xb, wi) + bi)
    neg_lam = -lam
    softplus = jnp.maximum(neg_lam, 0.0) + jnp.log1p(jnp.exp(-jnp.abs(neg_lam)))
    log_a = -LRU_C * r * softplus
    a = jnp.exp(log_a)
    u = -jnp.tanh(log_a) * (1.0 + a * a)
    b = jnp.where(u > 0.0, u * lax.rsqrt(u), 0.0) * (i * xc)
    n_blk = D_LRU // LANES
    pitch = a_ref.shape[1] // n_seg
    for s in range(n_seg):
        for l in range(n_blk):
            a_ref[l, s * pitch:s * pitch + seg, :] = a[s * seg:(s + 1) * seg, l * LANES:(l + 1) * LANES]
            b_ref[l, s * pitch:s * pitch + seg, :] = b[s * seg:(s + 1) * seg, l * LANES:(l + 1) * LANES]

    h = [jnp.zeros((n_seg, LANES), F32)] * n_blk
    p = [jnp.ones((n_seg, LANES), F32)] * n_blk
    for j in range(seg):
        rows = pl.ds(j, n_seg, stride=pitch)
        for l in range(n_blk):
            a_j = a_ref[l, rows, :]
            h[l] = a_j * h[l] + b_ref[l, rows, :]
            p[l] = a_j * p[l]
            hs_ref[l, rows, :] = h[l]
            p_ref[l, rows, :] = p[l]

    h_end = jnp.concatenate(h, axis=1)
    p_end = jnp.concatenate(p, axis=1)
    carry = h_ref[...]
    for s in range(n_seg):
        rows = slice(s * pitch, s * pitch + seg)
        p_rows = jnp.concatenate([p_ref[l, rows, :] for l in range(n_blk)], axis=1)
        hs_rows = jnp.concatenate([hs_ref[l, rows, :] for l in range(n_blk)], axis=1)
        hs = p_rows * carry + hs_rows
        g = gate[s * seg:(s + 1) * seg, :]
        gelu = 0.5 * g * (1.0 + jnp.tanh(math.sqrt(2.0 / math.pi) * (g + 0.044715 * (g * g * g))))
        store_out(slice(s * seg, (s + 1) * seg), hs * gelu)
        carry = p_end[s:s + 1, :] * carry + h_end[s:s + 1, :]
    h_ref[...] = carry
    return carry


def _lru_scratch(tt):
    n_seg = min(SUBLANES, tt // SUBLANES)
    return ([pltpu.VMEM((tt + SUBLANES, D_LRU), F32), pltpu.VMEM((1, D_LRU), F32)]
            + [pltpu.VMEM((D_LRU // LANES, tt + 4 * n_seg, LANES), F32)] * 4)


def _lru_kernel(x_ref, gate_ref, h0_ref, cbuf_ref, cw_ref, cb_ref, wr_ref, br_ref, wi_ref, bi_ref, lam_ref,
                out_ref, hlast_ref, *scratch):
    def store_out(rows, value):
        out_ref[0, rows, :] = value.astype(out_ref.dtype)

    _lru_begin(pl.program_id(1) == 0, h0_ref[0], cbuf_ref[0], x_ref.shape[1], *scratch)
    hlast_ref[0] = _lru_tile(x_ref[0], gate_ref[0], cw_ref[...], cb_ref[...], wr_ref[...], br_ref[...],
                             wi_ref[...], bi_ref[...], lam_ref[...], store_out, *scratch)


def _lru(x, gate, h0, cbuf, cw, cb, wr, br, wi, bi, lam):
    b, t, _ = x.shape
    tt = min(LRU_TILE, t)
    small = lambda a: pl.BlockSpec(a.shape, lambda bi_, ti: (0, 0))
    tile = pl.BlockSpec((1, tt, D_LRU), lambda bi_, ti: (bi_, ti, 0))
    per_stream = lambda a: pl.BlockSpec((1,) + a.shape[1:], lambda bi_, ti: (bi_, 0, 0))
    return pl.pallas_call(
        _lru_kernel,
        grid=(b, t // tt),
        in_specs=[tile, tile, per_stream(h0), per_stream(cbuf), small(cw), small(cb),
                  small(wr), small(br), small(wi), small(bi), small(lam)],
        out_specs=(tile, pl.BlockSpec((1, 1, D_LRU), lambda bi_, ti: (bi_, 0, 0))),
        out_shape=(jax.ShapeDtypeStruct((b, t, D_LRU), BF16), jax.ShapeDtypeStruct((b, 1, D_LRU), F32)),
        scratch_shapes=_lru_scratch(tt),
        compiler_params=pltpu.CompilerParams(dimension_semantics=("arbitrary",) * 2, vmem_limit_bytes=VMEM_LIMIT),
        name="lru",
    )(x, gate, h0, cbuf, cw, cb, wr, br, wi, bi, lam)


def _ffn_inproj_kernel(x_ref, g1a_ref, g1b_ref, gma_ref, wg_ref, wu_ref, wd_ref, wq_ref, wkt_ref, wv_ref, wl_ref,
                       *refs, seq_tiles):
    tm = x_ref.shape[0]
    if seq_tiles is None:
        x1_ref, qb_ref, kf_ref, kb_ref, vf_ref, vb_ref, lx_ref, lg_ref = refs
    else:
        n_cast = N_LATER_WEIGHTS
        (h0_ref, cbuf_ref, cw_ref, cb_ref, wr_ref, br_ref, wi_ref, bi_ref, lam_ref, *rest) = refs
        cast_in, rest = rest[:n_cast], rest[n_cast:]
        (x1_ref, qb_ref, kf_ref, kb_ref, vf_ref, vb_ref, lx_ref, lo_ref, hlast_ref, *rest) = rest
        cast_out, scratch = rest[:n_cast], rest[n_cast:]
        for src, dst in zip(cast_in, cast_out):
            dst[...] = src[...].astype(BF16)
        _lru_begin(pl.program_id(0) % seq_tiles == 0, h0_ref[0], cbuf_ref[0], tm, *scratch)
    x1 = _swiglu_ffn(x_ref[...], g1a_ref[...], g1b_ref[...], wg_ref, wu_ref, wd_ref)
    x1_ref[...] = x1
    xm = _rms(x1, gma_ref[...]).astype(BF16)
    lru = _dot(xm, wl_ref[...])
    lx_ref[...] = lru[:, :D_LRU]
    qb_ref[...] = (_dot(xm, wq_ref[...]) * (LOG2E / math.sqrt(D_HEAD_QK))).astype(BF16)
    v = _dot(xm, wv_ref[...])
    vb_ref[...] = v.astype(BF16)
    if seq_tiles is None:
        lg_ref[...] = lru[:, D_LRU:]
        k = _dot_nt(xm, wkt_ref[...])
        kf_ref[...] = k
        kb_ref[...] = k.astype(BF16)
        vf_ref[...] = v
    else:
        kt = _dot_nt(wkt_ref[...], xm)
        kf_ref[0] = kt
        kb_ref[0, 0] = kt.astype(BF16)
        for h in range(N_HEADS):
            vf_ref[pl.ds(h, tm, stride=N_HEADS), :] = v[:, h * D_HEAD_V:(h + 1) * D_HEAD_V]

        def store_out(rows, value):
            lo_ref[rows, :] = value.astype(lo_ref.dtype)

        hlast_ref[0] = _lru_tile(lru[:, :D_LRU], lru[:, D_LRU:], cw_ref[...], cb_ref[...], wr_ref[...], br_ref[...],
                                 wi_ref[...], bi_ref[...], lam_ref[...], store_out, *scratch)


def _ffn_inproj(x, g1a, g1b, gma, wg, wu, wd, wq, wkt, wv, wl, seq_len=None, lru_state=None, lru_params=None,
                later_weights=()):
    m = x.shape[0]
    tm = min(ROW_TILE, m)
    rows = lambda d: pl.BlockSpec((tm, d), lambda i: (i, 0))
    row_out = lambda d, dt: jax.ShapeDtypeStruct((m, d), dt)
    weights = (g1a, g1b, gma, wg, wu, wd, wq, wkt, wv, wl)
    in_specs = [rows(D_MODEL)] + [_resident(w.shape) for w in weights]
    operands = (x, *weights)
    scratch = []
    if seq_len is not None:
        nt = seq_len // tm
        b = m // seq_len
        k_shapes = (jax.ShapeDtypeStruct((b, D_ATT, seq_len), F32), jax.ShapeDtypeStruct((b, nt, D_ATT, tm), BF16))
        k_specs = (pl.BlockSpec((1, D_ATT, tm), lambda i: (i // nt, 0, i % nt)),
                   pl.BlockSpec((1, 1, D_ATT, tm), lambda i: (i // nt, i % nt, 0, 0)))
        vf_shape = jax.ShapeDtypeStruct((m * N_HEADS, D_HEAD_V), F32)
        vf_spec = pl.BlockSpec((tm * N_HEADS, D_HEAD_V), lambda i: (i, 0))
        per_seq = lambda a: pl.BlockSpec((1,) + a.shape[1:], lambda i: (i // nt, 0, 0))
        tail_shapes = (row_out(D_LRU, F32), row_out(D_LRU, BF16), jax.ShapeDtypeStruct((b, 1, D_LRU), F32))
        tail_specs = (rows(D_LRU), rows(D_LRU), pl.BlockSpec((1, 1, D_LRU), lambda i: (i // nt, 0, 0)))
        assert len(later_weights) == N_LATER_WEIGHTS
        n_steps = m // tm
        cast_specs, cast_shapes = [], []
        for w in later_weights:
            r = next(r for r in range(2 * SUBLANES, w.shape[0] + 1, 2 * SUBLANES)
                     if w.shape[0] % r == 0 and w.shape[0] // r <= n_steps)
            last = w.shape[0] // r - 1
            cast_specs.append(pl.BlockSpec((r, w.shape[1]), lambda i, last=last: (jnp.minimum(i, last), 0)))
            cast_shapes.append(jax.ShapeDtypeStruct(w.shape, BF16))
        in_specs += [per_seq(a) for a in lru_state] + [_resident(p.shape) for p in lru_params] + cast_specs
        operands += (*lru_state, *lru_params, *later_weights)
        tail_shapes += tuple(cast_shapes)
        tail_specs += tuple(cast_specs)
        scratch = _lru_scratch(tm)
        seq_tiles = nt
    else:
        k_shapes = (row_out(D_ATT, F32), row_out(D_ATT, BF16))
        k_specs = (rows(D_ATT), rows(D_ATT))
        vf_shape = row_out(D_ATT, F32)
        vf_spec = rows(D_ATT)
        tail_shapes = (row_out(D_LRU, F32), row_out(D_LRU, F32))
        tail_specs = (rows(D_LRU), rows(D_LRU))
        seq_tiles = None
    out_shape = (row_out(D_MODEL, F32), row_out(D_ATT, BF16), *k_shapes, vf_shape, row_out(D_ATT, BF16), *tail_shapes)
    out_specs = (rows(D_MODEL), rows(D_ATT), *k_specs, vf_spec, rows(D_ATT), *tail_specs)
    return pl.pallas_call(
        functools.partial(_ffn_inproj_kernel, seq_tiles=seq_tiles),
        grid=(m // tm,),
        in_specs=in_specs,
        out_specs=out_specs,
        out_shape=out_shape,
        scratch_shapes=scratch,
        compiler_params=pltpu.CompilerParams(dimension_semantics=("arbitrary",), vmem_limit_bytes=VMEM_LIMIT),
        name="ffn_inproj",
    )(*operands)


def _outproj_ffn_kernel(*refs, n_main):
    n_in = 3 if n_main is None else 6
    streams, (wo_ref, gmb_ref, g2a_ref, g2b_ref, wg_ref, wu_ref, wd_ref), outs = refs[:n_in], refs[n_in:n_in + 7], refs[n_in + 7:]

    def tile(att_ref, lru_ref, x1_ref, y_ref):
        mix = _dot(att_ref[...], wo_ref[0:D_ATT, :]) + _dot(lru_ref[...], wo_ref[D_ATT:D_ATT + D_LRU, :])
        x2 = x1_ref[...] + _rms(mix, gmb_ref[...])
        y_ref[...] = _swiglu_ffn(x2, g2a_ref[...], g2b_ref[...], wg_ref, wu_ref, wd_ref)

    if n_main is None:
        tile(*streams, *outs)
    else:
        pl.when(pl.program_id(0) < n_main)(lambda: tile(*streams[:3], outs[0]))
        pl.when(pl.program_id(0) == n_main)(lambda: tile(*streams[3:], outs[1]))


def _outproj_ffn(att, lru, x1, wo, gmb, g2a, g2b, wg, wu, wd, small_stream=None):
    m = x1.shape[0]
    tm = min(OUT_ROW_TILE, m)
    n_main = m // tm
    weights = (wo, gmb, g2a, g2b, wg, wu, wd)
    if small_stream is None:
        rows = lambda d: pl.BlockSpec((tm, d), lambda i: (i, 0))
        streams, stream_specs = (att, lru, x1), [rows(D_ATT), rows(D_LRU), rows(D_MODEL)]
        out_specs, out_shape = rows(D_MODEL), jax.ShapeDtypeStruct((m, D_MODEL), F32)
        grid, n_arg = (n_main,), None
    else:
        ms = small_stream[2].shape[0]
        assert ms <= tm
        rows = lambda d: pl.BlockSpec((tm, d), lambda i: (jnp.minimum(i, n_main - 1), 0))
        whole = lambda d, **kw: pl.BlockSpec((ms, d), lambda i: (0, 0), **kw)
        once = dict(pipeline_mode=pl.Buffered(1))
        streams = (att, lru, x1, *small_stream)
        stream_specs = [rows(D_ATT), rows(D_LRU), rows(D_MODEL),
                        whole(D_ATT, **once), whole(D_LRU, **once), whole(D_MODEL, **once)]
        out_specs = (rows(D_MODEL), whole(D_MODEL))
        out_shape = (jax.ShapeDtypeStruct((m, D_MODEL), F32), jax.ShapeDtypeStruct((ms, D_MODEL), F32))
        grid, n_arg = (n_main + 1,), n_main
    return pl.pallas_call(
        functools.partial(_outproj_ffn_kernel, n_main=n_arg),
        grid=grid,
        in_specs=stream_specs + [_resident(w.shape) for w in weights],
        out_specs=out_specs,
        out_shape=out_shape,
        compiler_params=pltpu.CompilerParams(dimension_semantics=("arbitrary",), vmem_limit_bytes=VMEM_LIMIT),
        name="outproj_ffn",
    )(*streams, *weights)


def _split_maps(q):
    lane = lax.broadcasted_iota(jnp.int32, q.shape, 1)
    zero = jnp.zeros_like(q)
    return jnp.where(lane < D_HEAD_QK, q, zero), jnp.where(lane >= D_HEAD_QK, q, zero)


def _init_softmax_state(m_ref, l_ref, acc_ref):
    m_ref[...] = jnp.full(m_ref.shape, NEG_INF, F32)
    l_ref[...] = jnp.zeros(l_ref.shape, F32)
    acc_ref[...] = jnp.zeros(acc_ref.shape, F32)


def _bf16_split3(x):
    hi = x.astype(BF16)
    r1 = x - hi.astype(F32)
    lo = r1.astype(BF16)
    lo2 = (r1 - lo.astype(F32)).astype(BF16)
    return hi, lo, lo2


def _attn_prompt_kernel(slopes_ref, q_ref, kt_ref, v_ref, lq1, lk1, lq2, lk2, g_ref, o_ref,
                        kta_ref, va_ref, mask_ref, m_ref, acc_ref, pa_ref, pb_ref, *, tq, nh):
    hg = pl.program_id(1)
    qi = pl.program_id(2)
    n_kt = kt_ref.shape[1]
    half = D_HEAD_QK
    slopes = [slopes_ref[hg * nh + g] * LOG2E for g in range(nh)]
    n_slots = 2 * nh

    @pl.when(qi == 0)
    def _():
        row = lax.broadcasted_iota(jnp.int32, (LANES, tq), 0)
        col = lax.broadcasted_iota(jnp.int32, (1, tq), 1)
        qrow = lax.broadcasted_iota(jnp.int32, (tq, tq), 0)
        kcol = lax.broadcasted_iota(jnp.int32, (tq, tq), 1)
        zero = jnp.zeros((LANES, tq), F32)
        for g in range(nh):
            terms = _bf16_split3(slopes[g] * col.astype(F32))

            def bias_rows(first):
                blk = zero
                for i, t in enumerate(terms):
                    blk = jnp.where(row == first + i, jnp.broadcast_to(t.astype(F32), (LANES, tq)), blk)
                return blk

            bias0, bias1 = bias_rows(half), bias_rows(0)

            def fill(n, carry):
                kt = kt_ref[0, n, g * LANES:(g + 1) * LANES, :].astype(F32)
                kta_ref[2 * g, n] = jnp.where(row < half, kt, bias0).astype(BF16)
                kta_ref[2 * g + 1, n] = jnp.where(row >= half, kt, bias1).astype(BF16)
                r0 = pl.multiple_of(n * tq, tq)
                va_ref[g, pl.ds(r0, tq), 0:D_HEAD_V] = v_ref[0, pl.ds(r0, tq), g * LANES:(g + 1) * LANES]
                va_ref[g, pl.ds(r0, tq), D_HEAD_V:2 * D_HEAD_V] = jnp.ones((tq, D_HEAD_V), BF16)
                return carry

            lax.fori_loop(0, n_kt, fill, 0)
            mask_ref[g] = jnp.where((qrow // CHUNK) >= (kcol // CHUNK),
                                    slopes[g] * jnp.minimum(2 * (qrow - kcol), 0).astype(F32), NEG_INF)
        acc_ref[...] = jnp.zeros(acc_ref.shape, F32)
        pb_ref[...] = jnp.zeros(pb_ref.shape, BF16)

    lane = lax.broadcasted_iota(jnp.int32, (tq, LANES), 1)
    ones0 = ((lane >= half) & (lane < half + 3)).astype(F32).astype(BF16)
    ones1 = (lane < 3).astype(F32).astype(BF16)
    qa = []
    for g in range(nh):
        q = q_ref[0, :, g * LANES:(g + 1) * LANES]
        qa += [jnp.where(lane < half, q, ones0), jnp.where(lane >= half, q, ones1)]
    m_ref[...] = jnp.full(m_ref.shape, NEG_INF, F32)

    def values(g, j):
        return va_ref[g, pl.ds(pl.multiple_of(j * tq, tq), tq), :]

    def softmax_tile(s, m_prev, tile_bias):
        m_new = jnp.maximum(m_prev, jnp.max(s, axis=1, keepdims=True) + tile_bias)
        shift = m_new - tile_bias
        p = jnp.concatenate([jnp.exp2((s[:, b * LANES:(b + 1) * LANES] - shift).astype(BF16))
                             for b in range(s.shape[1] // LANES)], axis=1)
        return m_new, p

    def update(j, p_in_ref, p_out_ref, diagonal=False):
        logits = [_dot(qa[k], kta_ref[k, j]) for k in range(n_slots)]
        if diagonal:
            logits = [logits[k] + mask_ref[k // 2] for k in range(n_slots)]
        j_prev = jnp.maximum(j - 1, 0)
        pv = [_dot(p_in_ref[k], values(k // 2, j_prev)) for k in range(n_slots)]
        m_prev = [m_ref[k] for k in range(n_slots)]
        acc_prev = [acc_ref[k] for k in range(n_slots)]
        m_next, acc_next, p_next = [], [], []
        for k in range(n_slots):
            tile_bias = slopes[k // 2] * ((j - qi) * tq).astype(F32)
            m_new, p = softmax_tile(logits[k], m_prev[k], tile_bias)
            alpha = jnp.exp2(m_prev[k] - m_new)
            acc = jnp.concatenate([alpha, alpha], axis=1) * (acc_prev[k] + pv[k])
            if diagonal:
                acc = acc + _dot(p, values(k // 2, j))
            acc_next.append(acc)
            m_next.append(m_new)
            p_next.append(p)
        for k in range(n_slots):
            acc_ref[k] = acc_next[k]
            m_ref[k] = m_next[k]
            if not diagonal:
                p_out_ref[k] = p_next[k]

    odd = qi % 2

    @pl.when(odd == 1)
    def _():
        for k in range(n_slots):
            tile_bias = slopes[k // 2] * (-qi * tq).astype(F32)
            m_new, p = softmax_tile(_dot(qa[k], kta_ref[k, 0]), m_ref[k], tile_bias)
            m_ref[k] = m_new
            pb_ref[k] = p
        acc_ref[...] = jnp.zeros(acc_ref.shape, F32)

    def pair(i, carry):
        j = odd + 2 * i
        update(j, pb_ref, pa_ref)

        @pl.when(j + 1 < qi)
        def _():
            update(j + 1, pa_ref, pb_ref)

        return carry

    lax.fori_loop(0, qi // 2, pair, 0)
    update(qi, pb_ref, None, diagonal=True)

    lam = (jnp.exp(jnp.sum(lq1[...] * lk1[...], axis=1, keepdims=True))
           - jnp.exp(jnp.sum(lq2[...] * lk2[...], axis=1, keepdims=True)) + LAMBDA_INIT)
    for g in range(nh):
        a0, a1 = acc_ref[2 * g], acc_ref[2 * g + 1]
        o = a0[:, :D_HEAD_V] / a0[:, D_HEAD_V:] - lam * (a1[:, :D_HEAD_V] / a1[:, D_HEAD_V:])
        o_ref[0, :, g * LANES:(g + 1) * LANES] = (_rms(o, g_ref[...]) * (1.0 - LAMBDA_INIT)).astype(o_ref.dtype)


def _attn_prompt(slopes, q, kt, v, lq1, lk1, lq2, lk2, g):
    b, t, _ = q.shape
    n_kt, tq = kt.shape[1], kt.shape[3]
    nh = ATTN_HEADS_PER_STEP
    w = nh * LANES
    small = lambda a: pl.BlockSpec(a.shape, lambda bi, hi, qi: (0, 0))
    return pl.pallas_call(
        functools.partial(_attn_prompt_kernel, tq=tq, nh=nh),
        grid=(b, N_HEADS // nh, t // tq),
        in_specs=[pl.BlockSpec(memory_space=pltpu.SMEM),
                  pl.BlockSpec((1, tq, w), lambda bi, hi, qi: (bi, qi, hi)),
                  pl.BlockSpec((1, n_kt, w, tq), lambda bi, hi, qi: (bi, 0, hi, 0)),
                  pl.BlockSpec((1, t, w), lambda bi, hi, qi: (bi, 0, hi)),
                  small(lq1), small(lk1), small(lq2), small(lk2), small(g)],
        out_specs=pl.BlockSpec((1, tq, w), lambda bi, hi, qi: (bi, qi, hi)),
        out_shape=jax.ShapeDtypeStruct((b, t, D_ATT), BF16),
        scratch_shapes=[pltpu.VMEM((2 * nh, n_kt, LANES, tq), BF16), pltpu.VMEM((nh, t, 2 * D_HEAD_V), BF16),
                        pltpu.VMEM((nh, tq, tq), F32), pltpu.VMEM((2 * nh, tq, LANES), F32),
                        pltpu.VMEM((2 * nh, tq, 2 * D_HEAD_V), F32)] + [pltpu.VMEM((2 * nh, tq, tq), BF16)] * 2,
        compiler_params=pltpu.CompilerParams(dimension_semantics=("arbitrary",) * 3, vmem_limit_bytes=VMEM_LIMIT),
        name="attn_prompt",
    )(slopes, q, kt, v, lq1, lk1, lq2, lk2, g)


def _attn_sample_kernel(slopes_ref, q_ref, ckt_ref, cv_ref, kn_ref, vn_ref, lq1, lk1, lq2, lk2, g_ref, o_ref,
                        m_ref, l_ref, acc_ref, *, past, tk):
    j = pl.program_id(1)
    tq = q_ref.shape[1]

    @pl.when(j == 0)
    def _():
        _init_softmax_state(m_ref, l_ref, acc_ref)

    def attend(n, k_start, n_valid, scores, values):
        q_pos = past + lax.broadcasted_iota(jnp.int32, (tq, n), 0)
        k_pos = k_start + lax.broadcasted_iota(jnp.int32, (tq, n), 1)
        dist = jnp.abs(q_pos - k_pos).astype(F32)
        dist = jnp.concatenate([dist, dist], axis=0)
        visible = ((q_pos // CHUNK) >= (k_pos // CHUNK)) & (k_pos < k_start + n_valid)
        visible = jnp.concatenate([visible, visible], axis=0)
        state = [(m_ref[h], l_ref[h], acc_ref[h]) for h in range(N_HEADS)]
        for h in range(N_HEADS):
            m_prev, l_prev, acc_prev = state[h]
            q2 = jnp.concatenate(_split_maps(q_ref[0, :, h * LANES:(h + 1) * LANES]), axis=0)
            s = jnp.where(visible, scores(q2, h) - (slopes_ref[h] * LOG2E) * dist, NEG_INF)
            m_new = jnp.maximum(m_prev, jnp.max(s, axis=1, keepdims=True))
            alpha = jnp.exp2(m_prev - m_new)
            ps = [jnp.exp2(s[:, b * LANES:(b + 1) * LANES] - m_new) for b in range(n // LANES)]
            p = jnp.concatenate(ps, axis=1).astype(BF16)
            state[h] = (m_new, alpha * l_prev + functools.reduce(lambda x, y: x + y, ps),
                        alpha * acc_prev + _dot(p, values(h)))
        for h in range(N_HEADS):
            m_ref[h], l_ref[h], acc_ref[h] = state[h]

    attend(tk, j * tk, tk,
           lambda q2, h: _dot(q2, ckt_ref[0, h * LANES:(h + 1) * LANES, :].astype(BF16)),
           lambda h: cv_ref[0, pl.ds(h, tk, stride=N_HEADS), :].astype(BF16))

    @pl.when(j == pl.num_programs(1) - 1)
    def _():
        attend(kn_ref.shape[1], past, tq,
               lambda q2, h: _dot_nt(q2, kn_ref[0, :, h * LANES:(h + 1) * LANES]),
               lambda h: vn_ref[0, :, h * LANES:(h + 1) * LANES])
        lam = (jnp.exp(jnp.sum(lq1[...] * lk1[...], axis=1, keepdims=True))
               - jnp.exp(jnp.sum(lq2[...] * lk2[...], axis=1, keepdims=True)) + LAMBDA_INIT)
        for h in range(N_HEADS):
            sm = acc_ref[h] / jnp.sum(l_ref[h], axis=1, keepdims=True)
            o = _rms(sm[:tq] - lam * sm[tq:], g_ref[...]) * (1.0 - LAMBDA_INIT)
            o_ref[0, :, h * LANES:(h + 1) * LANES] = o.astype(o_ref.dtype)


def _attn_sample(slopes, q, cache_kt, cache_v_rows, k_new, v_new, lq1, lk1, lq2, lk2, g):
    b, t, _ = q.shape
    past = cache_kt.shape[2]
    tk = min(CACHE_TILE, past)
    k_new = jnp.pad(k_new, ((0, 0), (0, -t % LANES), (0, 0)))
    v_new = jnp.pad(v_new, ((0, 0), (0, -t % LANES), (0, 0)))
    small = lambda a: pl.BlockSpec(a.shape, lambda bi, j: (0, 0))
    per_stream = lambda a: pl.BlockSpec((1,) + a.shape[1:], lambda bi, j: (bi, 0, 0))
    return pl.pallas_call(
        functools.partial(_attn_sample_kernel, past=past, tk=tk),
        grid=(b, past // tk),
        in_specs=[pl.BlockSpec(memory_space=pltpu.SMEM),
                  per_stream(q),
                  pl.BlockSpec((1, D_ATT, tk), lambda bi, j: (bi, 0, j)),
                  pl.BlockSpec((1, tk * N_HEADS, D_HEAD_V), lambda bi, j: (bi, j, 0)),
                  per_stream(k_new), per_stream(v_new),
                  small(lq1), small(lk1), small(lq2), small(lk2), small(g)],
        out_specs=pl.BlockSpec((1, t, D_ATT), lambda bi, j: (bi, 0, 0)),
        out_shape=jax.ShapeDtypeStruct((b, t, D_ATT), BF16),
        scratch_shapes=[pltpu.VMEM((N_HEADS, 2 * t, LANES), F32)] * 3,
        compiler_params=pltpu.CompilerParams(dimension_semantics=("arbitrary",) * 2, vmem_limit_bytes=VMEM_LIMIT),
        name="attn_sample",
    )(slopes, q, cache_kt, cache_v_rows, k_new, v_new, lq1, lk1, lq2, lk2, g)


def _block_diag_dense(w):
    n, c, _ = w.shape
    eye = jnp.eye(n, dtype=w.dtype)
    return (eye[:, None, :, None] * w[:, :, None, :]).reshape(n * c, n * c)


def kernel(x_prompt, x_sample, cache_k, cache_v, state_lru_h, state_conv, w_in, w_out, lambda_q1, lambda_k1,
           lambda_q2, lambda_k2, subln_g, conv_w, conv_b, w_rgate, b_rgate, w_igate, b_igate, lru_lambda,
           ffn1_w_gate, ffn1_w_up, ffn1_w_down, ffn2_w_gate, ffn2_w_up, ffn2_w_down,
           g_ffn1_pre, g_ffn1_post, g_mix_pre, g_mix_post, g_ffn2_pre, g_ffn2_post):
    bp, tp, _ = x_prompt.shape
    bs, ts, _ = x_sample.shape
    past = cache_k.shape[2]
    assert w_in.shape[0] == 1, "one layer: LAMBDA_INIT is the depth-0 value"
    assert tp % ROW_TILE == 0 and (bp * tp) % OUT_ROW_TILE == 0
    assert bs * ts <= ROW_TILE and ts % (2 * SUBLANES) == 0
    assert past % min(CACHE_TILE, past) == 0

    wg1, wu1, wd1 = ffn1_w_gate[0].astype(BF16), ffn1_w_up[0].astype(BF16), ffn1_w_down[0].astype(BF16)
    win = w_in[0].astype(BF16)
    wq, wkt, wv, wl = win[:, :D_ATT], win[:, D_ATT:2 * D_ATT].T, win[:, 2 * D_ATT:3 * D_ATT], win[:, 3 * D_ATT:]
    wr = _block_diag_dense(w_rgate[0]).astype(BF16)
    wi = _block_diag_dense(w_igate[0]).astype(BF16)
    slopes = jnp.asarray(2.0 ** (-8.0 * np.arange(1, N_HEADS + 1) / N_HEADS), dtype=F32)
    lam_vecs = (lambda_q1, lambda_k1, lambda_q2, lambda_k2)
    ffn1 = (g_ffn1_pre, g_ffn1_post, g_mix_pre, wg1, wu1, wd1, wq, wkt, wv, wl)
    lru_params = (conv_w[0], conv_b, wr, b_rgate, wi, b_igate, lru_lambda)
    tail_pad = 8 - (CONV_W - 1)

    def conv_state(conv_hist, lx, b, t):
        return jnp.concatenate([conv_hist, lx.reshape(b, t, D_LRU)], axis=1)[:, -(CONV_W - 1):][None]

    def pad_hist(conv_hist):
        return jnp.pad(conv_hist, ((0, 0), (tail_pad, 0), (0, 0)))

    hist_p = jnp.zeros((bp, CONV_W - 1, D_LRU), F32)
    x1, qb, kt, ktb, v_rows, vb, lx, lru_out, hp, wg2, wu2, wd2, wo = _ffn_inproj(
        x_prompt.reshape(bp * tp, D_MODEL), *ffn1, seq_len=tp,
        lru_state=(jnp.zeros((bp, 1, D_LRU), F32), pad_hist(hist_p)), lru_params=lru_params,
        later_weights=(ffn2_w_gate[0], ffn2_w_up[0], ffn2_w_down[0], w_out[0]))
    ffn2 = (wo, g_mix_post, g_ffn2_pre, g_ffn2_post, wg2, wu2, wd2)
    att_p = _attn_prompt(slopes, qb.reshape(bp, tp, D_ATT), ktb, vb.reshape(bp, tp, D_ATT), *lam_vecs, subln_g)
    x1_p, lru_p = x1, lru_out
    hp, cp = hp.reshape(1, bp, D_LRU), conv_state(hist_p, lx, bp, tp)
    kp = kt.reshape(bp, N_HEADS, 2, D_HEAD_QK, tp).transpose(0, 4, 1, 2, 3)[None]
    vp = v_rows.reshape(1, bp, tp, N_HEADS, D_HEAD_V)

    x1, qb, kf, kb, vf, vb, lx, lg = _ffn_inproj(x_sample.reshape(bs * ts, D_MODEL), *ffn1)
    cache_kt = cache_k[0].transpose(0, 2, 3, 4, 1).reshape(bs, D_ATT, past)
    cache_v_rows = cache_v[0].reshape(bs, past * N_HEADS, D_HEAD_V)
    att = _attn_sample(slopes, qb.reshape(bs, ts, D_ATT), cache_kt, cache_v_rows, kb.reshape(bs, ts, D_ATT),
                       vb.reshape(bs, ts, D_ATT), *lam_vecs, subln_g)
    lru_out, hs = _lru(lx.reshape(bs, ts, D_LRU), lg.reshape(bs, ts, D_LRU), state_lru_h[0].reshape(bs, 1, D_LRU),
                       pad_hist(state_conv[0]), *lru_params)
    yp, ys = _outproj_ffn(att_p.reshape(bp * tp, D_ATT), lru_p, x1_p, *ffn2,
                          small_stream=(att.reshape(bs * ts, D_ATT), lru_out.reshape(bs * ts, D_LRU), x1))
    yp = yp.reshape(bp, tp, D_MODEL)
    ys, hs, cs = ys.reshape(bs, ts, D_MODEL), hs.reshape(1, bs, D_LRU), conv_state(state_conv[0], lx, bs, ts)
    ks = kf.reshape(1, bs, ts, N_HEADS, 2, D_HEAD_QK)
    vs = vf.reshape(1, bs, ts, N_HEADS, D_HEAD_V)
    return (yp, ys, kp, vp, hp, cp, ks, vs, hs, cs)
```

```python
import functools
import math

import jax
import jax.numpy as jnp
import numpy as np
from jax import lax
from jax.experimental import pallas as pl
from jax.experimental.pallas import tpu as pltpu

F32 = jnp.float32
BF16 = jnp.bfloat16

D_MODEL = 1024
D_ATT = 512
D_LRU = 512
N_HEADS = 4
D_HEAD_V = 128
D_HEAD_QK = 64
CONV_W = 4
LRU_C = 8.0
D_FF = 2816
CHUNK = 64
RMS_EPS = 1e-6
NEG_INF = -1e30
LAMBDA_INIT = 0.8 - 0.6 * math.exp(-0.3 * 0)
LOG2E = math.log2(math.e)

LANES = 128
SUBLANES = 8
FF_CHUNK = 256
N_FF_CHUNKS = D_FF // FF_CHUNK
ROW_TILE = 512
N_LATER_WEIGHTS = 4
OUT_ROW_TILE = 1024
ATTN_HEADS_PER_STEP = 2
LRU_TILE = 1024
CACHE_TILE = 4096
VMEM_LIMIT = 56 * 1024 * 1024


def _dot(a, b):
    return jnp.dot(a, b, preferred_element_type=F32)


def _dot_nt(a, b):
    return lax.dot_general(a, b, (((1,), (1,)), ((), ())), preferred_element_type=F32)


def _rms(x, g):
    return x * lax.rsqrt(jnp.mean(x * x, axis=-1, keepdims=True) + RMS_EPS) * g


def _swiglu_ffn(x, g_pre, g_post, wg_ref, wu_ref, wd_ref):
    xn = _rms(x, g_pre).astype(BF16)
    acc = jnp.zeros(x.shape, F32)
    for j in range(N_FF_CHUNKS):
        cols = slice(j * FF_CHUNK, (j + 1) * FF_CHUNK)
        g = _dot(xn, wg_ref[:, cols])
        u = _dot(xn, wu_ref[:, cols])
        h = (g * jax.nn.sigmoid(g) * u).astype(BF16)
        acc = acc + _dot(h, wd_ref[cols, :])
    return x + 0.5 * _rms(acc, g_post)


def _resident(shape):
    nd = len(shape)
    return pl.BlockSpec(shape, lambda i: (0,) * nd, pipeline_mode=pl.Buffered(1))


def _sigmoid(x):
    return 0.5 * (1.0 + jnp.tanh(0.5 * x))


def _lru_begin(is_first, h0, cbuf, tt, xbuf_ref, h_ref, *_):
    pad = xbuf_ref.shape[0] - tt

    @pl.when(is_first)
    def _():
        xbuf_ref[0:pad, :] = cbuf
        h_ref[...] = h0

    @pl.when(jnp.logical_not(is_first))
    def _():
        xbuf_ref[0:pad, :] = xbuf_ref[tt:tt + pad, :]


def _lru_tile(x, gate, cw, cb, wr, br, wi, bi, lam, store_out, xbuf_ref, h_ref, a_ref, b_ref, p_ref, hs_ref):
    tt = x.shape[0]
    pad = xbuf_ref.shape[0] - tt
    n_seg = min(SUBLANES, tt // SUBLANES)
    seg = tt // n_seg
    xbuf_ref[pad:pad + tt, :] = x
    xc = cb + xbuf_ref[pad:pad + tt, :] * cw[CONV_W - 1:CONV_W, :]
    for j in range(CONV_W - 1):
        back = CONV_W - 1 - j
        xc = xc + xbuf_ref[pad - back:pad - back + tt, :] * cw[j:j + 1, :]

    xb = xc.astype(BF16)
    r = _sigmoid(_dot(xb, wr) + br)
    i = _sigmoid(_dot(xb, wi) + bi)
    neg_lam = -lam
    softplus = jnp.maximum(neg_lam, 0.0) + jnp.log1p(jnp.exp(-jnp.abs(neg_lam)))
    log_a = -LRU_C * r * softplus
    a = jnp.exp(log_a)
    u = -jnp.tanh(log_a) * (1.0 + a * a)
    b = jnp.where(u > 0.0, u * lax.rsqrt(u), 0.0) * (i * xc)
    n_blk = D_LRU // LANES
    pitch = a_ref.shape[1] // n_seg
    for s in range(n_seg):
        for l in range(n_blk):
            a_ref[l, s * pitch:s * pitch + seg, :] = a[s * seg:(s + 1) * seg, l * LANES:(l + 1) * LANES]
            b_ref[l, s * pitch:s * pitch + seg, :] = b[s * seg:(s + 1) * seg, l * LANES:(l + 1) * LANES]

    h = [jnp.zeros((n_seg, LANES), F32)] * n_blk
    p = [jnp.ones((n_seg, LANES), F32)] * n_blk
    for j in range(seg):
        rows = pl.ds(j, n_seg, stride=pitch)
        for l in range(n_blk):
            a_j = a_ref[l, rows, :]
            h[l] = a_j * h[l] + b_ref[l, rows, :]
            p[l] = a_j * p[l]
            hs_ref[l, rows, :] = h[l]
            p_ref[l, rows, :] = p[l]

    h_end = jnp.concatenate(h, axis=1)
    p_end = jnp.concatenate(p, axis=1)
    carry = h_ref[...]
    for s in range(n_seg):
        rows = slice(s * pitch, s * pitch + seg)
        p_rows = jnp.concatenate([p_ref[l, rows, :] for l in range(n_blk)], axis=1)
        hs_rows = jnp.concatenate([hs_ref[l, rows, :] for l in range(n_blk)], axis=1)
        hs = p_rows * carry + hs_rows
        g = gate[s * seg:(s + 1) * seg, :]
        gelu = 0.5 * g * (1.0 + jnp.tanh(math.sqrt(2.0 / math.pi) * (g + 0.044715 * (g * g * g))))
        store_out(slice(s * seg, (s + 1) * seg), hs * gelu)
        carry = p_end[s:s + 1, :] * carry + h_end[s:s + 1, :]
    h_ref[...] = carry
    return carry


def _lru_scratch(tt):
    n_seg = min(SUBLANES, tt // SUBLANES)
    return ([pltpu.VMEM((tt + SUBLANES, D_LRU), F32), pltpu.VMEM((1, D_LRU), F32)]
            + [pltpu.VMEM((D_LRU // LANES, tt + 4 * n_seg, LANES), F32)] * 4)


def _lru_kernel(x_ref, gate_ref, h0_ref, cbuf_ref, cw_ref, cb_ref, wr_ref, br_ref, wi_ref, bi_ref, lam_ref,
                out_ref, hlast_ref, *scratch):
    def store_out(rows, value):
        out_ref[0, rows, :] = value.astype(out_ref.dtype)

    _lru_begin(pl.program_id(1) == 0, h0_ref[0], cbuf_ref[0], x_ref.shape[1], *scratch)
    hlast_ref[0] = _lru_tile(x_ref[0], gate_ref[0], cw_ref[...], cb_ref[...], wr_ref[...], br_ref[...],
                             wi_ref[...], bi_ref[...], lam_ref[...], store_out, *scratch)


def _lru(x, gate, h0, cbuf, cw, cb, wr, br, wi, bi, lam):
    b, t, _ = x.shape
    tt = min(LRU_TILE, t)
    small = lambda a: pl.BlockSpec(a.shape, lambda bi_, ti: (0, 0))
    tile = pl.BlockSpec((1, tt, D_LRU), lambda bi_, ti: (bi_, ti, 0))
    per_stream = lambda a: pl.BlockSpec((1,) + a.shape[1:], lambda bi_, ti: (bi_, 0, 0))
    return pl.pallas_call(
        _lru_kernel,
        grid=(b, t // tt),
        in_specs=[tile, tile, per_stream(h0), per_stream(cbuf), small(cw), small(cb),
                  small(wr), small(br), small(wi), small(bi), small(lam)],
        out_specs=(tile, pl.BlockSpec((1, 1, D_LRU), lambda bi_, ti: (bi_, 0, 0))),
        out_shape=(jax.ShapeDtypeStruct((b, t, D_LRU), BF16), jax.ShapeDtypeStruct((b, 1, D_LRU), F32)),
        scratch_shapes=_lru_scratch(tt),
        compiler_params=pltpu.CompilerParams(dimension_semantics=("parallel", "arbitrary"), vmem_limit_bytes=VMEM_LIMIT),
        name="lru",
    )(x, gate, h0, cbuf, cw, cb, wr, br, wi, bi, lam)


def _ffn_inproj_kernel(x_ref, g1a_ref, g1b_ref, gma_ref, wg_ref, wu_ref, wd_ref, wq_ref, wkt_ref, wv_ref, wl_ref,
                       *refs, seq_tiles):
    tm = x_ref.shape[0]
    if seq_tiles is None:
        x1_ref, qb_ref, kf_ref, kb_ref, vf_ref, vb_ref, lx_ref, lg_ref = refs
    else:
        n_cast = N_LATER_WEIGHTS
        (h0_ref, cbuf_ref, cw_ref, cb_ref, wr_ref, br_ref, wi_ref, bi_ref, lam_ref, *rest) = refs
        cast_in, rest = rest[:n_cast], rest[n_cast:]
        (x1_ref, qb_ref, kf_ref, kb_ref, vf_ref, vb_ref, lx_ref, lo_ref, hlast_ref, *rest) = rest
        cast_out, scratch = rest[:n_cast], rest[n_cast:]
        for src, dst in zip(cast_in, cast_out):
            dst[...] = src[...].astype(BF16)
        _lru_begin(pl.program_id(0) % seq_tiles == 0, h0_ref[0], cbuf_ref[0], tm, *scratch)
    x1 = _swiglu_ffn(x_ref[...], g1a_ref[...], g1b_ref[...], wg_ref, wu_ref, wd_ref)
    x1_ref[...] = x1
    xm = _rms(x1, gma_ref[...]).astype(BF16)
    lru = _dot(xm, wl_ref[...])
    lx_ref[...] = lru[:, :D_LRU]
    qb_ref[...] = (_dot(xm, wq_ref[...]) * (LOG2E / math.sqrt(D_HEAD_QK))).astype(BF16)
    v = _dot(xm, wv_ref[...])
    vb_ref[...] = v.astype(BF16)
    if seq_tiles is None:
        lg_ref[...] = lru[:, D_LRU:]
        k = _dot_nt(xm, wkt_ref[...])
        kf_ref[...] = k
        kb_ref[...] = k.astype(BF16)
        vf_ref[...] = v
    else:
        kt = _dot_nt(wkt_ref[...], xm)
        kf_ref[0] = kt
        kb_ref[0, 0] = kt.astype(BF16)
        for h in range(N_HEADS):
            vf_ref[pl.ds(h, tm, stride=N_HEADS), :] = v[:, h * D_HEAD_V:(h + 1) * D_HEAD_V]

        def store_out(rows, value):
            lo_ref[rows, :] = value.astype(lo_ref.dtype)

        hlast_ref[0] = _lru_tile(lru[:, :D_LRU], lru[:, D_LRU:], cw_ref[...], cb_ref[...], wr_ref[...], br_ref[...],
                                 wi_ref[...], bi_ref[...], lam_ref[...], store_out, *scratch)


def _ffn_inproj(x, g1a, g1b, gma, wg, wu, wd, wq, wkt, wv, wl, seq_len=None, lru_state=None, lru_params=None,
                later_weights=()):
    m = x.shape[0]
    tm = min(ROW_TILE, m)
    rows = lambda d: pl.BlockSpec((tm, d), lambda i: (i, 0))
    row_out = lambda d, dt: jax.ShapeDtypeStruct((m, d), dt)
    weights = (g1a, g1b, gma, wg, wu, wd, wq, wkt, wv, wl)
    in_specs = [rows(D_MODEL)] + [_resident(w.shape) for w in weights]
    operands = (x, *weights)
    scratch = []
    if seq_len is not None:
        nt = seq_len // tm
        b = m // seq_len
        k_shapes = (jax.ShapeDtypeStruct((b, D_ATT, seq_len), F32), jax.ShapeDtypeStruct((b, nt, D_ATT, tm), BF16))
        k_specs = (pl.BlockSpec((1, D_ATT, tm), lambda i: (i // nt, 0, i % nt)),
                   pl.BlockSpec((1, 1, D_ATT, tm), lambda i: (i // nt, i % nt, 0, 0)))
        vf_shape = jax.ShapeDtypeStruct((m * N_HEADS, D_HEAD_V), F32)
        vf_spec = pl.BlockSpec((tm * N_HEADS, D_HEAD_V), lambda i: (i, 0))
        per_seq = lambda a: pl.BlockSpec((1,) + a.shape[1:], lambda i: (i // nt, 0, 0))
        tail_shapes = (row_out(D_LRU, F32), row_out(D_LRU, BF16), jax.ShapeDtypeStruct((b, 1, D_LRU), F32))
        tail_specs = (rows(D_LRU), rows(D_LRU), pl.BlockSpec((1, 1, D_LRU), lambda i: (i // nt, 0, 0)))
        assert len(later_weights) == N_LATER_WEIGHTS
        n_steps = m // tm
        cast_specs, cast_shapes = [], []
        for w in later_weights:
            r = next(r for r in range(2 * SUBLANES, w.shape[0] + 1, 2 * SUBLANES)
                     if w.shape[0] % r == 0 and w.shape[0] // r <= n_steps)
            last = w.shape[0] // r - 1
            cast_specs.append(pl.BlockSpec((r, w.shape[1]), lambda i, last=last: (jnp.minimum(i, last), 0)))
            cast_shapes.append(jax.ShapeDtypeStruct(w.shape, BF16))
        in_specs += [per_seq(a) for a in lru_state] + [_resident(p.shape) for p in lru_params] + cast_specs
        operands += (*lru_state, *lru_params, *later_weights)
        tail_shapes += tuple(cast_shapes)
        tail_specs += tuple(cast_specs)
        scratch = _lru_scratch(tm)
        seq_tiles = nt
    else:
        k_shapes = (row_out(D_ATT, F32), row_out(D_ATT, BF16))
        k_specs = (rows(D_ATT), rows(D_ATT))
        vf_shape = row_out(D_ATT, F32)
        vf_spec = rows(D_ATT)
        tail_shapes = (row_out(D_LRU, F32), row_out(D_LRU, F32))
        tail_specs = (rows(D_LRU), rows(D_LRU))
        seq_tiles = None
    out_shape = (row_out(D_MODEL, F32), row_out(D_ATT, BF16), *k_shapes, vf_shape, row_out(D_ATT, BF16), *tail_shapes)
    out_specs = (rows(D_MODEL), rows(D_ATT), *k_specs, vf_spec, rows(D_ATT), *tail_specs)
    return pl.pallas_call(
        functools.partial(_ffn_inproj_kernel, seq_tiles=seq_tiles),
        grid=(m // tm,),
        in_specs=in_specs,
        out_specs=out_specs,
        out_shape=out_shape,
        scratch_shapes=scratch,
        compiler_params=pltpu.CompilerParams(dimension_semantics=("arbitrary",), vmem_limit_bytes=VMEM_LIMIT),
        name="ffn_inproj",
    )(*operands)


def _outproj_ffn_kernel(att_ref, lru_ref, x1_ref, wo_ref, gmb_ref, g2a_ref, g2b_ref,
                        wg_ref, wu_ref, wd_ref, y_ref):
    mix = _dot(att_ref[...], wo_ref[0:D_ATT, :]) + _dot(lru_ref[...], wo_ref[D_ATT:D_ATT + D_LRU, :])
    x2 = x1_ref[...] + _rms(mix, gmb_ref[...])
    y_ref[...] = _swiglu_ffn(x2, g2a_ref[...], g2b_ref[...], wg_ref, wu_ref, wd_ref)


def _outproj_ffn(att, lru, x1, wo, gmb, g2a, g2b, wg, wu, wd):
    m = x1.shape[0]
    tm = min(OUT_ROW_TILE, m)
    rows = lambda d: pl.BlockSpec((tm, d), lambda i: (i, 0))
    weights = (wo, gmb, g2a, g2b, wg, wu, wd)
    return pl.pallas_call(
        _outproj_ffn_kernel,
        grid=(m // tm,),
        in_specs=[rows(D_ATT), rows(D_LRU), rows(D_MODEL)] + [_resident(w.shape) for w in weights],
        out_specs=rows(D_MODEL),
        out_shape=jax.ShapeDtypeStruct((m, D_MODEL), F32),
        compiler_params=pltpu.CompilerParams(dimension_semantics=("parallel",), vmem_limit_bytes=VMEM_LIMIT),
        name="outproj_ffn",
    )(att, lru, x1, *weights)


def _split_maps(q):
    lane = lax.broadcasted_iota(jnp.int32, q.shape, 1)
    zero = jnp.zeros_like(q)
    return jnp.where(lane < D_HEAD_QK, q, zero), jnp.where(lane >= D_HEAD_QK, q, zero)


def _init_softmax_state(m_ref, l_ref, acc_ref):
    m_ref[...] = jnp.full(m_ref.shape, NEG_INF, F32)
    l_ref[...] = jnp.zeros(l_ref.shape, F32)
    acc_ref[...] = jnp.zeros(acc_ref.shape, F32)


def _bf16_split3(x):
    hi = x.astype(BF16)
    r1 = x - hi.astype(F32)
    lo = r1.astype(BF16)
    lo2 = (r1 - lo.astype(F32)).astype(BF16)
    return hi, lo, lo2


def _attn_prompt_kernel(slopes_ref, q_ref, kt_ref, v_ref, lq1, lk1, lq2, lk2, g_ref, o_ref,
                        kta_ref, va_ref, mask_ref, m_ref, acc_ref, pa_ref, pb_ref, *, tq, nh):
    hg = pl.program_id(1)
    qi = pl.program_id(2)
    n_kt = kt_ref.shape[1]
    half = D_HEAD_QK
    slopes = [slopes_ref[hg * nh + g] * LOG2E for g in range(nh)]
    n_slots = 2 * nh

    @pl.when(qi == 0)
    def _():
        row = lax.broadcasted_iota(jnp.int32, (LANES, tq), 0)
        col = lax.broadcasted_iota(jnp.int32, (1, tq), 1)
        qrow = lax.broadcasted_iota(jnp.int32, (tq, tq), 0)
        kcol = lax.broadcasted_iota(jnp.int32, (tq, tq), 1)
        zero = jnp.zeros((LANES, tq), F32)
        for g in range(nh):
            terms = _bf16_split3(slopes[g] * col.astype(F32))

            def bias_rows(first):
                blk = zero
                for i, t in enumerate(terms):
                    blk = jnp.where(row == first + i, jnp.broadcast_to(t.astype(F32), (LANES, tq)), blk)
                return blk

            bias0, bias1 = bias_rows(half), bias_rows(0)

            def fill(n, carry):
                kt = kt_ref[0, n, g * LANES:(g + 1) * LANES, :].astype(F32)
                kta_ref[2 * g, n] = jnp.where(row < half, kt, bias0).astype(BF16)
                kta_ref[2 * g + 1, n] = jnp.where(row >= half, kt, bias1).astype(BF16)
                r0 = pl.multiple_of(n * tq, tq)
                va_ref[g, pl.ds(r0, tq), 0:D_HEAD_V] = v_ref[0, pl.ds(r0, tq), g * LANES:(g + 1) * LANES]
                va_ref[g, pl.ds(r0, tq), D_HEAD_V:2 * D_HEAD_V] = jnp.ones((tq, D_HEAD_V), BF16)
                return carry

            lax.fori_loop(0, n_kt, fill, 0)
            mask_ref[g] = jnp.where((qrow // CHUNK) >= (kcol // CHUNK),
                                    slopes[g] * jnp.minimum(2 * (qrow - kcol), 0).astype(F32), NEG_INF)
        acc_ref[...] = jnp.zeros(acc_ref.shape, F32)
        pb_ref[...] = jnp.zeros(pb_ref.shape, BF16)

    lane = lax.broadcasted_iota(jnp.int32, (tq, LANES), 1)
    ones0 = ((lane >= half) & (lane < half + 3)).astype(F32).astype(BF16)
    ones1 = (lane < 3).astype(F32).astype(BF16)
    qa = []
    for g in range(nh):
        q = q_ref[0, :, g * LANES:(g + 1) * LANES]
        qa += [jnp.where(lane < half, q, ones0), jnp.where(lane >= half, q, ones1)]
    m_ref[...] = jnp.full(m_ref.shape, NEG_INF, F32)

    def values(g, j):
        return va_ref[g, pl.ds(pl.multiple_of(j * tq, tq), tq), :]

    def softmax_tile(s, m_prev, tile_bias):
        m_new = jnp.maximum(m_prev, jnp.max(s, axis=1, keepdims=True) + tile_bias)
        shift = m_new - tile_bias
        p = jnp.concatenate([jnp.exp2((s[:, b * LANES:(b + 1) * LANES] - shift).astype(BF16))
                             for b in range(s.shape[1] // LANES)], axis=1)
        return m_new, p

    def update(j, p_in_ref, p_out_ref, diagonal=False):
        logits = [_dot(qa[k], kta_ref[k, j]) for k in range(n_slots)]
        if diagonal:
            logits = [logits[k] + mask_ref[k // 2] for k in range(n_slots)]
        j_prev = jnp.maximum(j - 1, 0)
        pv = [_dot(p_in_ref[k], values(k // 2, j_prev)) for k in range(n_slots)]
        m_prev = [m_ref[k] for k in range(n_slots)]
        acc_prev = [acc_ref[k] for k in range(n_slots)]
        m_next, acc_next, p_next = [], [], []
        for k in range(n_slots):
            tile_bias = slopes[k // 2] * ((j - qi) * tq).astype(F32)
            m_new, p = softmax_tile(logits[k], m_prev[k], tile_bias)
            alpha = jnp.exp2(m_prev[k] - m_new)
            acc = jnp.concatenate([alpha, alpha], axis=1) * (acc_prev[k] + pv[k])
            if diagonal:
                acc = acc + _dot(p, values(k // 2, j))
            acc_next.append(acc)
            m_next.append(m_new)
            p_next.append(p)
        for k in range(n_slots):
            acc_ref[k] = acc_next[k]
            m_ref[k] = m_next[k]
            if not diagonal:
                p_out_ref[k] = p_next[k]

    odd = qi % 2

    @pl.when(odd == 1)
    def _():
        for k in range(n_slots):
            tile_bias = slopes[k // 2] * (-qi * tq).astype(F32)
            m_new, p = softmax_tile(_dot(qa[k], kta_ref[k, 0]), m_ref[k], tile_bias)
            m_ref[k] = m_new
            pb_ref[k] = p
        acc_ref[...] = jnp.zeros(acc_ref.shape, F32)

    def pair(i, carry):
        j = odd + 2 * i
        update(j, pb_ref, pa_ref)

        @pl.when(j + 1 < qi)
        def _():
            update(j + 1, pa_ref, pb_ref)

        return carry

    lax.fori_loop(0, qi // 2, pair, 0)
    update(qi, pb_ref, None, diagonal=True)

    lam = (jnp.exp(jnp.sum(lq1[...] * lk1[...], axis=1, keepdims=True))
           - jnp.exp(jnp.sum(lq2[...] * lk2[...], axis=1, keepdims=True)) + LAMBDA_INIT)
    for g in range(nh):
        a0, a1 = acc_ref[2 * g], acc_ref[2 * g + 1]
        o = a0[:, :D_HEAD_V] / a0[:, D_HEAD_V:] - lam * (a1[:, :D_HEAD_V] / a1[:, D_HEAD_V:])
        o_ref[0, :, g * LANES:(g + 1) * LANES] = (_rms(o, g_ref[...]) * (1.0 - LAMBDA_INIT)).astype(o_ref.dtype)


def _attn_prompt(slopes, q, kt, v, lq1, lk1, lq2, lk2, g):
    b, t, _ = q.shape
    n_kt, tq = kt.shape[1], kt.shape[3]
    nh = ATTN_HEADS_PER_STEP
    w = nh * LANES
    small = lambda a: pl.BlockSpec(a.shape, lambda bi, hi, qi: (0, 0))
    return pl.pallas_call(
        functools.partial(_attn_prompt_kernel, tq=tq, nh=nh),
        grid=(b, N_HEADS // nh, t // tq),
        in_specs=[pl.BlockSpec(memory_space=pltpu.SMEM),
                  pl.BlockSpec((1, tq, w), lambda bi, hi, qi: (bi, qi, hi)),
                  pl.BlockSpec((1, n_kt, w, tq), lambda bi, hi, qi: (bi, 0, hi, 0)),
                  pl.BlockSpec((1, t, w), lambda bi, hi, qi: (bi, 0, hi)),
                  small(lq1), small(lk1), small(lq2), small(lk2), small(g)],
        out_specs=pl.BlockSpec((1, tq, w), lambda bi, hi, qi: (bi, qi, hi)),
        out_shape=jax.ShapeDtypeStruct((b, t, D_ATT), BF16),
        scratch_shapes=[pltpu.VMEM((2 * nh, n_kt, LANES, tq), BF16), pltpu.VMEM((nh, t, 2 * D_HEAD_V), BF16),
                        pltpu.VMEM((nh, tq, tq), F32), pltpu.VMEM((2 * nh, tq, LANES), F32),
                        pltpu.VMEM((2 * nh, tq, 2 * D_HEAD_V), F32)] + [pltpu.VMEM((2 * nh, tq, tq), BF16)] * 2,
        compiler_params=pltpu.CompilerParams(dimension_semantics=("parallel", "parallel", "arbitrary"),
                                             vmem_limit_bytes=VMEM_LIMIT),
        name="attn_prompt",
    )(slopes, q, kt, v, lq1, lk1, lq2, lk2, g)


def _attn_sample_kernel(slopes_ref, q_ref, ckt_ref, cv_ref, kn_ref, vn_ref, lq1, lk1, lq2, lk2, g_ref, o_ref,
                        m_ref, l_ref, acc_ref, *, past, tk):
    j = pl.program_id(1)
    tq = q_ref.shape[1]

    @pl.when(j == 0)
    def _():
        _init_softmax_state(m_ref, l_ref, acc_ref)

    def attend(n, k_start, n_valid, scores, values):
        q_pos = past + lax.broadcasted_iota(jnp.int32, (tq, n), 0)
        k_pos = k_start + lax.broadcasted_iota(jnp.int32, (tq, n), 1)
        dist = jnp.abs(q_pos - k_pos).astype(F32)
        dist = jnp.concatenate([dist, dist], axis=0)
        visible = ((q_pos // CHUNK) >= (k_pos // CHUNK)) & (k_pos < k_start + n_valid)
        visible = jnp.concatenate([visible, visible], axis=0)
        state = [(m_ref[h], l_ref[h], acc_ref[h]) for h in range(N_HEADS)]
        for h in range(N_HEADS):
            m_prev, l_prev, acc_prev = state[h]
            q2 = jnp.concatenate(_split_maps(q_ref[0, :, h * LANES:(h + 1) * LANES]), axis=0)
            s = jnp.where(visible, scores(q2, h) - (slopes_ref[h] * LOG2E) * dist, NEG_INF)
            m_new = jnp.maximum(m_prev, jnp.max(s, axis=1, keepdims=True))
            alpha = jnp.exp2(m_prev - m_new)
            ps = [jnp.exp2(s[:, b * LANES:(b + 1) * LANES] - m_new) for b in range(n // LANES)]
            p = jnp.concatenate(ps, axis=1).astype(BF16)
            state[h] = (m_new, alpha * l_prev + functools.reduce(lambda x, y: x + y, ps),
                        alpha * acc_prev + _dot(p, values(h)))
        for h in range(N_HEADS):
            m_ref[h], l_ref[h], acc_ref[h] = state[h]

    attend(tk, j * tk, tk,
           lambda q2, h: _dot(q2, ckt_ref[0, h * LANES:(h + 1) * LANES, :].astype(BF16)),
           lambda h: cv_ref[0, pl.ds(h, tk, stride=N_HEADS), :].astype(BF16))

    @pl.when(j == pl.num_programs(1) - 1)
    def _():
        attend(kn_ref.shape[1], past, tq,
               lambda q2, h: _dot_nt(q2, kn_ref[0, :, h * LANES:(h + 1) * LANES]),
               lambda h: vn_ref[0, :, h * LANES:(h + 1) * LANES])
        lam = (jnp.exp(jnp.sum(lq1[...] * lk1[...], axis=1, keepdims=True))
               - jnp.exp(jnp.sum(lq2[...] * lk2[...], axis=1, keepdims=True)) + LAMBDA_INIT)
        for h in range(N_HEADS):
            sm = acc_ref[h] / jnp.sum(l_ref[h], axis=1, keepdims=True)
            o = _rms(sm[:tq] - lam * sm[tq:], g_ref[...]) * (1.0 - LAMBDA_INIT)
            o_ref[0, :, h * LANES:(h + 1) * LANES] = o.astype(o_ref.dtype)


def _attn_sample(slopes, q, cache_kt, cache_v_rows, k_new, v_new, lq1, lk1, lq2, lk2, g):
    b, t, _ = q.shape
    past = cache_kt.shape[2]
    tk = min(CACHE_TILE, past)
    k_new = jnp.pad(k_new, ((0, 0), (0, -t % LANES), (0, 0)))
    v_new = jnp.pad(v_new, ((0, 0), (0, -t % LANES), (0, 0)))
    small = lambda a: pl.BlockSpec(a.shape, lambda bi, j: (0, 0))
    per_stream = lambda a: pl.BlockSpec((1,) + a.shape[1:], lambda bi, j: (bi, 0, 0))
    return pl.pallas_call(
        functools.partial(_attn_sample_kernel, past=past, tk=tk),
        grid=(b, past // tk),
        in_specs=[pl.BlockSpec(memory_space=pltpu.SMEM),
                  per_stream(q),
                  pl.BlockSpec((1, D_ATT, tk), lambda bi, j: (bi, 0, j)),
                  pl.BlockSpec((1, tk * N_HEADS, D_HEAD_V), lambda bi, j: (bi, j, 0)),
                  per_stream(k_new), per_stream(v_new),
                  small(lq1), small(lk1), small(lq2), small(lk2), small(g)],
        out_specs=pl.BlockSpec((1, t, D_ATT), lambda bi, j: (bi, 0, 0)),
        out_shape=jax.ShapeDtypeStruct((b, t, D_ATT), BF16),
        scratch_shapes=[pltpu.VMEM((N_HEADS, 2 * t, LANES), F32)] * 3,
        compiler_params=pltpu.CompilerParams(dimension_semantics=("parallel", "arbitrary"), vmem_limit_bytes=VMEM_LIMIT),
        name="attn_sample",
    )(slopes, q, cache_kt, cache_v_rows, k_new, v_new, lq1, lk1, lq2, lk2, g)


def _block_diag_dense(w):
    n, c, _ = w.shape
    eye = jnp.eye(n, dtype=w.dtype)
    return (eye[:, None, :, None] * w[:, :, None, :]).reshape(n * c, n * c)


def kernel(x_prompt, x_sample, cache_k, cache_v, state_lru_h, state_conv, w_in, w_out, lambda_q1, lambda_k1,
           lambda_q2, lambda_k2, subln_g, conv_w, conv_b, w_rgate, b_rgate, w_igate, b_igate, lru_lambda,
           ffn1_w_gate, ffn1_w_up, ffn1_w_down, ffn2_w_gate, ffn2_w_up, ffn2_w_down,
           g_ffn1_pre, g_ffn1_post, g_mix_pre, g_mix_post, g_ffn2_pre, g_ffn2_post):
    bp, tp, _ = x_prompt.shape
    bs, ts, _ = x_sample.shape
    past = cache_k.shape[2]
    assert w_in.shape[0] == 1, "one layer: LAMBDA_INIT is the depth-0 value"
    assert tp % ROW_TILE == 0 and (bp * tp) % OUT_ROW_TILE == 0
    assert bs * ts <= ROW_TILE and ts % (2 * SUBLANES) == 0
    assert past % min(CACHE_TILE, past) == 0

    wg1, wu1, wd1 = ffn1_w_gate[0].astype(BF16), ffn1_w_up[0].astype(BF16), ffn1_w_down[0].astype(BF16)
    win = w_in[0].astype(BF16)
    wq, wkt, wv, wl = win[:, :D_ATT], win[:, D_ATT:2 * D_ATT].T, win[:, 2 * D_ATT:3 * D_ATT], win[:, 3 * D_ATT:]
    wr = _block_diag_dense(w_rgate[0]).astype(BF16)
    wi = _block_diag_dense(w_igate[0]).astype(BF16)
    slopes = jnp.asarray(2.0 ** (-8.0 * np.arange(1, N_HEADS + 1) / N_HEADS), dtype=F32)
    lam_vecs = (lambda_q1, lambda_k1, lambda_q2, lambda_k2)
    ffn1 = (g_ffn1_pre, g_ffn1_post, g_mix_pre, wg1, wu1, wd1, wq, wkt, wv, wl)
    lru_params = (conv_w[0], conv_b, wr, b_rgate, wi, b_igate, lru_lambda)
    tail_pad = 8 - (CONV_W - 1)

    def conv_state(conv_hist, lx, b, t):
        return jnp.concatenate([conv_hist, lx.reshape(b, t, D_LRU)], axis=1)[:, -(CONV_W - 1):][None]

    def pad_hist(conv_hist):
        return jnp.pad(conv_hist, ((0, 0), (tail_pad, 0), (0, 0)))

    hist_p = jnp.zeros((bp, CONV_W - 1, D_LRU), F32)
    x1, qb, kt, ktb, v_rows, vb, lx, lru_out, hp, wg2, wu2, wd2, wo = _ffn_inproj(
        x_prompt.reshape(bp * tp, D_MODEL), *ffn1, seq_len=tp,
        lru_state=(jnp.zeros((bp, 1, D_LRU), F32), pad_hist(hist_p)), lru_params=lru_params,
        later_weights=(ffn2_w_gate[0], ffn2_w_up[0], ffn2_w_down[0], w_out[0]))
    ffn2 = (wo, g_mix_post, g_ffn2_pre, g_ffn2_post, wg2, wu2, wd2)
    att = _attn_prompt(slopes, qb.reshape(bp, tp, D_ATT), ktb, vb.reshape(bp, tp, D_ATT), *lam_vecs, subln_g)
    yp = _outproj_ffn(att.reshape(bp * tp, D_ATT), lru_out, x1, *ffn2).reshape(bp, tp, D_MODEL)
    hp, cp = hp.reshape(1, bp, D_LRU), conv_state(hist_p, lx, bp, tp)
    kp = kt.reshape(bp, N_HEADS, 2, D_HEAD_QK, tp).transpose(0, 4, 1, 2, 3)[None]
    vp = v_rows.reshape(1, bp, tp, N_HEADS, D_HEAD_V)

    x1, qb, kf, kb, vf, vb, lx, lg = _ffn_inproj(x_sample.reshape(bs * ts, D_MODEL), *ffn1)
    cache_kt = cache_k[0].transpose(0, 2, 3, 4, 1).reshape(bs, D_ATT, past)
    cache_v_rows = cache_v[0].reshape(bs, past * N_HEADS, D_HEAD_V)
    att = _attn_sample(slopes, qb.reshape(bs, ts, D_ATT), cache_kt, cache_v_rows, kb.reshape(bs, ts, D_ATT),
                       vb.reshape(bs, ts, D_ATT), *lam_vecs, subln_g)
    lru_out, hs = _lru(lx.reshape(bs, ts, D_LRU), lg.reshape(bs, ts, D_LRU), state_lru_h[0].reshape(bs, 1, D_LRU),
                       pad_hist(state_conv[0]), *lru_params)
    ys = _outproj_ffn(att.reshape(bs * ts, D_ATT), lru_out.reshape(bs * ts, D_LRU), x1, *ffn2)
    ys, hs, cs = ys.reshape(bs, ts, D_MODEL), hs.reshape(1, bs, D_LRU), conv_state(state_conv[0], lx, bs, ts)
    ks = kf.reshape(1, bs, ts, N_HEADS, 2, D_HEAD_QK)
    vs = vf.reshape(1, bs, ts, N_HEADS, D_HEAD_V)
    return (yp, ys, kp, vp, hp, cp, ks, vs, hs, cs)
```
